```python
import jax, jax.numpy as jnp
from jax import lax
import numpy as np

D_MODEL = 1024
BATCH = 8
SEQ = 2048
DEPTH = 1
DEC_BATCH = 128
DEC_SEQ = 8
PAST_LEN = 16384
PAGE_SIZE = 128

GDN_HEADS = 8
GDN_DK = 128
GDN_DV = 128
GDN_QK = GDN_HEADS * GDN_DK
GDN_VW = GDN_HEADS * GDN_DV
QKV_W = 2 * GDN_QK + GDN_VW
GDN_CONV = 4
GDN_CHUNK = 64
SC_WIDTH = D_MODEL
SC_CONV = 3
N_EXPERTS = 32
TOP_K = 4
D_EXPERT = D_MODEL
SWIGLU_LIMIT = 7.0
SWIGLU_ALPHA = 1.702
MOE_BLOCK = 128
NORM_EPS = 1e-6
IN_SIZES = (QKV_W, GDN_VW, GDN_HEADS, GDN_HEADS, SC_WIDTH, SC_WIDTH, SC_WIDTH, D_MODEL, D_MODEL)
IN_WIDTH = 2 * GDN_QK + 2 * GDN_VW + 2 * GDN_HEADS + 3 * SC_WIDTH + 2 * D_MODEL

kernel_name = "hybrid_gdn_shortconv_moe_adaln_step"


def rms_norm(x, w):
    xf = x.astype(jnp.float32)
    y = xf * lax.rsqrt(jnp.mean(xf * xf, axis=-1, keepdims=True) + NORM_EPS)
    return (y * w.astype(jnp.float32)).astype(x.dtype)


def l2_norm(x):
    return x * lax.rsqrt(jnp.sum(x * x, axis=-1, keepdims=True) + 1e-6)


def causal_dwconv(x_ext, w):
    ch = w.shape[1]
    return lax.conv_general_dilated(x_ext, w[:, None, :].astype(x_ext.dtype), window_strides=(1,),
                                    padding='VALID', dimension_numbers=('NWC', 'WIO', 'NWC'),
                                    feature_group_count=ch)


def split_cols(a, sizes):
    out, off = [], 0
    for s in sizes:
        out.append(a[..., off:off + s])
        off += s
    return out


def gated_delta_chunked(q, k, v, beta, g, s0):
    Bn, T, H, DK = q.shape
    C = min(GDN_CHUNK, T)
    n = -(-T // C)
    pad = n * C - T
    if pad:
        pw = ((0, 0), (0, pad), (0, 0), (0, 0))
        q, k, v = jnp.pad(q, pw), jnp.pad(k, pw), jnp.pad(v, pw)
        beta, g = jnp.pad(beta, pw[:3]), jnp.pad(g, pw[:3])

    def to_chunks(a):
        a = a.reshape((Bn, n, C) + a.shape[2:])
        return jnp.moveaxis(a, (1, 3), (0, 2))

    qc, kc, vc = to_chunks(q), to_chunks(k), to_chunks(v)
    bc, gc = to_chunks(beta), to_chunks(g)
    dec = jnp.cumsum(gc, axis=-1)
    idx = jnp.arange(C)
    causal = idx[:, None] >= idx[None, :]
    strict = idx[:, None] > idx[None, :]
    decay_mat = jnp.exp(jnp.where(causal, dec[..., :, None] - dec[..., None, :], -jnp.inf))
    kk = jnp.einsum('nbhid,nbhjd->nbhij', kc, kc)
    a_mat = jnp.where(strict, bc[..., :, None] * kk * decay_mat, 0.0)
    lhs = a_mat + jnp.eye(C, dtype=jnp.float32)
    u_v = lax.linalg.triangular_solve(lhs, bc[..., None] * vc, left_side=True, lower=True,
                                      unit_diagonal=True)
    w_k = lax.linalg.triangular_solve(lhs, (bc * jnp.exp(dec))[..., None] * kc, left_side=True,
                                      lower=True, unit_diagonal=True)
    qk = jnp.einsum('nbhid,nbhjd->nbhij', qc, kc) * decay_mat
    q_dec = qc * jnp.exp(dec)[..., None]
    k_tail = kc * jnp.exp(dec[..., -1:] - dec)[..., None]
    chunk_decay = jnp.exp(dec[..., -1])

    def step(S, xs):
        uv, wk, qkm, qd, kt, cd = xs
        u = uv - jnp.einsum('bhcd,bhde->bhce', wk, S)
        o = jnp.einsum('bhcd,bhde->bhce', qd, S) + jnp.einsum('bhij,bhje->bhie', qkm, u)
        S = S * cd[..., None, None] + jnp.einsum('bhcd,bhce->bhde', kt, u)
        return S, o

    s_final, o = lax.scan(step, s0, (u_v, w_k, qk, q_dec, k_tail, chunk_decay))
    o = jnp.transpose(o, (1, 0, 3, 2, 4)).reshape(Bn, n * C, H, o.shape[-1])[:, :T]
    return o, s_final


def moe_ffn(xf, w_router, b_router, w_gate, b_gate, w_up, b_up, w_down, b_down):
    N, D = xf.shape
    logits = xf.astype(jnp.float32) @ w_router.astype(jnp.float32) + b_router.astype(jnp.float32)
    top_val, top_idx = lax.top_k(logits, TOP_K)
    top_w = jax.nn.softmax(top_val, axis=-1)
    M = N * TOP_K
    flat_e = top_idx.reshape(M).astype(jnp.int32)
    flat_tok = jnp.arange(M, dtype=jnp.int32) // TOP_K
    order = jnp.argsort(flat_e, stable=True)
    e_sorted, tok_sorted = flat_e[order], flat_tok[order]
    w_sorted = top_w.reshape(M)[order]
    counts = jnp.zeros((N_EXPERTS,), jnp.int32).at[flat_e].add(1)
    padded = (counts + MOE_BLOCK - 1) // MOE_BLOCK * MOE_BLOCK
    pad_end = jnp.cumsum(padded)
    pad_start = pad_end - padded
    start = jnp.cumsum(counts) - counts
    dest = pad_start[e_sorted] + jnp.arange(M, dtype=jnp.int32) - start[e_sorted]
    n_blocks = -(-M // MOE_BLOCK) + N_EXPERTS
    buf = jnp.zeros((n_blocks * MOE_BLOCK, D), xf.dtype).at[dest].set(xf[tok_sorted])
    block_e = jnp.minimum(jnp.searchsorted(pad_end, jnp.arange(n_blocks, dtype=jnp.int32) * MOE_BLOCK,
                                           side='right'), N_EXPERTS - 1).astype(jnp.int32)

    def expert_block(args):
        xb, e = args
        gate = xb @ w_gate[e] + b_gate[e]
        up = xb @ w_up[e] + b_up[e]
        gate = jnp.minimum(gate, SWIGLU_LIMIT)
        up = jnp.clip(up, -SWIGLU_LIMIT, SWIGLU_LIMIT)
        glu = gate * jax.nn.sigmoid(SWIGLU_ALPHA * gate)
        return ((up + 1.0) * glu) @ w_down[e] + b_down[e]

    out = lax.map(expert_block, (buf.reshape(n_blocks, MOE_BLOCK, D), block_e))
    rows = out.reshape(n_blocks * MOE_BLOCK, D)[dest]
    y = jnp.zeros((N, D), jnp.float32).at[tok_sorted].add(rows.astype(jnp.float32) * w_sorted[:, None])
    return y.astype(xf.dtype)


def hybrid_layer(x, c, conv_state, rec_state, sc_state, w_ada, b_ada, norm1_w, w_in, gdn_conv_w,
                 gdn_a_log, gdn_dt_bias, gdn_norm_w, w_branch_a, sc_conv_w, w_branch_b, w_out,
                 norm2_w, w_router, b_router, w_gate, b_gate, w_up, b_up, w_down, b_down):
    Bn, T, D = x.shape
    dt = x.dtype
    f32 = jnp.float32
    ada = (jax.nn.silu(c) @ w_ada + b_ada)[:, None, :]
    sh1, sc1, g1, sh2, sc2, g2 = jnp.split(ada, 6, axis=-1)
    h = rms_norm(x, norm1_w) * (1 + sc1) + sh1
    proj = h @ w_in
    qkv, z, b_lin, a_lin, sc_b, sc_c, sc_h, gate_a, gate_b = split_cols(proj, IN_SIZES)

    qkv_ext = jnp.concatenate([conv_state.astype(dt), qkv], axis=1)
    new_conv = qkv_ext[:, -(GDN_CONV - 1):]
    qkv_c = jax.nn.silu(causal_dwconv(qkv_ext, gdn_conv_w)).astype(f32)
    q = qkv_c[..., :GDN_QK].reshape(Bn, T, GDN_HEADS, GDN_DK)
    k = qkv_c[..., GDN_QK:2 * GDN_QK].reshape(Bn, T, GDN_HEADS, GDN_DK)
    v = qkv_c[..., 2 * GDN_QK:].reshape(Bn, T, GDN_HEADS, GDN_DV)
    q = l2_norm(q) * (GDN_DK ** -0.5)
    k = l2_norm(k)
    beta = jax.nn.sigmoid(b_lin.astype(f32))
    g = -jnp.exp(gdn_a_log.astype(f32)) * jax.nn.softplus(a_lin.astype(f32) + gdn_dt_bias.astype(f32))
    o, new_rec = gated_delta_chunked(q, k, v, beta, g, rec_state.astype(f32))
    o = o * lax.rsqrt(jnp.mean(o * o, axis=-1, keepdims=True) + NORM_EPS) * gdn_norm_w.astype(f32)
    o = o * jax.nn.silu(z.astype(f32).reshape(Bn, T, GDN_HEADS, GDN_DV))
    y_a = o.reshape(Bn, T, GDN_VW).astype(dt) @ w_branch_a

    pre = sc_c * sc_h
    pre_ext = jnp.concatenate([sc_state.astype(dt), pre], axis=1)
    new_sc = pre_ext[:, -(SC_CONV - 1):]
    y_b = (sc_b * causal_dwconv(pre_ext, sc_conv_w)) @ w_branch_b

    merged = jax.nn.sigmoid(gate_a) * y_a + jax.nn.sigmoid(gate_b) * y_b
    x = x + g1 * (merged @ w_out)

    h2 = rms_norm(x, norm2_w) * (1 + sc2) + sh2
    ffn = moe_ffn(h2.reshape(Bn * T, D), w_router, b_router, w_gate, b_gate, w_up, b_up, w_down, b_down)
    x = x + g2 * ffn.reshape(Bn, T, D)
    return x, new_conv.astype(conv_state.dtype), new_rec.astype(rec_state.dtype), new_sc.astype(sc_state.dtype)


def setup_inputs(seed: int = 0) -> dict:
    key = jax.random.key(seed)
    ks = jax.random.split(key, 40)
    f32 = jnp.float32

    def nrm(k, shape, scale):
        return jax.random.normal(k, shape, f32) * scale

    L, D, E, F = DEPTH, D_MODEL, N_EXPERTS, D_EXPERT
    dt0 = jnp.exp(jax.random.uniform(ks[11], (L, GDN_HEADS), f32, np.log(0.001), np.log(0.1)))
    return {
        'x_prompt': nrm(ks[0], (BATCH, SEQ, D), 1.0),
        'x_sample': nrm(ks[1], (DEC_BATCH, DEC_SEQ, D), 1.0),
        'c_prompt': nrm(ks[2], (BATCH, D), 1.0),
        'c_sample': nrm(ks[3], (DEC_BATCH, D), 1.0),
        'state_gdn_conv': nrm(ks[4], (L, DEC_BATCH, GDN_CONV - 1, QKV_W), 1.0),
        'state_gdn_rec': nrm(ks[5], (L, DEC_BATCH, GDN_HEADS, GDN_DK, GDN_DV), 0.1),
        'state_sc_conv': nrm(ks[6], (L, DEC_BATCH, SC_CONV - 1, SC_WIDTH), 1.0),
        'w_ada': nrm(ks[7], (L, D, 6 * D), D ** -0.5),
        'b_ada': nrm(ks[8], (L, 6 * D), 0.02),
        'norm1_w': 1.0 + nrm(ks[9], (L, D), 0.02),
        'w_in': nrm(ks[10], (L, D, IN_WIDTH), D ** -0.5),
        'gdn_conv_w': nrm(ks[12], (L, GDN_CONV, QKV_W), GDN_CONV ** -0.5),
        'gdn_a_log': jnp.log(jax.random.uniform(ks[13], (L, GDN_HEADS), f32, 1.0, 16.0)),
        'gdn_dt_bias': dt0 + jnp.log(-jnp.expm1(-dt0)),
        'gdn_norm_w': 1.0 + nrm(ks[14], (L, GDN_DV), 0.02),
        'w_branch_a': nrm(ks[15], (L, GDN_VW, D), GDN_VW ** -0.5),
        'sc_conv_w': nrm(ks[16], (L, SC_CONV, SC_WIDTH), SC_CONV ** -0.5),
        'w_branch_b': nrm(ks[17], (L, SC_WIDTH, D), SC_WIDTH ** -0.5),
        'w_out': nrm(ks[18], (L, D, D), D ** -0.5),
        'norm2_w': 1.0 + nrm(ks[19], (L, D), 0.02),
        'w_router': nrm(ks[20], (L, D, E), D ** -0.5),
        'b_router': nrm(ks[21], (L, E), 0.01),
        'w_gate': nrm(ks[22], (L, E, D, F), D ** -0.5),
        'b_gate': nrm(ks[23], (L, E, F), 0.02),
        'w_up': nrm(ks[24], (L, E, D, F), D ** -0.5),
        'b_up': nrm(ks[25], (L, E, F), 0.02),
        'w_down': nrm(ks[26], (L, E, F, D), F ** -0.5),
        'b_down': nrm(ks[27], (L, E, D), 0.02),
        'final_norm_w': 1.0 + nrm(ks[28], (D,), 0.02),
    }


def reference(x_prompt, x_sample, c_prompt, c_sample, state_gdn_conv, state_gdn_rec, state_sc_conv,
              w_ada, b_ada, norm1_w, w_in, gdn_conv_w, gdn_a_log, gdn_dt_bias, gdn_norm_w, w_branch_a,
              sc_conv_w, w_branch_b, w_out, norm2_w, w_router, b_router, w_gate, b_gate, w_up, b_up,
              w_down, b_down, final_norm_w):
    bp = x_prompt.shape[0]
    zero_conv = jnp.zeros((bp, GDN_CONV - 1, QKV_W), state_gdn_conv.dtype)
    zero_rec = jnp.zeros((bp, GDN_HEADS, GDN_DK, GDN_DV), state_gdn_rec.dtype)
    zero_sc = jnp.zeros((bp, SC_CONV - 1, SC_WIDTH), state_sc_conv.dtype)
    xp, xs = x_prompt, x_sample
    pc, pr, ps, sc_, sr, ss = [], [], [], [], [], []
    for l in range(DEPTH):
        lp = (w_ada[l], b_ada[l], norm1_w[l], w_in[l], gdn_conv_w[l], gdn_a_log[l], gdn_dt_bias[l],
              gdn_norm_w[l], w_branch_a[l], sc_conv_w[l], w_branch_b[l], w_out[l], norm2_w[l],
              w_router[l], b_router[l], w_gate[l], b_gate[l], w_up[l], b_up[l], w_down[l], b_down[l])
        xp, c1, r1, s1 = hybrid_layer(xp, c_prompt, zero_conv, zero_rec, zero_sc, *lp)
        xs, c2, r2, s2 = hybrid_layer(xs, c_sample, state_gdn_conv[l], state_gdn_rec[l], state_sc_conv[l], *lp)
        pc.append(c1); pr.append(r1); ps.append(s1)
        sc_.append(c2); sr.append(r2); ss.append(s2)
    y_prompt = rms_norm(xp, final_norm_w)
    y_sample = rms_norm(xs, final_norm_w)
    return (y_prompt, y_sample, jnp.stack(pc), jnp.stack(pr), jnp.stack(ps),
            jnp.stack(sc_), jnp.stack(sr), jnp.stack(ss))
```

```python
import functools

import jax
import jax.numpy as jnp
from jax import lax
from jax.experimental import pallas as pl
from jax.experimental.pallas import tpu as pltpu

F32 = jnp.float32
BF16 = jnp.bfloat16
I32 = jnp.int32
HIGHEST = lax.Precision.HIGHEST

D_MODEL = 1024
GDN_HEADS = 8
GDN_DK = 128
GDN_DV = 128
GDN_QK = GDN_HEADS * GDN_DK
QKV_W = 3 * GDN_QK
GDN_CONV = 4
GDN_CHUNK = 64
SC_CONV = 3
N_EXPERTS = 32
TOP_K = 4
SWIGLU_LIMIT = 7.0
SWIGLU_ALPHA = 1.702
NORM_EPS = 1e-6
N_GATE_COLS = 2 * GDN_HEADS
PROJ_MAIN_W = 9 * D_MODEL
ROUTER_LANES = 128
EXPERT_BLOCK = 256
SUBLANES = 8
VMEM_LIMIT = 56 * 1024 * 1024

_NT = (((1,), (1,)), ((), ()))
_TN = (((0,), (0,)), ((), ()))


def _bdot(a, b):
    return jnp.dot(a.astype(BF16), b.astype(BF16), preferred_element_type=F32)


def _bdot_nt(a, b):
    return lax.dot_general(a.astype(BF16), b.astype(BF16), _NT, preferred_element_type=F32)


def _bdot_tn(a, b):
    return lax.dot_general(a.astype(BF16), b.astype(BF16), _TN, preferred_element_type=F32)


def _silu(x):
    return x * jax.nn.sigmoid(x)


def _softplus(x):
    return jnp.maximum(x, 0.0) + jnp.log1p(jnp.exp(-jnp.abs(x)))


def _seq_tile(n_seq, seq_len, target):
    if seq_len >= target:
        assert seq_len % target == 0
        return 1, target
    sb = min(n_seq, target // seq_len)
    assert n_seq % sb == 0 and seq_len % SUBLANES == 0
    return sb, seq_len


def _params(sem):
    return pltpu.CompilerParams(dimension_semantics=sem, vmem_limit_bytes=VMEM_LIMIT)


def _ada_body(c_ref, w_ref, b_ref, o_ref):
    o_ref[...] = _bdot(_silu(c_ref[...]), w_ref[...]) + b_ref[...]


def _ada(c, w_ada, b_ada):
    rows, d = c.shape
    n = w_ada.shape[1]
    tn = 1024
    return pl.pallas_call(
        _ada_body,
        grid=(n // tn,),
        in_specs=[pl.BlockSpec((rows, d), lambda j: (0, 0)),
                  pl.BlockSpec((d, tn), lambda j: (0, j)),
                  pl.BlockSpec((1, tn), lambda j: (0, j))],
        out_specs=pl.BlockSpec((rows, tn), lambda j: (0, j)),
        out_shape=jax.ShapeDtypeStruct((rows, n), F32),
        compiler_params=_params(("arbitrary",)),
        name="ada",
    )(c, w_ada, b_ada.reshape(1, n))


def _gates(v, a_log, dt_bias, axis):
    is_beta = lax.broadcasted_iota(I32, v.shape, axis) < GDN_HEADS
    beta = jax.nn.sigmoid(v)
    g = -jnp.exp(a_log) * _softplus(v + dt_bias)
    return jnp.where(is_beta, beta, g)


def _proj_body(x_ref, sc_ref, sh_ref, nw_ref, w_ref, wg_ref, wgt_ref, prow_ref, pcol_ref,
               o_ref, gc_ref, gr_ref, h_scr):
    @pl.when(pl.program_id(2) == 0)
    def _():
        x = x_ref[...]
        y = x * lax.rsqrt(jnp.mean(x * x, axis=-1, keepdims=True) + NORM_EPS) * nw_ref[...]
        h = (y * (1.0 + sc_ref[...]) + sh_ref[...]).reshape(h_scr.shape).astype(BF16)
        h_scr[...] = h
        gc = jnp.dot(h, wg_ref[...], preferred_element_type=F32)
        gr = lax.dot_general(wgt_ref[...], h, _NT, preferred_element_type=F32)
        gc_ref[...] = _gates(gc, prow_ref[0:1, :], prow_ref[1:2, :], 1)
        gr_ref[...] = _gates(gr, pcol_ref[:, 0:1], pcol_ref[:, 1:2], 0)

    o_ref[...] = jnp.dot(h_scr[...], w_ref[...], preferred_element_type=F32)


def _proj(x, sc, sh, norm_w, w_main, w_g, w_gt, p_row, p_col):
    n_seq, seq_len, d = x.shape
    sb, tb = _seq_tile(n_seq, seq_len, 1024)
    tm = sb * tb
    tn = 1024
    nt = seq_len // tb
    n_tok = n_seq * seq_len
    row = lambda s, t, j: s * nt + t
    return pl.pallas_call(
        _proj_body,
        grid=(n_seq // sb, nt, PROJ_MAIN_W // tn),
        in_specs=[pl.BlockSpec((sb, tb, d), lambda s, t, j: (s, t, 0)),
                  pl.BlockSpec((sb, 1, d), lambda s, t, j: (s, 0, 0)),
                  pl.BlockSpec((sb, 1, d), lambda s, t, j: (s, 0, 0)),
                  pl.BlockSpec((1, d), lambda s, t, j: (0, 0)),
                  pl.BlockSpec((d, tn), lambda s, t, j: (0, j)),
                  pl.BlockSpec((d, N_GATE_COLS), lambda s, t, j: (0, 0)),
                  pl.BlockSpec((N_GATE_COLS, d), lambda s, t, j: (0, 0)),
                  pl.BlockSpec((2, N_GATE_COLS), lambda s, t, j: (0, 0)),
                  pl.BlockSpec((N_GATE_COLS, 2), lambda s, t, j: (0, 0))],
        out_specs=[pl.BlockSpec((tm, tn), lambda s, t, j: (row(s, t, j), j)),
                   pl.BlockSpec((tm, N_GATE_COLS), lambda s, t, j: (row(s, t, j), 0)),
                   pl.BlockSpec((N_GATE_COLS, tm), lambda s, t, j: (0, row(s, t, j)))],
        out_shape=[jax.ShapeDtypeStruct((n_tok, PROJ_MAIN_W), F32),
                   jax.ShapeDtypeStruct((n_tok, N_GATE_COLS), F32),
                   jax.ShapeDtypeStruct((N_GATE_COLS, n_tok), F32)],
        scratch_shapes=[pltpu.VMEM((tm, d), BF16)],
        compiler_params=_params(("arbitrary", "arbitrary", "arbitrary")),
        name="proj",
    )(x, sc, sh, norm_w, w_main, w_g, w_gt, p_row, p_col)


def _shift_rows(x, hist, s, per_seq):
    rows, width = x.shape
    if per_seq:
        src = pltpu.roll(hist, (rows - SUBLANES + s) % rows, 0) if rows > SUBLANES else pltpu.roll(hist, s, 0)
        xr = pltpu.roll(x, s, 0)
        r = lax.broadcasted_iota(I32, (rows, width), 0) & (SUBLANES - 1)
        return jnp.where(r < s, src, xr)
    xr = pltpu.roll(x, s, 0)
    hr = pltpu.roll(hist, s, 0)
    r = lax.broadcasted_iota(I32, (SUBLANES, width), 0)
    head = jnp.where(r < s, hr, xr[:SUBLANES])
    if rows == SUBLANES:
        return head
    return jnp.concatenate([head, xr[SUBLANES:]], axis=0)


def _causal_conv(x, hist, w, per_seq):
    taps = w.shape[0]
    acc = x * w[taps - 1:taps, :]
    for s in range(1, taps):
        acc = acc + _shift_rows(x, hist, s, per_seq) * w[taps - 1 - s:taps - s, :]
    return acc


def _gdn_body(qkv_ref, prev_ref, conv0_ref, z_ref, gc_ref, gr_ref, s0_ref, cw_ref, nw_ref,
              o_ref, s_ref, *, chunk):
    t = pl.program_id(1)

    @pl.when(t == 0)
    def _():
        s_ref[...] = s0_ref[...]

    x = qkv_ref[...]
    hist = jnp.where(t == 0, conv0_ref[0], prev_ref[...])
    qkvc = _silu(_causal_conv(x, hist, cw_ref[...], False))

    gcol = gc_ref[...]
    grow = gr_ref[0]
    ri = lax.broadcasted_iota(I32, (chunk, chunk), 0)
    ci = lax.broadcasted_iota(I32, (chunk, chunk), 1)
    causal = ri >= ci
    strict = ri > ci
    tri = causal.astype(F32)
    eye = (ri == ci).astype(F32)
    dec_col = jnp.dot(tri, gcol[:, GDN_HEADS:], precision=HIGHEST, preferred_element_type=F32)
    dec_row = lax.dot_general(grow[GDN_HEADS:, :], tri, _NT, precision=HIGHEST, preferred_element_type=F32)
    n_lvl = chunk.bit_length() - 1
    blk = [lax.shift_right_logical(ri, l) == lax.shift_right_logical(ci, l) for l in range(1, n_lvl + 1)]
    pair = [blk[l] & jnp.logical_not(blk[l - 1]) for l in range(1, n_lvl)]

    for h in range(GDN_HEADS):
        q = qkvc[:, h * GDN_DK:(h + 1) * GDN_DK]
        k = qkvc[:, GDN_QK + h * GDN_DK:GDN_QK + (h + 1) * GDN_DK]
        v = qkvc[:, 2 * GDN_QK + h * GDN_DV:2 * GDN_QK + (h + 1) * GDN_DV]
        q = q * lax.rsqrt(jnp.sum(q * q, axis=-1, keepdims=True) + 1e-6) * (GDN_DK ** -0.5)
        k = k * lax.rsqrt(jnp.sum(k * k, axis=-1, keepdims=True) + 1e-6)
        beta = gcol[:, h:h + 1]
        dcol = dec_col[:, h:h + 1]
        drow = dec_row[h:h + 1, :]
        dlast = dec_col[chunk - 1:chunk, h:h + 1]
        decay = jnp.where(causal, jnp.exp(dcol - drow), 0.0)
        kk = _bdot_nt(k, k)
        a = jnp.where(strict, beta * kk * decay, 0.0)
        inv = eye - jnp.where(blk[0], a, 0.0)
        for lower_left in pair:
            inv = inv - _bdot(inv, _bdot(jnp.where(lower_left, a, 0.0), inv))
        edec = jnp.exp(dcol)
        rhs = jnp.concatenate([beta * v, (beta * edec) * k], axis=1)
        sol = _bdot(inv, rhs)
        u_v = sol[:, :GDN_DV]
        w_k = sol[:, GDN_DV:]
        qk = _bdot_nt(q, k) * decay
        state = s_ref[0, h]
        ws = _bdot(jnp.concatenate([w_k, q * edec], axis=0), state)
        u = u_v - ws[:chunk]
        o = ws[chunk:] + _bdot(qk, u)
        k_tail = k * jnp.exp(dlast - dcol)
        s_ref[0, h] = state * jnp.exp(dlast) + _bdot_tn(k_tail, u)
        o = o * lax.rsqrt(jnp.mean(o * o, axis=-1, keepdims=True) + NORM_EPS) * nw_ref[...]
        o_ref[:, h * GDN_DV:(h + 1) * GDN_DV] = o * _silu(z_ref[:, h * GDN_DV:(h + 1) * GDN_DV])


def _gdn(proj, gcol, grow, conv0, state0, conv_w, norm_w, n_seq, seq_len):
    chunk = min(GDN_CHUNK, seq_len)
    assert seq_len % chunk == 0 and chunk % SUBLANES == 0
    nt = seq_len // chunk
    n_tok = n_seq * seq_len
    cpb = chunk // SUBLANES
    row = lambda s, t: s * nt + t
    grow_chunks = grow.reshape(N_GATE_COLS, n_tok // chunk, chunk).transpose(1, 0, 2)
    return pl.pallas_call(
        functools.partial(_gdn_body, chunk=chunk),
        grid=(n_seq, nt),
        in_specs=[pl.BlockSpec((chunk, QKV_W), lambda s, t: (row(s, t), 0)),
                  pl.BlockSpec((SUBLANES, QKV_W), lambda s, t: (jnp.maximum(row(s, t) * cpb - 1, 0), 0)),
                  pl.BlockSpec((1, SUBLANES, QKV_W), lambda s, t: (s, 0, 0)),
                  pl.BlockSpec((chunk, D_MODEL), lambda s, t: (row(s, t), 3)),
                  pl.BlockSpec((chunk, N_GATE_COLS), lambda s, t: (row(s, t), 0)),
                  pl.BlockSpec((1, N_GATE_COLS, chunk), lambda s, t: (row(s, t), 0, 0)),
                  pl.BlockSpec((1, GDN_HEADS, GDN_DK, GDN_DV), lambda s, t: (s, 0, 0, 0)),
                  pl.BlockSpec((GDN_CONV, QKV_W), lambda s, t: (0, 0)),
                  pl.BlockSpec((1, GDN_DV), lambda s, t: (0, 0))],
        out_specs=[pl.BlockSpec((chunk, D_MODEL), lambda s, t: (row(s, t), 0)),
                   pl.BlockSpec((1, GDN_HEADS, GDN_DK, GDN_DV), lambda s, t: (s, 0, 0, 0))],
        out_shape=[jax.ShapeDtypeStruct((n_tok, D_MODEL), F32),
                   jax.ShapeDtypeStruct((n_seq, GDN_HEADS, GDN_DK, GDN_DV), F32)],
        compiler_params=_params(("arbitrary", "arbitrary")),
        name="gdn",
    )(proj, proj, conv0, proj, gcol, grow_chunks, state0, conv_w, norm_w)


def _merge_body(x_ref, on_ref, b_ref, c_ref, h_ref, ga_ref, gb_ref, cprev_ref, hprev_ref, sc0_ref,
                g1_ref, sc2_ref, sh2_ref, wa_ref, wb_ref, wo_ref, cw_ref, n2_ref, wr_ref, br_ref, cnt0_ref,
                x1_ref, h2_ref, ri_ref, rw_ref, cnt_ref, tail_ref, cnt_scr, *, per_seq, n_tiles):
    step = pl.program_id(0) * pl.num_programs(1) + pl.program_id(1)

    @pl.when(step == 0)
    def _():
        cnt_scr[...] = cnt0_ref[...]

    tm, d = on_ref.shape
    pre = c_ref[...] * h_ref[...]
    if per_seq:
        hist = sc0_ref[...].reshape(tm, d)
    else:
        hist = jnp.where(pl.program_id(1) == 0, sc0_ref[0], cprev_ref[...] * hprev_ref[...])
    tail_ref[...] = pre[tm - SUBLANES:, :] if not per_seq else pre
    y_b = _bdot(b_ref[...] * _causal_conv(pre, hist, cw_ref[...], per_seq), wb_ref[...])
    y_a = _bdot(on_ref[...], wa_ref[...])
    merged = jax.nn.sigmoid(ga_ref[...]) * y_a + jax.nn.sigmoid(gb_ref[...]) * y_b
    mo = _bdot(merged, wo_ref[...]).reshape(x_ref.shape)
    x1 = x_ref[...] + g1_ref[...] * mo
    y = x1 * lax.rsqrt(jnp.mean(x1 * x1, axis=-1, keepdims=True) + NORM_EPS) * n2_ref[...]
    h2 = (y * (1.0 + sc2_ref[...]) + sh2_ref[...]).reshape(tm, d)
    x1_ref[...] = x1.reshape(tm, d)
    h2_ref[...] = h2

    logits = jnp.dot(h2, wr_ref[...], precision=HIGHEST, preferred_element_type=F32) + br_ref[...]
    lane = lax.broadcasted_iota(I32, logits.shape, 1)
    lane_f = lane.astype(F32)
    work = logits
    vals, sels, hots = [], [], []
    member = jnp.zeros(logits.shape, F32)
    for _ in range(TOP_K):
        m = jnp.max(work, axis=-1, keepdims=True)
        sel = jnp.min(jnp.where(work == m, lane_f, float(N_EXPERTS - 1)), axis=-1, keepdims=True)
        hot = lane_f == sel
        vals.append(m)
        sels.append(sel.astype(I32))
        hots.append(hot)
        member = member + hot.astype(F32)
        work = jnp.where(hot, -jnp.inf, work)
    exps = [jnp.exp(v - vals[0]) for v in vals]
    denom = exps[0] + exps[1] + exps[2] + exps[3]
    ti = lax.broadcasted_iota(I32, (tm, tm), 0)
    tj = lax.broadcasted_iota(I32, (tm, tm), 1)
    before = (tj < ti).astype(BF16)
    rank_all = jnp.dot(before, member.astype(BF16), preferred_element_type=F32) + cnt_scr[0:1, :]
    ri = jnp.zeros((tm, ROUTER_LANES), I32)
    rw = jnp.zeros((tm, ROUTER_LANES), F32)
    for kk in range(TOP_K):
        rank_k = jnp.sum(jnp.where(hots[kk], rank_all, 0.0), axis=-1, keepdims=True).astype(I32)
        ri = jnp.where(lane == kk, sels[kk], ri)
        ri = jnp.where(lane == TOP_K + kk, rank_k, ri)
        rw = jnp.where(lane == kk, exps[kk] / denom, rw)
    ri_ref[...] = ri
    rw_ref[...] = rw
    cnt_scr[...] = cnt_scr[...] + jnp.sum(member, axis=0, keepdims=True)

    @pl.when(step == n_tiles - 1)
    def _():
        cnt_ref[...] = cnt_scr[...]


def _merge(x, on, proj, sc0, g1, sc2, sh2, w_a, w_b, w_o, conv_w, norm2_w, w_r, b_r, cnt0):
    n_seq, seq_len, d = x.shape
    sb, tb = _seq_tile(n_seq, seq_len, 512 if seq_len >= 512 else 256)
    per_seq = sb > 1 or tb == SUBLANES
    if per_seq:
        assert tb == SUBLANES
    tm = sb * tb
    nt = seq_len // tb
    ns = n_seq // sb
    n_tok = n_seq * seq_len
    rpb = tm // SUBLANES
    row = lambda s, t: s * nt + t
    prev = lambda s, t: jnp.maximum(row(s, t) * rpb - 1, 0)
    col = lambda j: (lambda s, t: (row(s, t), j))
    full = lambda shape: pl.BlockSpec(shape, lambda s, t: (0,) * len(shape))
    ada = pl.BlockSpec((sb, 1, d), lambda s, t: (s, 0, 0))
    tok = pl.BlockSpec((tm, d), lambda s, t: (row(s, t), 0))
    lanes = pl.BlockSpec((tm, ROUTER_LANES), lambda s, t: (row(s, t), 0))
    tail_rows = tm if per_seq else SUBLANES
    return pl.pallas_call(
        functools.partial(_merge_body, per_seq=per_seq, n_tiles=ns * nt),
        grid=(ns, nt),
        in_specs=[pl.BlockSpec((sb, tb, d), lambda s, t: (s, t, 0)),
                  tok,
                  pl.BlockSpec((tm, d), col(4)), pl.BlockSpec((tm, d), col(5)), pl.BlockSpec((tm, d), col(6)),
                  pl.BlockSpec((tm, d), col(7)), pl.BlockSpec((tm, d), col(8)),
                  pl.BlockSpec((SUBLANES, d), lambda s, t: (prev(s, t), 5)),
                  pl.BlockSpec((SUBLANES, d), lambda s, t: (prev(s, t), 6)),
                  pl.BlockSpec((sb, SUBLANES, d), lambda s, t: (s, 0, 0)),
                  ada, ada, ada,
                  full((d, d)), full((d, d)), full((d, d)),
                  full((SC_CONV, d)), full((1, d)), full((d, ROUTER_LANES)), full((1, ROUTER_LANES)),
                  full((SUBLANES, ROUTER_LANES))],
        out_specs=[tok, tok, lanes, lanes, full((SUBLANES, ROUTER_LANES)),
                   pl.BlockSpec((tail_rows, d), lambda s, t: (row(s, t), 0))],
        out_shape=[jax.ShapeDtypeStruct((n_tok, d), F32),
                   jax.ShapeDtypeStruct((n_tok, d), F32),
                   jax.ShapeDtypeStruct((n_tok, ROUTER_LANES), I32),
                   jax.ShapeDtypeStruct((n_tok, ROUTER_LANES), F32),
                   jax.ShapeDtypeStruct((SUBLANES, ROUTER_LANES), F32),
                   jax.ShapeDtypeStruct((ns * nt * tail_rows, d), F32)],
        scratch_shapes=[pltpu.VMEM((SUBLANES, ROUTER_LANES), F32)],
        compiler_params=_params(("arbitrary", "arbitrary")),
        name="merge",
    )(x, on, proj, proj, proj, proj, proj, proj, proj, sc0, g1, sc2, sh2, w_a, w_b, w_o, conv_w, norm2_w,
      w_r, b_r, cnt0)


def _dispatch_body(off_ref, eid_ref, rank_ref, h2_ref, buf_in_ref, buf_ref, sem, *, tile):
    del buf_in_ref

    def copies(i):
        src = h2_ref.at[pl.ds(pl.program_id(0) * tile + i, 1)]
        return [pltpu.make_async_copy(src, buf_ref.at[pl.ds(off_ref[eid_ref[k, i]] + rank_ref[k, i], 1)], sem)
                for k in range(TOP_K)]

    def issue(i, carry):
        for cp in copies(i):
            cp.start()
        return carry

    def drain(i, carry):
        for cp in copies(i):
            cp.wait()
        return carry

    lax.fori_loop(0, tile, issue, 0)
    lax.fori_loop(0, tile, drain, 0)


def _dispatch(offsets, eid, rank, h2, buf):
    n_tok, d = h2.shape
    tile = min(512, n_tok)
    assert n_tok % tile == 0 and buf.shape[0] >= TOP_K * tile
    grid_spec = pltpu.PrefetchScalarGridSpec(
        num_scalar_prefetch=1,
        grid=(n_tok // tile,),
        in_specs=[pl.BlockSpec((TOP_K, tile), lambda i, off: (0, i), memory_space=pltpu.SMEM),
                  pl.BlockSpec((TOP_K, tile), lambda i, off: (0, i), memory_space=pltpu.SMEM),
                  pl.BlockSpec(memory_space=pl.ANY),
                  pl.BlockSpec(memory_space=pl.ANY)],
        out_specs=pl.BlockSpec(memory_space=pl.ANY),
        scratch_shapes=[pltpu.SemaphoreType.DMA(())],
    )
    return pl.pallas_call(
        functools.partial(_dispatch_body, tile=tile),
        grid_spec=grid_spec,
        out_shape=jax.ShapeDtypeStruct(buf.shape, buf.dtype),
        input_output_aliases={4: 0},
        compiler_params=pltpu.CompilerParams(dimension_semantics=("arbitrary",), has_side_effects=True),
        name="dispatch",
    )(offsets, eid, rank, h2, buf)


def _expert_body(be_ref, nv_ref, x_ref, wg_ref, bg_ref, wu_ref, bu_ref, wd_ref, bd_ref, o_ref,
                 wg_s, wu_s, wd_s):
    i = pl.program_id(0)

    @pl.when(i < nv_ref[0])
    def _():
        @pl.when((i == 0) | (be_ref[i] != be_ref[jnp.maximum(i - 1, 0)]))
        def _():
            wg_s[...] = wg_ref[0].astype(BF16)
            wu_s[...] = wu_ref[0].astype(BF16)
            wd_s[...] = wd_ref[0].astype(BF16)

        x = x_ref[...].astype(BF16)
        gate = jnp.dot(x, wg_s[...], preferred_element_type=F32) + bg_ref[0]
        up = jnp.dot(x, wu_s[...], preferred_element_type=F32) + bu_ref[0]
        gate = jnp.minimum(gate, SWIGLU_LIMIT)
        up = jnp.clip(up, -SWIGLU_LIMIT, SWIGLU_LIMIT)
        glu = gate * jax.nn.sigmoid(SWIGLU_ALPHA * gate)
        o_ref[...] = _bdot((up + 1.0) * glu, wd_s[...]) + bd_ref[0]

    @pl.when(i >= nv_ref[0])
    def _():
        o_ref[...] = jnp.zeros(o_ref.shape, F32)


def _experts(block_e, n_valid, xs, w_gate, b_gate, w_up, b_up, w_down, b_down):
    m_pad, d = xs.shape
    n_blocks = m_pad // EXPERT_BLOCK
    f = w_gate.shape[2]
    blk = lambda i, be, nv: (jnp.minimum(i, nv[0] - 1), 0)
    wspec = lambda a, b: pl.BlockSpec((1, a, b), lambda i, be, nv: (be[i], 0, 0))
    grid_spec = pltpu.PrefetchScalarGridSpec(
        num_scalar_prefetch=2,
        grid=(n_blocks,),
        in_specs=[pl.BlockSpec((EXPERT_BLOCK, d), blk),
                  wspec(d, f), wspec(1, f), wspec(d, f), wspec(1, f), wspec(f, d), wspec(1, d)],
        out_specs=pl.BlockSpec((EXPERT_BLOCK, d), lambda i, be, nv: (i, 0)),
        scratch_shapes=[pltpu.VMEM((d, f), BF16), pltpu.VMEM((d, f), BF16), pltpu.VMEM((f, d), BF16)],
    )
    return pl.pallas_call(
        _expert_body,
        grid_spec=grid_spec,
        out_shape=jax.ShapeDtypeStruct((m_pad, d), F32),
        compiler_params=_params(("arbitrary",)),
        name="experts",
    )(block_e, n_valid, xs, w_gate, b_gate[:, None, :], w_up, b_up[:, None, :], w_down, b_down[:, None, :])


def _combine_body(off_ref, eid_ref, rank_ref, ys_ref, x1_ref, rw_ref, g2_ref, fw_ref, o_ref, rows, sem, *, tile):
    def copies(i):
        return [pltpu.make_async_copy(ys_ref.at[pl.ds(off_ref[eid_ref[k, i]] + rank_ref[k, i], 1)],
                                      rows.at[k, pl.ds(i, 1)], sem)
                for k in range(TOP_K)]

    def issue(i, carry):
        for cp in copies(i):
            cp.start()
        return carry

    def drain(i, carry):
        for cp in copies(i):
            cp.wait()
        return carry

    lax.fori_loop(0, tile, issue, 0)
    lax.fori_loop(0, tile, drain, 0)
    rw = rw_ref[...]
    acc = rw[:, 0:1] * rows[0]
    for k in range(1, TOP_K):
        acc = acc + rw[:, k:k + 1] * rows[k]
    y = x1_ref[...] + g2_ref[...] * acc.reshape(x1_ref.shape)
    o_ref[...] = y * lax.rsqrt(jnp.mean(y * y, axis=-1, keepdims=True) + NORM_EPS) * fw_ref[...]


def _combine(offsets, eid, rank, ys, x1, rw, g2, final_w, tok0):
    n_seq, seq_len, d = x1.shape
    sb, tb = _seq_tile(n_seq, seq_len, 128)
    tile = sb * tb
    nt = seq_len // tb
    t0 = tok0 // tile
    assert tok0 % tile == 0
    row = lambda s, t: s * nt + t
    grid_spec = pltpu.PrefetchScalarGridSpec(
        num_scalar_prefetch=1,
        grid=(n_seq // sb, nt),
        in_specs=[pl.BlockSpec((TOP_K, tile), lambda s, t, off: (0, t0 + row(s, t)), memory_space=pltpu.SMEM),
                  pl.BlockSpec((TOP_K, tile), lambda s, t, off: (0, t0 + row(s, t)), memory_space=pltpu.SMEM),
                  pl.BlockSpec(memory_space=pl.ANY),
                  pl.BlockSpec((sb, tb, d), lambda s, t, off: (s, t, 0)),
                  pl.BlockSpec((tile, ROUTER_LANES), lambda s, t, off: (row(s, t), 0)),
                  pl.BlockSpec((sb, 1, d), lambda s, t, off: (s, 0, 0)),
                  pl.BlockSpec((1, d), lambda s, t, off: (0, 0))],
        out_specs=pl.BlockSpec((sb, tb, d), lambda s, t, off: (s, t, 0)),
        scratch_shapes=[pltpu.VMEM((TOP_K, tile, d), F32), pltpu.SemaphoreType.DMA(())],
    )
    return pl.pallas_call(
        functools.partial(_combine_body, tile=tile),
        grid_spec=grid_spec,
        out_shape=jax.ShapeDtypeStruct((n_seq, seq_len, d), F32),
        compiler_params=_params(("arbitrary", "arbitrary")),
        name="combine",
    )(offsets, eid, rank, ys, x1, rw, g2, final_w)


def _pad_state(state, rows):
    return jnp.pad(state, ((0, 0), (rows - state.shape[1], 0), (0, 0)))


def kernel(x_prompt, x_sample, c_prompt, c_sample, state_gdn_conv, state_gdn_rec, state_sc_conv, w_ada, b_ada,
           norm1_w, w_in, gdn_conv_w, gdn_a_log, gdn_dt_bias, gdn_norm_w, w_branch_a, sc_conv_w, w_branch_b,
           w_out, norm2_w, w_router, b_router, w_gate, b_gate, w_up, b_up, w_down, b_down, final_norm_w):
    assert w_ada.shape[0] == 1, "single-layer trunk"
    d = D_MODEL
    bp, tp, _ = x_prompt.shape
    bs, ts, _ = x_sample.shape
    n_p, n_s = bp * tp, bs * ts
    n_tok = n_p + n_s

    w_in0 = w_in[0]
    g_lo, g_hi = QKV_W + d, QKV_W + d + N_GATE_COLS
    w_main = jnp.concatenate([w_in0[:, :g_lo], w_in0[:, g_hi:]], axis=1).astype(BF16)
    w_g = w_in0[:, g_lo:g_hi].astype(BF16)
    w_gt = w_g.T
    zeros_h = jnp.zeros((GDN_HEADS,), F32)
    p_row = jnp.stack([jnp.concatenate([zeros_h, gdn_a_log[0]]), jnp.concatenate([zeros_h, gdn_dt_bias[0]])])
    p_col = p_row.T
    w_a = w_branch_a[0].astype(BF16)
    w_b = w_branch_b[0].astype(BF16)
    w_o = w_out[0].astype(BF16)
    w_r = jnp.pad(w_router[0], ((0, 0), (0, ROUTER_LANES - N_EXPERTS)))
    b_r = jnp.pad(b_router[0], (0, ROUTER_LANES - N_EXPERTS), constant_values=-jnp.inf).reshape(1, ROUTER_LANES)

    n_c = bp + bs
    c_rows = -(-n_c // 16) * 16
    c_all = jnp.pad(jnp.concatenate([c_prompt, c_sample], axis=0), ((0, c_rows - n_c), (0, 0)))
    ada = _ada(c_all, w_ada[0], b_ada[0])

    def ada_parts(lo, hi):
        return [ada[lo:hi, j * d:(j + 1) * d].reshape(hi - lo, 1, d) for j in range(6)]

    groups = [
        dict(x=x_prompt, ada=ada_parts(0, bp), n_seq=bp, seq_len=tp,
             conv0=jnp.zeros((bp, SUBLANES, QKV_W), F32),
             rec0=jnp.zeros((bp, GDN_HEADS, GDN_DK, GDN_DV), F32),
             sc0=jnp.zeros((bp, SUBLANES, d), F32)),
        dict(x=x_sample, ada=ada_parts(bp, n_c), n_seq=bs, seq_len=ts,
             conv0=_pad_state(state_gdn_conv[0], SUBLANES),
             rec0=state_gdn_rec[0],
             sc0=_pad_state(state_sc_conv[0], SUBLANES)),
    ]

    counts = jnp.zeros((SUBLANES, ROUTER_LANES), F32)
    for g in groups:
        sh1, sc1, g1, sh2, sc2, g2 = g["ada"]
        n_seq, seq_len = g["n_seq"], g["seq_len"]
        proj, gcol, grow = _proj(g["x"], sc1, sh1, norm1_w, w_main, w_g, w_gt, p_row, p_col)
        on, rec = _gdn(proj, gcol, grow, g["conv0"], g["rec0"], gdn_conv_w[0], gdn_norm_w, n_seq, seq_len)
        x1, h2, ri, rw, counts, tail = _merge(g["x"], on, proj, g["sc0"], g1, sc2, sh2, w_a, w_b, w_o,
                                              sc_conv_w[0], norm2_w, w_r, b_r, counts)
        proj3 = proj.reshape(n_seq, seq_len, PROJ_MAIN_W)
        g.update(x1=x1, h2=h2, ri=ri, rw=rw, rec=rec, g2=g2,
                 new_conv=proj3[:, seq_len - (GDN_CONV - 1):, :QKV_W],
                 new_sc=tail.reshape(n_seq, -1, d)[:, -(SC_CONV - 1):, :])

    cnt = counts[0, :N_EXPERTS].astype(I32)
    padded = (cnt + EXPERT_BLOCK - 1) // EXPERT_BLOCK * EXPERT_BLOCK
    pad_end = jnp.cumsum(padded)
    offsets = (pad_end - padded).astype(I32)
    n_blocks = -(-(n_tok * TOP_K) // EXPERT_BLOCK) + N_EXPERTS
    block_e = jnp.minimum(jnp.searchsorted(pad_end, jnp.arange(n_blocks, dtype=I32) * EXPERT_BLOCK, side="right"),
                          N_EXPERTS - 1).astype(I32)
    n_valid = (pad_end[-1:] // EXPERT_BLOCK).astype(I32)
    route_i = jnp.concatenate([g["ri"][:, :2 * TOP_K] for g in groups], axis=0).T
    eid, rank = route_i[:TOP_K], route_i[TOP_K:]

    buf = jnp.zeros((n_blocks * EXPERT_BLOCK, d), F32)
    tok0 = 0
    for g in groups:
        n = g["h2"].shape[0]
        buf = _dispatch(offsets, eid[:, tok0:tok0 + n], rank[:, tok0:tok0 + n], g["h2"], buf)
        tok0 += n
    ys = _experts(block_e, n_valid, buf, w_gate[0], b_gate[0], w_up[0], b_up[0], w_down[0], b_down[0])
    outs = []
    tok0 = 0
    for g in groups:
        n_seq, seq_len = g["n_seq"], g["seq_len"]
        x1 = g["x1"].reshape(n_seq, seq_len, d)
        outs.append(_combine(offsets, eid, rank, ys, x1, g["rw"], g["g2"], final_norm_w.reshape(1, d), tok0))
        tok0 += n_seq * seq_len

    gp, gs = groups
    return (outs[0], outs[1], gp["new_conv"][None], gp["rec"][None], gp["new_sc"][None],
            gs["new_conv"][None], gs["rec"][None], gs["new_sc"][None])
```

```python
import functools

import jax
import jax.numpy as jnp
from jax import lax
from jax.experimental import pallas as pl
from jax.experimental.pallas import tpu as pltpu

F32 = jnp.float32
BF16 = jnp.bfloat16
I32 = jnp.int32
HIGHEST = lax.Precision.HIGHEST

D_MODEL = 1024
GDN_HEADS = 8
GDN_DK = 128
GDN_DV = 128
GDN_QK = GDN_HEADS * GDN_DK
QKV_W = 3 * GDN_QK
GDN_CONV = 4
GDN_CHUNK = 64
SC_CONV = 3
N_EXPERTS = 32
TOP_K = 4
SWIGLU_LIMIT = 7.0
SWIGLU_ALPHA = 1.702
NORM_EPS = 1e-6
N_GATE_COLS = 2 * GDN_HEADS
PROJ_MAIN_W = 9 * D_MODEL
ROUTER_LANES = 128
EXPERT_BLOCK = 256
SUBLANES = 8
VMEM_LIMIT = 56 * 1024 * 1024

_NT = (((1,), (1,)), ((), ()))
_TN = (((0,), (0,)), ((), ()))


def _bdot(a, b):
    return jnp.dot(a.astype(BF16), b.astype(BF16), preferred_element_type=F32)


def _bdot_nt(a, b):
    return lax.dot_general(a.astype(BF16), b.astype(BF16), _NT, preferred_element_type=F32)


def _bdot_tn(a, b):
    return lax.dot_general(a.astype(BF16), b.astype(BF16), _TN, preferred_element_type=F32)


def _silu(x):
    return x * jax.nn.sigmoid(x)


def _softplus(x):
    return jnp.maximum(x, 0.0) + jnp.log1p(jnp.exp(-jnp.abs(x)))


def _seq_tile(n_seq, seq_len, target):
    if seq_len >= target:
        assert seq_len % target == 0
        return 1, target
    sb = min(n_seq, target // seq_len)
    assert n_seq % sb == 0 and seq_len % SUBLANES == 0
    return sb, seq_len


def _params(sem):
    return pltpu.CompilerParams(dimension_semantics=sem, vmem_limit_bytes=VMEM_LIMIT)


def _ada_body(c_ref, w_ref, b_ref, o_ref):
    o_ref[...] = _bdot(_silu(c_ref[...]), w_ref[...]) + b_ref[...]


def _ada(c, w_ada, b_ada):
    rows, d = c.shape
    n = w_ada.shape[1]
    tn = 1024
    return pl.pallas_call(
        _ada_body,
        grid=(n // tn,),
        in_specs=[pl.BlockSpec((rows, d), lambda j: (0, 0)),
                  pl.BlockSpec((d, tn), lambda j: (0, j)),
                  pl.BlockSpec((1, tn), lambda j: (0, j))],
        out_specs=pl.BlockSpec((rows, tn), lambda j: (0, j)),
        out_shape=jax.ShapeDtypeStruct((rows, n), F32),
        compiler_params=_params(("arbitrary",)),
        name="ada",
    )(c, w_ada, b_ada.reshape(1, n))


def _gates(v, a_log, dt_bias, axis):
    is_beta = lax.broadcasted_iota(I32, v.shape, axis) < GDN_HEADS
    beta = jax.nn.sigmoid(v)
    g = -jnp.exp(a_log) * _softplus(v + dt_bias)
    return jnp.where(is_beta, beta, g)


def _proj_body(x_ref, sc_ref, sh_ref, nw_ref, w_ref, wg_ref, wgt_ref, prow_ref, pcol_ref,
               o_ref, gc_ref, gr_ref, h_scr):
    @pl.when(pl.program_id(2) == 0)
    def _():
        x = x_ref[...]
        y = x * lax.rsqrt(jnp.mean(x * x, axis=-1, keepdims=True) + NORM_EPS) * nw_ref[...]
        h = (y * (1.0 + sc_ref[...]) + sh_ref[...]).reshape(h_scr.shape).astype(BF16)
        h_scr[...] = h
        gc = jnp.dot(h, wg_ref[...], preferred_element_type=F32)
        gr = lax.dot_general(wgt_ref[...], h, _NT, preferred_element_type=F32)
        gc_ref[...] = _gates(gc, prow_ref[0:1, :], prow_ref[1:2, :], 1)
        gr_ref[...] = _gates(gr, pcol_ref[:, 0:1], pcol_ref[:, 1:2], 0)

    o_ref[...] = jnp.dot(h_scr[...], w_ref[...], preferred_element_type=F32)


def _proj(x, sc, sh, norm_w, w_main, w_g, w_gt, p_row, p_col):
    n_seq, seq_len, d = x.shape
    sb, tb = _seq_tile(n_seq, seq_len, 1024)
    tm = sb * tb
    tn = 1024
    nt = seq_len // tb
    n_tok = n_seq * seq_len
    row = lambda s, t, j: s * nt + t
    return pl.pallas_call(
        _proj_body,
        grid=(n_seq // sb, nt, PROJ_MAIN_W // tn),
        in_specs=[pl.BlockSpec((sb, tb, d), lambda s, t, j: (s, t, 0)),
                  pl.BlockSpec((sb, 1, d), lambda s, t, j: (s, 0, 0)),
                  pl.BlockSpec((sb, 1, d), lambda s, t, j: (s, 0, 0)),
                  pl.BlockSpec((1, d), lambda s, t, j: (0, 0)),
                  pl.BlockSpec((d, tn), lambda s, t, j: (0, j)),
                  pl.BlockSpec((d, N_GATE_COLS), lambda s, t, j: (0, 0)),
                  pl.BlockSpec((N_GATE_COLS, d), lambda s, t, j: (0, 0)),
                  pl.BlockSpec((2, N_GATE_COLS), lambda s, t, j: (0, 0)),
                  pl.BlockSpec((N_GATE_COLS, 2), lambda s, t, j: (0, 0))],
        out_specs=[pl.BlockSpec((tm, tn), lambda s, t, j: (row(s, t, j), j)),
                   pl.BlockSpec((tm, N_GATE_COLS), lambda s, t, j: (row(s, t, j), 0)),
                   pl.BlockSpec((N_GATE_COLS, tm), lambda s, t, j: (0, row(s, t, j)))],
        out_shape=[jax.ShapeDtypeStruct((n_tok, PROJ_MAIN_W), F32),
                   jax.ShapeDtypeStruct((n_tok, N_GATE_COLS), F32),
                   jax.ShapeDtypeStruct((N_GATE_COLS, n_tok), F32)],
        scratch_shapes=[pltpu.VMEM((tm, d), BF16)],
        compiler_params=_params(("arbitrary", "arbitrary", "arbitrary")),
        name="proj",
    )(x, sc, sh, norm_w, w_main, w_g, w_gt, p_row, p_col)


def _shift_rows(x, hist, s, per_seq):
    rows, width = x.shape
    if per_seq:
        src = pltpu.roll(hist, (rows - SUBLANES + s) % rows, 0) if rows > SUBLANES else pltpu.roll(hist, s, 0)
        xr = pltpu.roll(x, s, 0)
        r = lax.broadcasted_iota(I32, (rows, width), 0) & (SUBLANES - 1)
        return jnp.where(r < s, src, xr)
    xr = pltpu.roll(x, s, 0)
    hr = pltpu.roll(hist, s, 0)
    r = lax.broadcasted_iota(I32, (SUBLANES, width), 0)
    head = jnp.where(r < s, hr, xr[:SUBLANES])
    if rows == SUBLANES:
        return head
    return jnp.concatenate([head, xr[SUBLANES:]], axis=0)


def _causal_conv(x, hist, w, per_seq):
    taps = w.shape[0]
    acc = x * w[taps - 1:taps, :]
    for s in range(1, taps):
        acc = acc + _shift_rows(x, hist, s, per_seq) * w[taps - 1 - s:taps - s, :]
    return acc


def _gdn_body(qkv_ref, prev_ref, conv0_ref, z_ref, gc_ref, gr_ref, s0_ref, cw_ref, nw_ref,
              o_ref, s_ref, *, chunk):
    t = pl.program_id(1)

    @pl.when(t == 0)
    def _():
        s_ref[...] = s0_ref[...]

    x = qkv_ref[...]
    hist = jnp.where(t == 0, conv0_ref[0], prev_ref[...])
    qkvc = _silu(_causal_conv(x, hist, cw_ref[...], False))

    gcol = gc_ref[...]
    grow = gr_ref[0]
    ri = lax.broadcasted_iota(I32, (chunk, chunk), 0)
    ci = lax.broadcasted_iota(I32, (chunk, chunk), 1)
    causal = ri >= ci
    strict = ri > ci
    tri = causal.astype(F32)
    eye = (ri == ci).astype(F32)
    dec_col = jnp.dot(tri, gcol[:, GDN_HEADS:], precision=HIGHEST, preferred_element_type=F32)
    dec_row = lax.dot_general(grow[GDN_HEADS:, :], tri, _NT, precision=HIGHEST, preferred_element_type=F32)
    n_lvl = chunk.bit_length() - 1
    blk = [lax.shift_right_logical(ri, l) == lax.shift_right_logical(ci, l) for l in range(1, n_lvl + 1)]
    pair = [blk[l] & jnp.logical_not(blk[l - 1]) for l in range(1, n_lvl)]

    heads = range(GDN_HEADS)
    q, k, v, beta, dcol, dlast, decay, edec = [], [], [], [], [], [], [], []
    for h in heads:
        qh = qkvc[:, h * GDN_DK:(h + 1) * GDN_DK]
        kh = qkvc[:, GDN_QK + h * GDN_DK:GDN_QK + (h + 1) * GDN_DK]
        q.append(qh * lax.rsqrt(jnp.sum(qh * qh, axis=-1, keepdims=True) + 1e-6) * (GDN_DK ** -0.5))
        k.append(kh * lax.rsqrt(jnp.sum(kh * kh, axis=-1, keepdims=True) + 1e-6))
        v.append(qkvc[:, 2 * GDN_QK + h * GDN_DV:2 * GDN_QK + (h + 1) * GDN_DV])
        beta.append(gcol[:, h:h + 1])
        dcol.append(dec_col[:, h:h + 1])
        dlast.append(dec_col[chunk - 1:chunk, h:h + 1])
        decay.append(jnp.where(causal, jnp.exp(dcol[h] - dec_row[h:h + 1, :]), 0.0))
        edec.append(jnp.exp(dcol[h]))
    kk = [_bdot_nt(k[h], k[h]) for h in heads]
    qk = [_bdot_nt(q[h], k[h]) * decay[h] for h in heads]
    a = [jnp.where(strict, beta[h] * kk[h] * decay[h], 0.0) for h in heads]
    inv = [eye - jnp.where(blk[0], a[h], 0.0) for h in heads]
    for lower_left in pair:
        right = [_bdot(jnp.where(lower_left, a[h], 0.0), inv[h]) for h in heads]
        inv = [inv[h] - _bdot(inv[h], right[h]) for h in heads]
    sol = [_bdot(inv[h], jnp.concatenate([beta[h] * v[h], (beta[h] * edec[h]) * k[h]], axis=1)) for h in heads]
    state = [s_ref[0, h] for h in heads]
    ws = [_bdot(jnp.concatenate([sol[h][:, GDN_DV:], q[h] * edec[h]], axis=0), state[h]) for h in heads]
    u = [sol[h][:, :GDN_DV] - ws[h][:chunk] for h in heads]
    o = [ws[h][chunk:] + _bdot(qk[h], u[h]) for h in heads]
    upd = [_bdot_tn(k[h] * jnp.exp(dlast[h] - dcol[h]), u[h]) for h in heads]
    for h in heads:
        s_ref[0, h] = state[h] * jnp.exp(dlast[h]) + upd[h]
        on = o[h] * lax.rsqrt(jnp.mean(o[h] * o[h], axis=-1, keepdims=True) + NORM_EPS) * nw_ref[...]
        o_ref[:, h * GDN_DV:(h + 1) * GDN_DV] = on * _silu(z_ref[:, h * GDN_DV:(h + 1) * GDN_DV])


def _gdn(proj, gcol, grow, conv0, state0, conv_w, norm_w, n_seq, seq_len):
    chunk = min(GDN_CHUNK, seq_len)
    assert seq_len % chunk == 0 and chunk % SUBLANES == 0
    nt = seq_len // chunk
    n_tok = n_seq * seq_len
    cpb = chunk // SUBLANES
    row = lambda s, t: s * nt + t
    grow_chunks = grow.reshape(N_GATE_COLS, n_tok // chunk, chunk).transpose(1, 0, 2)
    return pl.pallas_call(
        functools.partial(_gdn_body, chunk=chunk),
        grid=(n_seq, nt),
        in_specs=[pl.BlockSpec((chunk, QKV_W), lambda s, t: (row(s, t), 0)),
                  pl.BlockSpec((SUBLANES, QKV_W), lambda s, t: (jnp.maximum(row(s, t) * cpb - 1, 0), 0)),
                  pl.BlockSpec((1, SUBLANES, QKV_W), lambda s, t: (s, 0, 0)),
                  pl.BlockSpec((chunk, D_MODEL), lambda s, t: (row(s, t), 3)),
                  pl.BlockSpec((chunk, N_GATE_COLS), lambda s, t: (row(s, t), 0)),
                  pl.BlockSpec((1, N_GATE_COLS, chunk), lambda s, t: (row(s, t), 0, 0)),
                  pl.BlockSpec((1, GDN_HEADS, GDN_DK, GDN_DV), lambda s, t: (s, 0, 0, 0)),
                  pl.BlockSpec((GDN_CONV, QKV_W), lambda s, t: (0, 0)),
                  pl.BlockSpec((1, GDN_DV), lambda s, t: (0, 0))],
        out_specs=[pl.BlockSpec((chunk, D_MODEL), lambda s, t: (row(s, t), 0)),
                   pl.BlockSpec((1, GDN_HEADS, GDN_DK, GDN_DV), lambda s, t: (s, 0, 0, 0))],
        out_shape=[jax.ShapeDtypeStruct((n_tok, D_MODEL), F32),
                   jax.ShapeDtypeStruct((n_seq, GDN_HEADS, GDN_DK, GDN_DV), F32)],
        compiler_params=_params(("arbitrary", "arbitrary")),
        name="gdn",
    )(proj, proj, conv0, proj, gcol, grow_chunks, state0, conv_w, norm_w)


def _merge_body(x_ref, on_ref, b_ref, c_ref, h_ref, ga_ref, gb_ref, cprev_ref, hprev_ref, sc0_ref,
                g1_ref, sc2_ref, sh2_ref, wa_ref, wb_ref, wo_ref, cw_ref, n2_ref, wr_ref, br_ref, cnt0_ref,
                x1_ref, h2_ref, ri_ref, rw_ref, cnt_ref, tail_ref, cnt_scr, *, per_seq, n_tiles):
    step = pl.program_id(0) * pl.num_programs(1) + pl.program_id(1)

    @pl.when(step == 0)
    def _():
        cnt_scr[...] = cnt0_ref[...]

    tm, d = on_ref.shape
    pre = c_ref[...] * h_ref[...]
    if per_seq:
        hist = sc0_ref[...].reshape(tm, d)
    else:
        hist = jnp.where(pl.program_id(1) == 0, sc0_ref[0], cprev_ref[...] * hprev_ref[...])
    tail_ref[...] = pre[tm - SUBLANES:, :] if not per_seq else pre
    y_b = _bdot(b_ref[...] * _causal_conv(pre, hist, cw_ref[...], per_seq), wb_ref[...])
    y_a = _bdot(on_ref[...], wa_ref[...])
    merged = jax.nn.sigmoid(ga_ref[...]) * y_a + jax.nn.sigmoid(gb_ref[...]) * y_b
    mo = _bdot(merged, wo_ref[...]).reshape(x_ref.shape)
    x1 = x_ref[...] + g1_ref[...] * mo
    y = x1 * lax.rsqrt(jnp.mean(x1 * x1, axis=-1, keepdims=True) + NORM_EPS) * n2_ref[...]
    h2 = (y * (1.0 + sc2_ref[...]) + sh2_ref[...]).reshape(tm, d)
    x1_ref[...] = x1.reshape(tm, d)
    h2_ref[...] = h2

    logits = jnp.dot(h2, wr_ref[...], precision=HIGHEST, preferred_element_type=F32) + br_ref[...]
    lane = lax.broadcasted_iota(I32, logits.shape, 1)
    lane_f = lane.astype(F32)
    work = logits
    vals, sels, hots = [], [], []
    member = jnp.zeros(logits.shape, F32)
    for _ in range(TOP_K):
        m = jnp.max(work, axis=-1, keepdims=True)
        sel = jnp.min(jnp.where(work == m, lane_f, float(N_EXPERTS - 1)), axis=-1, keepdims=True)
        hot = lane_f == sel
        vals.append(m)
        sels.append(sel.astype(I32))
        hots.append(hot)
        member = member + hot.astype(F32)
        work = jnp.where(hot, -jnp.inf, work)
    exps = [jnp.exp(v - vals[0]) for v in vals]
    denom = exps[0] + exps[1] + exps[2] + exps[3]
    ti = lax.broadcasted_iota(I32, (tm, tm), 0)
    tj = lax.broadcasted_iota(I32, (tm, tm), 1)
    before = (tj < ti).astype(BF16)
    rank_all = jnp.dot(before, member.astype(BF16), preferred_element_type=F32) + cnt_scr[0:1, :]
    ri = jnp.zeros((tm, ROUTER_LANES), I32)
    rw = jnp.zeros((tm, ROUTER_LANES), F32)
    for kk in range(TOP_K):
        rank_k = jnp.sum(jnp.where(hots[kk], rank_all, 0.0), axis=-1, keepdims=True).astype(I32)
        ri = jnp.where(lane == kk, sels[kk], ri)
        ri = jnp.where(lane == TOP_K + kk, rank_k, ri)
        rw = jnp.where(lane == kk, exps[kk] / denom, rw)
    ri_ref[...] = ri
    rw_ref[...] = rw
    cnt_scr[...] = cnt_scr[...] + jnp.sum(member, axis=0, keepdims=True)

    @pl.when(step == n_tiles - 1)
    def _():
        cnt_ref[...] = cnt_scr[...]


def _merge(x, on, proj, sc0, g1, sc2, sh2, w_a, w_b, w_o, conv_w, norm2_w, w_r, b_r, cnt0):
    n_seq, seq_len, d = x.shape
    sb, tb = _seq_tile(n_seq, seq_len, 512 if seq_len >= 512 else 256)
    per_seq = sb > 1 or tb == SUBLANES
    if per_seq:
        assert tb == SUBLANES
    tm = sb * tb
    nt = seq_len // tb
    ns = n_seq // sb
    n_tok = n_seq * seq_len
    rpb = tm // SUBLANES
    row = lambda s, t: s * nt + t
    prev = lambda s, t: jnp.maximum(row(s, t) * rpb - 1, 0)
    col = lambda j: (lambda s, t: (row(s, t), j))
    full = lambda shape: pl.BlockSpec(shape, lambda s, t: (0,) * len(shape))
    ada = pl.BlockSpec((sb, 1, d), lambda s, t: (s, 0, 0))
    tok = pl.BlockSpec((tm, d), lambda s, t: (row(s, t), 0))
    lanes = pl.BlockSpec((tm, ROUTER_LANES), lambda s, t: (row(s, t), 0))
    tail_rows = tm if per_seq else SUBLANES
    return pl.pallas_call(
        functools.partial(_merge_body, per_seq=per_seq, n_tiles=ns * nt),
        grid=(ns, nt),
        in_specs=[pl.BlockSpec((sb, tb, d), lambda s, t: (s, t, 0)),
                  tok,
                  pl.BlockSpec((tm, d), col(4)), pl.BlockSpec((tm, d), col(5)), pl.BlockSpec((tm, d), col(6)),
                  pl.BlockSpec((tm, d), col(7)), pl.BlockSpec((tm, d), col(8)),
                  pl.BlockSpec((SUBLANES, d), lambda s, t: (prev(s, t), 5)),
                  pl.BlockSpec((SUBLANES, d), lambda s, t: (prev(s, t), 6)),
                  pl.BlockSpec((sb, SUBLANES, d), lambda s, t: (s, 0, 0)),
                  ada, ada, ada,
                  full((d, d)), full((d, d)), full((d, d)),
                  full((SC_CONV, d)), full((1, d)), full((d, ROUTER_LANES)), full((1, ROUTER_LANES)),
                  full((SUBLANES, ROUTER_LANES))],
        out_specs=[tok, tok, lanes, lanes, full((SUBLANES, ROUTER_LANES)),
                   pl.BlockSpec((tail_rows, d), lambda s, t: (row(s, t), 0))],
        out_shape=[jax.ShapeDtypeStruct((n_tok, d), F32),
                   jax.ShapeDtypeStruct((n_tok, d), F32),
                   jax.ShapeDtypeStruct((n_tok, ROUTER_LANES), I32),
                   jax.ShapeDtypeStruct((n_tok, ROUTER_LANES), F32),
                   jax.ShapeDtypeStruct((SUBLANES, ROUTER_LANES), F32),
                   jax.ShapeDtypeStruct((ns * nt * tail_rows, d), F32)],
        scratch_shapes=[pltpu.VMEM((SUBLANES, ROUTER_LANES), F32)],
        compiler_params=_params(("arbitrary", "arbitrary")),
        name="merge",
    )(x, on, proj, proj, proj, proj, proj, proj, proj, sc0, g1, sc2, sh2, w_a, w_b, w_o, conv_w, norm2_w,
      w_r, b_r, cnt0)


def _dispatch_body(off_ref, eid_ref, rank_ref, h2_ref, buf_in_ref, buf_ref, sem, *, tile):
    del buf_in_ref

    def copies(i):
        src = h2_ref.at[pl.ds(i, 1)]
        return [pltpu.make_async_copy(src, buf_ref.at[pl.ds(off_ref[eid_ref[k, i]] + rank_ref[k, i], 1)], sem)
                for k in range(TOP_K)]

    def issue(i, carry):
        for cp in copies(i):
            cp.start()
        return carry

    def drain(i, carry):
        for cp in copies(i):
            cp.wait()
        return carry

    lax.fori_loop(0, tile, issue, 0)
    lax.fori_loop(0, tile, drain, 0)


def _dispatch(offsets, eid, rank, h2, buf):
    n_tok, d = h2.shape
    tile = min(512, n_tok)
    assert n_tok % tile == 0 and buf.shape[0] >= TOP_K * tile
    grid_spec = pltpu.PrefetchScalarGridSpec(
        num_scalar_prefetch=1,
        grid=(n_tok // tile,),
        in_specs=[pl.BlockSpec((TOP_K, tile), lambda i, off: (0, i), memory_space=pltpu.SMEM),
                  pl.BlockSpec((TOP_K, tile), lambda i, off: (0, i), memory_space=pltpu.SMEM),
                  pl.BlockSpec((tile, d), lambda i, off: (i, 0)),
                  pl.BlockSpec(memory_space=pl.ANY)],
        out_specs=pl.BlockSpec(memory_space=pl.ANY),
        scratch_shapes=[pltpu.SemaphoreType.DMA(())],
    )
    return pl.pallas_call(
        functools.partial(_dispatch_body, tile=tile),
        grid_spec=grid_spec,
        out_shape=jax.ShapeDtypeStruct(buf.shape, buf.dtype),
        input_output_aliases={4: 0},
        compiler_params=pltpu.CompilerParams(dimension_semantics=("arbitrary",), has_side_effects=True),
        name="dispatch",
    )(offsets, eid, rank, h2, buf)


def _expert_body(be_ref, nv_ref, x_ref, wg_ref, bg_ref, wu_ref, bu_ref, wd_ref, bd_ref, o_ref,
                 wg_s, wu_s, wd_s):
    i = pl.program_id(0)

    @pl.when(i < nv_ref[0])
    def _():
        @pl.when((i == 0) | (be_ref[i] != be_ref[jnp.maximum(i - 1, 0)]))
        def _():
            wg_s[...] = wg_ref[0].astype(BF16)
            wu_s[...] = wu_ref[0].astype(BF16)
            wd_s[...] = wd_ref[0].astype(BF16)

        x = x_ref[...].astype(BF16)
        gate = jnp.dot(x, wg_s[...], preferred_element_type=F32) + bg_ref[0]
        up = jnp.dot(x, wu_s[...], preferred_element_type=F32) + bu_ref[0]
        gate = jnp.minimum(gate, SWIGLU_LIMIT)
        up = jnp.clip(up, -SWIGLU_LIMIT, SWIGLU_LIMIT)
        glu = gate * jax.nn.sigmoid(SWIGLU_ALPHA * gate)
        o_ref[...] = _bdot((up + 1.0) * glu, wd_s[...]) + bd_ref[0]

    @pl.when(i >= nv_ref[0])
    def _():
        o_ref[...] = jnp.zeros(o_ref.shape, F32)


def _experts(block_e, n_valid, xs, w_gate, b_gate, w_up, b_up, w_down, b_down):
    m_pad, d = xs.shape
    n_blocks = m_pad // EXPERT_BLOCK
    f = w_gate.shape[2]
    blk = lambda i, be, nv: (jnp.minimum(i, nv[0] - 1), 0)
    wspec = lambda a, b: pl.BlockSpec((1, a, b), lambda i, be, nv: (be[i], 0, 0))
    grid_spec = pltpu.PrefetchScalarGridSpec(
        num_scalar_prefetch=2,
        grid=(n_blocks,),
        in_specs=[pl.BlockSpec((EXPERT_BLOCK, d), blk),
                  wspec(d, f), wspec(1, f), wspec(d, f), wspec(1, f), wspec(f, d), wspec(1, d)],
        out_specs=pl.BlockSpec((EXPERT_BLOCK, d), lambda i, be, nv: (i, 0)),
        scratch_shapes=[pltpu.VMEM((d, f), BF16), pltpu.VMEM((d, f), BF16), pltpu.VMEM((f, d), BF16)],
    )
    return pl.pallas_call(
        _expert_body,
        grid_spec=grid_spec,
        out_shape=jax.ShapeDtypeStruct((m_pad, d), F32),
        compiler_params=_params(("arbitrary",)),
        name="experts",
    )(block_e, n_valid, xs, w_gate, b_gate[:, None, :], w_up, b_up[:, None, :], w_down, b_down[:, None, :])


def _combine_body(off_ref, eid_ref, rank_ref, ys_ref, x1_ref, rw_ref, g2_ref, fw_ref, o_ref, rows, sem, *, tile):
    def copies(i):
        return [pltpu.make_async_copy(ys_ref.at[pl.ds(off_ref[eid_ref[k, i]] + rank_ref[k, i], 1)],
                                      rows.at[k, pl.ds(i, 1)], sem)
                for k in range(TOP_K)]

    def issue(i, carry):
        for cp in copies(i):
            cp.start()
        return carry

    def drain(i, carry):
        for cp in copies(i):
            cp.wait()
        return carry

    lax.fori_loop(0, tile, issue, 0)
    lax.fori_loop(0, tile, drain, 0)
    rw = rw_ref[...]
    acc = rw[:, 0:1] * rows[0]
    for k in range(1, TOP_K):
        acc = acc + rw[:, k:k + 1] * rows[k]
    y = x1_ref[...] + g2_ref[...] * acc.reshape(x1_ref.shape)
    o_ref[...] = y * lax.rsqrt(jnp.mean(y * y, axis=-1, keepdims=True) + NORM_EPS) * fw_ref[...]


def _combine(offsets, eid, rank, ys, x1, rw, g2, final_w, tok0):
    n_seq, seq_len, d = x1.shape
    sb, tb = _seq_tile(n_seq, seq_len, 128)
    tile = sb * tb
    nt = seq_len // tb
    t0 = tok0 // tile
    assert tok0 % tile == 0
    row = lambda s, t: s * nt + t
    grid_spec = pltpu.PrefetchScalarGridSpec(
        num_scalar_prefetch=1,
        grid=(n_seq // sb, nt),
        in_specs=[pl.BlockSpec((TOP_K, tile), lambda s, t, off: (0, t0 + row(s, t)), memory_space=pltpu.SMEM),
                  pl.BlockSpec((TOP_K, tile), lambda s, t, off: (0, t0 + row(s, t)), memory_space=pltpu.SMEM),
                  pl.BlockSpec(memory_space=pl.ANY),
                  pl.BlockSpec((sb, tb, d), lambda s, t, off: (s, t, 0)),
                  pl.BlockSpec((tile, ROUTER_LANES), lambda s, t, off: (row(s, t), 0)),
                  pl.BlockSpec((sb, 1, d), lambda s, t, off: (s, 0, 0)),
                  pl.BlockSpec((1, d), lambda s, t, off: (0, 0))],
        out_specs=pl.BlockSpec((sb, tb, d), lambda s, t, off: (s, t, 0)),
        scratch_shapes=[pltpu.VMEM((TOP_K, tile, d), F32), pltpu.SemaphoreType.DMA(())],
    )
    return pl.pallas_call(
        functools.partial(_combine_body, tile=tile),
        grid_spec=grid_spec,
        out_shape=jax.ShapeDtypeStruct((n_seq, seq_len, d), F32),
        compiler_params=_params(("arbitrary", "arbitrary")),
        name="combine",
    )(offsets, eid, rank, ys, x1, rw, g2, final_w)


def _pad_state(state, rows):
    return jnp.pad(state, ((0, 0), (rows - state.shape[1], 0), (0, 0)))


def kernel(x_prompt, x_sample, c_prompt, c_sample, state_gdn_conv, state_gdn_rec, state_sc_conv, w_ada, b_ada,
           norm1_w, w_in, gdn_conv_w, gdn_a_log, gdn_dt_bias, gdn_norm_w, w_branch_a, sc_conv_w, w_branch_b,
           w_out, norm2_w, w_router, b_router, w_gate, b_gate, w_up, b_up, w_down, b_down, final_norm_w):
    assert w_ada.shape[0] == 1, "single-layer trunk"
    d = D_MODEL
    bp, tp, _ = x_prompt.shape
    bs, ts, _ = x_sample.shape
    n_p, n_s = bp * tp, bs * ts
    n_tok = n_p + n_s

    w_in0 = w_in[0]
    g_lo, g_hi = QKV_W + d, QKV_W + d + N_GATE_COLS
    w_main = jnp.concatenate([w_in0[:, :g_lo], w_in0[:, g_hi:]], axis=1).astype(BF16)
    w_g = w_in0[:, g_lo:g_hi].astype(BF16)
    w_gt = w_g.T
    zeros_h = jnp.zeros((GDN_HEADS,), F32)
    p_row = jnp.stack([jnp.concatenate([zeros_h, gdn_a_log[0]]), jnp.concatenate([zeros_h, gdn_dt_bias[0]])])
    p_col = p_row.T
    w_a = w_branch_a[0].astype(BF16)
    w_b = w_branch_b[0].astype(BF16)
    w_o = w_out[0].astype(BF16)
    w_r = jnp.pad(w_router[0], ((0, 0), (0, ROUTER_LANES - N_EXPERTS)))
    b_r = jnp.pad(b_router[0], (0, ROUTER_LANES - N_EXPERTS), constant_values=-jnp.inf).reshape(1, ROUTER_LANES)

    n_c = bp + bs
    c_rows = -(-n_c // 16) * 16
    c_all = jnp.pad(jnp.concatenate([c_prompt, c_sample], axis=0), ((0, c_rows - n_c), (0, 0)))
    ada = _ada(c_all, w_ada[0], b_ada[0])

    def ada_parts(lo, hi):
        return [ada[lo:hi, j * d:(j + 1) * d].reshape(hi - lo, 1, d) for j in range(6)]

    groups = [
        dict(x=x_prompt, ada=ada_parts(0, bp), n_seq=bp, seq_len=tp,
             conv0=jnp.zeros((bp, SUBLANES, QKV_W), F32),
             rec0=jnp.zeros((bp, GDN_HEADS, GDN_DK, GDN_DV), F32),
             sc0=jnp.zeros((bp, SUBLANES, d), F32)),
        dict(x=x_sample, ada=ada_parts(bp, n_c), n_seq=bs, seq_len=ts,
             conv0=_pad_state(state_gdn_conv[0], SUBLANES),
             rec0=state_gdn_rec[0],
             sc0=_pad_state(state_sc_conv[0], SUBLANES)),
    ]

    counts = jnp.zeros((SUBLANES, ROUTER_LANES), F32)
    for g in groups:
        sh1, sc1, g1, sh2, sc2, g2 = g["ada"]
        n_seq, seq_len = g["n_seq"], g["seq_len"]
        proj, gcol, grow = _proj(g["x"], sc1, sh1, norm1_w, w_main, w_g, w_gt, p_row, p_col)
        on, rec = _gdn(proj, gcol, grow, g["conv0"], g["rec0"], gdn_conv_w[0], gdn_norm_w, n_seq, seq_len)
        x1, h2, ri, rw, counts, tail = _merge(g["x"], on, proj, g["sc0"], g1, sc2, sh2, w_a, w_b, w_o,
                                              sc_conv_w[0], norm2_w, w_r, b_r, counts)
        proj3 = proj.reshape(n_seq, seq_len, PROJ_MAIN_W)
        g.update(x1=x1, h2=h2, ri=ri, rw=rw, rec=rec, g2=g2,
                 new_conv=proj3[:, seq_len - (GDN_CONV - 1):, :QKV_W],
                 new_sc=tail.reshape(n_seq, -1, d)[:, -(SC_CONV - 1):, :])

    cnt = counts[0, :N_EXPERTS].astype(I32)
    padded = (cnt + EXPERT_BLOCK - 1) // EXPERT_BLOCK * EXPERT_BLOCK
    pad_end = jnp.cumsum(padded)
    offsets = (pad_end - padded).astype(I32)
    n_blocks = -(-(n_tok * TOP_K) // EXPERT_BLOCK) + N_EXPERTS
    block_start = jnp.arange(n_blocks, dtype=I32) * EXPERT_BLOCK
    block_e = jnp.minimum(jnp.sum((pad_end[None, :] <= block_start[:, None]).astype(I32), axis=1), N_EXPERTS - 1)
    n_valid = (pad_end[-1:] // EXPERT_BLOCK).astype(I32)
    route_i = jnp.concatenate([g["ri"][:, :2 * TOP_K] for g in groups], axis=0).T
    eid, rank = route_i[:TOP_K], route_i[TOP_K:]

    buf = jnp.zeros((n_blocks * EXPERT_BLOCK, d), F32)
    tok0 = 0
    for g in groups:
        n = g["h2"].shape[0]
        buf = _dispatch(offsets, eid[:, tok0:tok0 + n], rank[:, tok0:tok0 + n], g["h2"], buf)
        tok0 += n
    ys = _experts(block_e, n_valid, buf, w_gate[0], b_gate[0], w_up[0], b_up[0], w_down[0], b_down[0])
    outs = []
    tok0 = 0
    for g in groups:
        n_seq, seq_len = g["n_seq"], g["seq_len"]
        x1 = g["x1"].reshape(n_seq, seq_len, d)
        outs.append(_combine(offsets, eid, rank, ys, x1, g["rw"], g["g2"], final_norm_w.reshape(1, d), tok0))
        tok0 += n_seq * seq_len

    gp, gs = groups
    return (outs[0], outs[1], gp["new_conv"][None], gp["rec"][None], gp["new_sc"][None],
            gs["new_conv"][None], gs["rec"][None], gs["new_sc"][None])
```

```python
import functools

import jax
import jax.numpy as jnp
from jax import lax
from jax.experimental import pallas as pl
from jax.experimental.pallas import tpu as pltpu

F32 = jnp.float32
BF16 = jnp.bfloat16
I32 = jnp.int32
HIGHEST = lax.Precision.HIGHEST

D_MODEL = 1024
GDN_HEADS = 8
GDN_DK = 128
GDN_DV = 128
GDN_QK = GDN_HEADS * GDN_DK
QKV_W = 3 * GDN_QK
GDN_CONV = 4
GDN_CHUNK = 64
GDN_STEP_CHUNKS = 4
SC_CONV = 3
N_EXPERTS = 32
TOP_K = 4
SWIGLU_LIMIT = 7.0
SWIGLU_ALPHA = 1.702
NORM_EPS = 1e-6
N_GATE_COLS = 2 * GDN_HEADS
PROJ_MAIN_W = 9 * D_MODEL
ROUTER_LANES = 128
EXPERT_BLOCK = 512
SUBLANES = 8
VMEM_LIMIT = 56 * 1024 * 1024

_NT = (((1,), (1,)), ((), ()))
_TN = (((0,), (0,)), ((), ()))


def _bdot(a, b):
    return jnp.dot(a.astype(BF16), b.astype(BF16), preferred_element_type=F32)


def _bdot_nt(a, b):
    return lax.dot_general(a.astype(BF16), b.astype(BF16), _NT, preferred_element_type=F32)


def _bdot_tn(a, b):
    return lax.dot_general(a.astype(BF16), b.astype(BF16), _TN, preferred_element_type=F32)


def _silu(x):
    return x * jax.nn.sigmoid(x)


def _softplus(x):
    return jnp.maximum(x, 0.0) + jnp.log1p(jnp.exp(-jnp.abs(x)))


def _seq_tile(n_seq, seq_len, target):
    if seq_len >= target:
        assert seq_len % target == 0
        return 1, target
    sb = min(n_seq, target // seq_len)
    assert n_seq % sb == 0 and seq_len % SUBLANES == 0
    return sb, seq_len


def _route_tile(n_seq, seq_len):
    return _seq_tile(n_seq, seq_len, 512 if seq_len >= 512 else 256)


def _params(sem):
    return pltpu.CompilerParams(dimension_semantics=sem, vmem_limit_bytes=VMEM_LIMIT)


def _ada_body(c_ref, w_ref, b_ref, o_ref):
    o_ref[...] = _bdot(_silu(c_ref[...]), w_ref[...]) + b_ref[...]


def _ada(c, w_ada, b_ada):
    rows, d = c.shape
    n = w_ada.shape[1]
    tn = 1024
    return pl.pallas_call(
        _ada_body,
        grid=(n // tn,),
        in_specs=[pl.BlockSpec((rows, d), lambda j: (0, 0)),
                  pl.BlockSpec((d, tn), lambda j: (0, j)),
                  pl.BlockSpec((1, tn), lambda j: (0, j))],
        out_specs=pl.BlockSpec((rows, tn), lambda j: (0, j)),
        out_shape=jax.ShapeDtypeStruct((rows, n), F32),
        compiler_params=_params(("arbitrary",)),
        name="ada",
    )(c, w_ada, b_ada.reshape(1, n))


def _gates(v, a_log, dt_bias, axis):
    is_beta = lax.broadcasted_iota(I32, v.shape, axis) < GDN_HEADS
    beta = jax.nn.sigmoid(v)
    g = -jnp.exp(a_log) * _softplus(v + dt_bias)
    return jnp.where(is_beta, beta, g)


def _proj_body(x_ref, sc_ref, sh_ref, nw_ref, w_ref, wg_ref, wgt_ref, prow_ref, pcol_ref,
               o_ref, gc_ref, gr_ref, h_scr):
    @pl.when(pl.program_id(2) == 0)
    def _():
        x = x_ref[...]
        y = x * lax.rsqrt(jnp.mean(x * x, axis=-1, keepdims=True) + NORM_EPS) * nw_ref[...]
        h = (y * (1.0 + sc_ref[...]) + sh_ref[...]).reshape(h_scr.shape).astype(BF16)
        h_scr[...] = h
        gc = jnp.dot(h, wg_ref[...], preferred_element_type=F32)
        gr = lax.dot_general(wgt_ref[...], h, _NT, preferred_element_type=F32)
        gc_ref[...] = _gates(gc, prow_ref[0:1, :], prow_ref[1:2, :], 1)
        gr_ref[...] = _gates(gr, pcol_ref[:, 0:1], pcol_ref[:, 1:2], 0)

    o_ref[...] = jnp.dot(h_scr[...], w_ref[...], preferred_element_type=F32)


def _proj(x, sc, sh, norm_w, w_main, w_g, w_gt, p_row, p_col):
    n_seq, seq_len, d = x.shape
    sb, tb = _seq_tile(n_seq, seq_len, 1024)
    tm = sb * tb
    tn = 1024
    nt = seq_len // tb
    n_tok = n_seq * seq_len
    row = lambda s, t, j: s * nt + t
    return pl.pallas_call(
        _proj_body,
        grid=(n_seq // sb, nt, PROJ_MAIN_W // tn),
        in_specs=[pl.BlockSpec((sb, tb, d), lambda s, t, j: (s, t, 0)),
                  pl.BlockSpec((sb, 1, d), lambda s, t, j: (s, 0, 0)),
                  pl.BlockSpec((sb, 1, d), lambda s, t, j: (s, 0, 0)),
                  pl.BlockSpec((1, d), lambda s, t, j: (0, 0)),
                  pl.BlockSpec((d, tn), lambda s, t, j: (0, j)),
                  pl.BlockSpec((d, N_GATE_COLS), lambda s, t, j: (0, 0)),
                  pl.BlockSpec((N_GATE_COLS, d), lambda s, t, j: (0, 0)),
                  pl.BlockSpec((2, N_GATE_COLS), lambda s, t, j: (0, 0)),
                  pl.BlockSpec((N_GATE_COLS, 2), lambda s, t, j: (0, 0))],
        out_specs=[pl.BlockSpec((tm, tn), lambda s, t, j: (row(s, t, j), j)),
                   pl.BlockSpec((tm, N_GATE_COLS), lambda s, t, j: (row(s, t, j), 0)),
                   pl.BlockSpec((N_GATE_COLS, tm), lambda s, t, j: (0, row(s, t, j)))],
        out_shape=[jax.ShapeDtypeStruct((n_tok, PROJ_MAIN_W), F32),
                   jax.ShapeDtypeStruct((n_tok, N_GATE_COLS), F32),
                   jax.ShapeDtypeStruct((N_GATE_COLS, n_tok), F32)],
        scratch_shapes=[pltpu.VMEM((tm, d), BF16)],
        compiler_params=_params(("arbitrary", "arbitrary", "arbitrary")),
        name="proj",
    )(x, sc, sh, norm_w, w_main, w_g, w_gt, p_row, p_col)


def _shift_rows(x, hist, s, per_seq):
    rows, width = x.shape
    if per_seq:
        src = pltpu.roll(hist, (rows - SUBLANES + s) % rows, 0) if rows > SUBLANES else pltpu.roll(hist, s, 0)
        xr = pltpu.roll(x, s, 0)
        r = lax.broadcasted_iota(I32, (rows, width), 0) & (SUBLANES - 1)
        return jnp.where(r < s, src, xr)
    xr = pltpu.roll(x, s, 0)
    hr = pltpu.roll(hist, s, 0)
    r = lax.broadcasted_iota(I32, (SUBLANES, width), 0)
    head = jnp.where(r < s, hr, xr[:SUBLANES])
    if rows == SUBLANES:
        return head
    return jnp.concatenate([head, xr[SUBLANES:]], axis=0)


def _causal_conv(x, hist, w, per_seq):
    taps = w.shape[0]
    acc = x * w[taps - 1:taps, :]
    for s in range(1, taps):
        acc = acc + _shift_rows(x, hist, s, per_seq) * w[taps - 1 - s:taps - s, :]
    return acc


def _gdn_body(qkv_ref, prev_ref, conv0_ref, z_ref, gc_ref, gr_ref, s0_ref, cw_ref, nw_ref,
              o_ref, s_ref, *, chunk):
    t = pl.program_id(1)

    @pl.when(t == 0)
    def _():
        s_ref[...] = s0_ref[...]

    x = qkv_ref[...]
    hist = jnp.where(t == 0, conv0_ref[0], prev_ref[...])
    qkvc = _silu(_causal_conv(x, hist, cw_ref[...], False))

    gcol = gc_ref[...]
    n_chunks = gr_ref.shape[0]
    ri = lax.broadcasted_iota(I32, (chunk, chunk), 0)
    ci = lax.broadcasted_iota(I32, (chunk, chunk), 1)
    causal = ri >= ci
    strict = ri > ci
    tri = causal.astype(F32)
    eye = (ri == ci).astype(F32)
    n_lvl = chunk.bit_length() - 1
    blk = [lax.shift_right_logical(ri, l) == lax.shift_right_logical(ci, l) for l in range(1, n_lvl + 1)]
    pair = [blk[l] & jnp.logical_not(blk[l - 1]) for l in range(1, n_lvl)]

    units = [(c, h) for c in range(n_chunks) for h in range(GDN_HEADS)]
    every = range(len(units))
    q, k, v, beta, dcol, dlast, decay, edec = [], [], [], [], [], [], [], []
    for c in range(n_chunks):
        rows = slice(c * chunk, (c + 1) * chunk)
        dec_col = jnp.dot(tri, gcol[rows, GDN_HEADS:], precision=HIGHEST, preferred_element_type=F32)
        dec_row = lax.dot_general(gr_ref[c][GDN_HEADS:, :], tri, _NT, precision=HIGHEST,
                                  preferred_element_type=F32)
        for h in range(GDN_HEADS):
            qh = qkvc[rows, h * GDN_DK:(h + 1) * GDN_DK]
            kh = qkvc[rows, GDN_QK + h * GDN_DK:GDN_QK + (h + 1) * GDN_DK]
            q.append(qh * lax.rsqrt(jnp.sum(qh * qh, axis=-1, keepdims=True) + 1e-6) * (GDN_DK ** -0.5))
            k.append(kh * lax.rsqrt(jnp.sum(kh * kh, axis=-1, keepdims=True) + 1e-6))
            v.append(qkvc[rows, 2 * GDN_QK + h * GDN_DV:2 * GDN_QK + (h + 1) * GDN_DV])
            beta.append(gcol[rows, h:h + 1])
            dcol.append(dec_col[:, h:h + 1])
            dlast.append(dec_col[chunk - 1:chunk, h:h + 1])
            decay.append(jnp.where(causal, jnp.exp(dcol[-1] - dec_row[h:h + 1, :]), 0.0))
            edec.append(jnp.exp(dcol[-1]))
    kk = [_bdot_nt(k[i], k[i]) for i in every]
    qk = [_bdot_nt(q[i], k[i]) * decay[i] for i in every]
    a = [jnp.where(strict, beta[i] * kk[i] * decay[i], 0.0) for i in every]
    inv = [eye - jnp.where(blk[0], a[i], 0.0) for i in every]
    for lower_left in pair:
        right = [_bdot(jnp.where(lower_left, a[i], 0.0), inv[i]) for i in every]
        inv = [inv[i] - _bdot(inv[i], right[i]) for i in every]
    sol = [_bdot(inv[i], jnp.concatenate([beta[i] * v[i], (beta[i] * edec[i]) * k[i]], axis=1)) for i in every]
    state = [s_ref[0, h] for h in range(GDN_HEADS)]
    for c in range(n_chunks):
        rows = slice(c * chunk, (c + 1) * chunk)
        idx = [c * GDN_HEADS + h for h in range(GDN_HEADS)]
        ws = [_bdot(jnp.concatenate([sol[i][:, GDN_DV:], q[i] * edec[i]], axis=0), state[h])
              for h, i in enumerate(idx)]
        u = [sol[i][:, :GDN_DV] - ws[h][:chunk] for h, i in enumerate(idx)]
        o = [ws[h][chunk:] + _bdot(qk[i], u[h]) for h, i in enumerate(idx)]
        upd = [_bdot_tn(k[i] * jnp.exp(dlast[i] - dcol[i]), u[h]) for h, i in enumerate(idx)]
        state = [state[h] * jnp.exp(dlast[i]) + upd[h] for h, i in enumerate(idx)]
        for h in range(GDN_HEADS):
            on = o[h] * lax.rsqrt(jnp.mean(o[h] * o[h], axis=-1, keepdims=True) + NORM_EPS) * nw_ref[...]
            o_ref[rows, h * GDN_DV:(h + 1) * GDN_DV] = on * _silu(z_ref[rows, h * GDN_DV:(h + 1) * GDN_DV])
    for h in range(GDN_HEADS):
        s_ref[0, h] = state[h]


def _gdn(proj, gcol, grow, conv0, state0, conv_w, norm_w, n_seq, seq_len):
    chunk = min(GDN_CHUNK, seq_len)
    assert seq_len % chunk == 0 and chunk % SUBLANES == 0
    n_step_chunks = GDN_STEP_CHUNKS if (seq_len // chunk) % GDN_STEP_CHUNKS == 0 else 1
    step_rows = n_step_chunks * chunk
    nt = seq_len // step_rows
    n_tok = n_seq * seq_len
    cpb = step_rows // SUBLANES
    row = lambda s, t: s * nt + t
    grow_chunks = grow.reshape(N_GATE_COLS, n_tok // chunk, chunk).transpose(1, 0, 2)
    return pl.pallas_call(
        functools.partial(_gdn_body, chunk=chunk),
        grid=(n_seq, nt),
        in_specs=[pl.BlockSpec((step_rows, QKV_W), lambda s, t: (row(s, t), 0)),
                  pl.BlockSpec((SUBLANES, QKV_W), lambda s, t: (jnp.maximum(row(s, t) * cpb - 1, 0), 0)),
                  pl.BlockSpec((1, SUBLANES, QKV_W), lambda s, t: (s, 0, 0)),
                  pl.BlockSpec((step_rows, D_MODEL), lambda s, t: (row(s, t), 3)),
                  pl.BlockSpec((step_rows, N_GATE_COLS), lambda s, t: (row(s, t), 0)),
                  pl.BlockSpec((n_step_chunks, N_GATE_COLS, chunk), lambda s, t: (row(s, t), 0, 0)),
                  pl.BlockSpec((1, GDN_HEADS, GDN_DK, GDN_DV), lambda s, t: (s, 0, 0, 0)),
                  pl.BlockSpec((GDN_CONV, QKV_W), lambda s, t: (0, 0)),
                  pl.BlockSpec((1, GDN_DV), lambda s, t: (0, 0))],
        out_specs=[pl.BlockSpec((step_rows, D_MODEL), lambda s, t: (row(s, t), 0)),
                   pl.BlockSpec((1, GDN_HEADS, GDN_DK, GDN_DV), lambda s, t: (s, 0, 0, 0))],
        out_shape=[jax.ShapeDtypeStruct((n_tok, D_MODEL), F32),
                   jax.ShapeDtypeStruct((n_seq, GDN_HEADS, GDN_DK, GDN_DV), F32)],
        compiler_params=_params(("arbitrary", "arbitrary")),
        name="gdn",
    )(proj, proj, conv0, proj, gcol, grow_chunks, state0, conv_w, norm_w)


def _merge_body(x_ref, on_ref, b_ref, c_ref, h_ref, ga_ref, gb_ref, cprev_ref, hprev_ref, sc0_ref,
                g1_ref, sc2_ref, sh2_ref, wa_ref, wb_ref, wo_ref, cw_ref, n2_ref, wr_ref, br_ref, cnt0_ref,
                x1_ref, h2_ref, pos_ref, rw_ref, cnt_ref, tail_ref, post_ref, tbl_ref, cnt_scr,
                *, per_seq, n_tiles):
    step = pl.program_id(0) * pl.num_programs(1) + pl.program_id(1)

    @pl.when(step == 0)
    def _():
        cnt_scr[...] = cnt0_ref[...]

    tm, d = on_ref.shape
    pre = c_ref[...] * h_ref[...]
    if per_seq:
        hist = sc0_ref[...].reshape(tm, d)
    else:
        hist = jnp.where(pl.program_id(1) == 0, sc0_ref[0], cprev_ref[...] * hprev_ref[...])
    tail_ref[...] = pre[tm - SUBLANES:, :] if not per_seq else pre
    y_b = _bdot(b_ref[...] * _causal_conv(pre, hist, cw_ref[...], per_seq), wb_ref[...])
    y_a = _bdot(on_ref[...], wa_ref[...])
    merged = jax.nn.sigmoid(ga_ref[...]) * y_a + jax.nn.sigmoid(gb_ref[...]) * y_b
    mo = _bdot(merged, wo_ref[...]).reshape(x_ref.shape)
    x1 = x_ref[...] + g1_ref[...] * mo
    y = x1 * lax.rsqrt(jnp.mean(x1 * x1, axis=-1, keepdims=True) + NORM_EPS) * n2_ref[...]
    h2 = (y * (1.0 + sc2_ref[...]) + sh2_ref[...]).reshape(tm, d)
    x1_ref[...] = x1.reshape(tm, d)
    h2_ref[...] = h2

    logits = jnp.dot(h2, wr_ref[...], precision=HIGHEST, preferred_element_type=F32) + br_ref[...]
    lane = lax.broadcasted_iota(I32, logits.shape, 1)
    lane_f = lane.astype(F32)
    work = logits
    vals, hots = [], []
    member = jnp.zeros(logits.shape, F32)
    for _ in range(TOP_K):
        m = jnp.max(work, axis=-1, keepdims=True)
        sel = jnp.min(jnp.where(work == m, lane_f, float(N_EXPERTS - 1)), axis=-1, keepdims=True)
        hot = lane_f == sel
        vals.append(m)
        hots.append(hot)
        member = member + hot.astype(F32)
        work = jnp.where(hot, -jnp.inf, work)
    exps = [jnp.exp(v - vals[0]) for v in vals]
    denom = exps[0] + exps[1] + exps[2] + exps[3]
    ti = lax.broadcasted_iota(I32, (tm, tm), 0)
    tj = lax.broadcasted_iota(I32, (tm, tm), 1)
    before = (tj < ti).astype(BF16)
    rank_loc = jnp.dot(before, member.astype(BF16), preferred_element_type=F32)
    cnt = jnp.sum(member, axis=0, keepdims=True).astype(I32)
    seg = lax.shift_left(lax.shift_right_logical(cnt + (SUBLANES - 1), 3), 3)
    seg8 = jnp.broadcast_to(seg.astype(F32), (SUBLANES, ROUTER_LANES))
    ei = lax.broadcasted_iota(I32, (ROUTER_LANES, ROUTER_LANES), 0)
    ej = lax.broadcasted_iota(I32, (ROUTER_LANES, ROUTER_LANES), 1)
    base8 = jnp.dot(seg8, (ei < ej).astype(F32), precision=HIGHEST, preferred_element_type=F32)
    row_all = base8[0:1, :] + rank_loc
    pos = jnp.zeros((tm, ROUTER_LANES), F32)
    rw = jnp.zeros((tm, ROUTER_LANES), F32)
    for kk in range(TOP_K):
        pos_k = jnp.sum(jnp.where(hots[kk], row_all, 0.0), axis=-1, keepdims=True)
        pos = jnp.where(lane == kk, pos_k, pos)
        rw = jnp.where(lane == kk, exps[kk] / denom, rw)
    pos_ref[...] = pos
    rw_ref[...] = rw
    pick = (lax.broadcasted_iota(I32, (SUBLANES, ROUTER_LANES), 0)
            == lax.broadcasted_iota(I32, (SUBLANES, ROUTER_LANES), 1)).astype(F32)
    post_ref[...] = lax.dot_general(pick, pos, _NT, precision=HIGHEST, preferred_element_type=F32)
    sub = lax.broadcasted_iota(I32, (SUBLANES, ROUTER_LANES), 0)
    tbl_ref[...] = jnp.where(sub == 0, base8.astype(I32),
                             jnp.where(sub == 1, seg8.astype(I32), cnt_scr[...].astype(I32)))
    cnt_scr[...] = cnt_scr[...] + seg8

    @pl.when(step == n_tiles - 1)
    def _():
        cnt_ref[...] = cnt_scr[...]


def _merge(x, on, proj, sc0, g1, sc2, sh2, w_a, w_b, w_o, conv_w, norm2_w, w_r, b_r, cnt0):
    n_seq, seq_len, d = x.shape
    sb, tb = _route_tile(n_seq, seq_len)
    per_seq = sb > 1 or tb == SUBLANES
    if per_seq:
        assert tb == SUBLANES
    tm = sb * tb
    nt = seq_len // tb
    ns = n_seq // sb
    n_tok = n_seq * seq_len
    rpb = tm // SUBLANES
    row = lambda s, t: s * nt + t
    prev = lambda s, t: jnp.maximum(row(s, t) * rpb - 1, 0)
    col = lambda j: (lambda s, t: (row(s, t), j))
    full = lambda shape: pl.BlockSpec(shape, lambda s, t: (0,) * len(shape))
    ada = pl.BlockSpec((sb, 1, d), lambda s, t: (s, 0, 0))
    tok = pl.BlockSpec((tm, d), lambda s, t: (row(s, t), 0))
    lanes = pl.BlockSpec((tm, ROUTER_LANES), lambda s, t: (row(s, t), 0))
    tail_rows = tm if per_seq else SUBLANES
    return pl.pallas_call(
        functools.partial(_merge_body, per_seq=per_seq, n_tiles=ns * nt),
        grid=(ns, nt),
        in_specs=[pl.BlockSpec((sb, tb, d), lambda s, t: (s, t, 0)),
                  tok,
                  pl.BlockSpec((tm, d), col(4)), pl.BlockSpec((tm, d), col(5)), pl.BlockSpec((tm, d), col(6)),
                  pl.BlockSpec((tm, d), col(7)), pl.BlockSpec((tm, d), col(8)),
                  pl.BlockSpec((SUBLANES, d), lambda s, t: (prev(s, t), 5)),
                  pl.BlockSpec((SUBLANES, d), lambda s, t: (prev(s, t), 6)),
                  pl.BlockSpec((sb, SUBLANES, d), lambda s, t: (s, 0, 0)),
                  ada, ada, ada,
                  full((d, d)), full((d, d)), full((d, d)),
                  full((SC_CONV, d)), full((1, d)), full((d, ROUTER_LANES)), full((1, ROUTER_LANES)),
                  full((SUBLANES, ROUTER_LANES))],
        out_specs=[tok, tok, lanes, lanes, full((SUBLANES, ROUTER_LANES)),
                   pl.BlockSpec((tail_rows, d), lambda s, t: (row(s, t), 0)),
                   pl.BlockSpec((SUBLANES, tm), lambda s, t: (0, row(s, t))),
                   pl.BlockSpec((SUBLANES, ROUTER_LANES), lambda s, t: (row(s, t), 0))],
        out_shape=[jax.ShapeDtypeStruct((n_tok, d), F32),
                   jax.ShapeDtypeStruct((n_tok, d), F32),
                   jax.ShapeDtypeStruct((n_tok, ROUTER_LANES), F32),
                   jax.ShapeDtypeStruct((n_tok, ROUTER_LANES), F32),
                   jax.ShapeDtypeStruct((SUBLANES, ROUTER_LANES), F32),
                   jax.ShapeDtypeStruct((ns * nt * tail_rows, d), F32),
                   jax.ShapeDtypeStruct((SUBLANES, n_tok), F32),
                   jax.ShapeDtypeStruct((ns * nt * SUBLANES, ROUTER_LANES), I32)],
        scratch_shapes=[pltpu.VMEM((SUBLANES, ROUTER_LANES), F32)],
        compiler_params=_params(("arbitrary", "arbitrary")),
        name="merge",
    )(x, on, proj, proj, proj, proj, proj, proj, proj, sc0, g1, sc2, sh2, w_a, w_b, w_o, conv_w, norm2_w,
      w_r, b_r, cnt0)


def _pow2_pieces(length, max_rows):
    out = []
    rows = max_rows
    while rows >= SUBLANES:
        shift = rows.bit_length()
        offset = lax.shift_left(lax.shift_right_logical(length, shift), shift)
        out.append(((length & rows) != 0, offset, rows))
        rows //= 2
    return out


def _segment_copies(tbl_ref, tile_id, local_ref, global_ref, sem, to_global, max_rows):
    out = []
    base = tile_id * (3 * N_EXPERTS)
    for e in range(N_EXPERTS):
        local0 = tbl_ref[base + e]
        length = tbl_ref[base + N_EXPERTS + e]
        global0 = tbl_ref[base + 2 * N_EXPERTS + e]
        for pred, offset, rows in _pow2_pieces(length, max_rows):
            loc = local_ref.at[pl.ds(pl.multiple_of(local0 + offset, SUBLANES), rows)]
            glo = global_ref.at[pl.ds(pl.multiple_of(global0 + offset, SUBLANES), rows)]
            cp = pltpu.make_async_copy(loc, glo, sem) if to_global else pltpu.make_async_copy(glo, loc, sem)
            out.append((pred, cp))
    return out


def _start_all(copies):
    for pred, cp in copies:
        pl.when(pred)(cp.start)


def _wait_all(copies):
    for pred, cp in copies:
        pl.when(pred)(cp.wait)


def _dispatch_body(tbl_ref, tails_ref, post_ref, h2_ref, *refs, tile0, zero_tails):
    buf_ref, sorted_scr, zero_scr, sems = refs[-4:]
    i = pl.program_id(0)
    slot = lax.rem(i, 2)
    tm = h2_ref.shape[0]
    r_rows = sorted_scr.shape[1]

    if zero_tails:
        @pl.when(i == 0)
        def _():
            zero_scr[...] = jnp.zeros(zero_scr.shape, F32)
            z_rows = zero_scr.shape[0]

            def fill(b, carry):
                cp = pltpu.make_async_copy(zero_scr, buf_ref.at[pl.ds(pl.multiple_of(b * SUBLANES, SUBLANES), z_rows)],
                                           sems.at[2])
                cp.start()
                cp.wait()
                return carry

            tails = []
            for e in range(N_EXPERTS):
                start, length = tails_ref[e], tails_ref[N_EXPERTS + e]
                whole = length // z_rows
                lax.fori_loop(0, whole, lambda b, c, s=start: fill(s // SUBLANES + b * (z_rows // SUBLANES), c), 0)
                rest0 = start + whole * z_rows
                for pred, offset, rows in _pow2_pieces(length - whole * z_rows, z_rows // 2):
                    at = pl.multiple_of(rest0 + offset, SUBLANES)
                    tails.append((pred, pltpu.make_async_copy(zero_scr.at[pl.ds(0, rows)],
                                                              buf_ref.at[pl.ds(at, rows)], sems.at[2])))
            _start_all(tails)
            _wait_all(tails)
            lax.fori_loop(tails_ref[2 * N_EXPERTS] // z_rows, buf_ref.shape[0] // z_rows,
                          lambda b, c: fill(b * (z_rows // SUBLANES), c), 0)

    pos = post_ref[...].astype(I32)
    j = lax.broadcasted_iota(I32, (r_rows, tm), 0)
    onehot = jnp.zeros((r_rows, tm), F32)
    for k in range(TOP_K):
        onehot = jnp.where(j == pos[k:k + 1, :], 1.0, onehot)
    sorted_scr[slot] = _bdot(onehot, h2_ref[...])

    _start_all(_segment_copies(tbl_ref, tile0 + i, sorted_scr.at[slot], buf_ref, sems.at[slot], True, tm))

    @pl.when(i > 0)
    def _():
        _wait_all(_segment_copies(tbl_ref, tile0 + i - 1, sorted_scr.at[1 - slot], buf_ref, sems.at[1 - slot],
                                  True, tm))

    @pl.when(i == pl.num_programs(0) - 1)
    def _():
        _wait_all(_segment_copies(tbl_ref, tile0 + i, sorted_scr.at[slot], buf_ref, sems.at[slot], True, tm))


def _dispatch(tbl, tails, post, h2, buf, buf_rows, tm, tile0):
    n_tok, d = h2.shape
    r_rows = TOP_K * tm + N_EXPERTS * SUBLANES
    first = buf is None
    grid_spec = pltpu.PrefetchScalarGridSpec(
        num_scalar_prefetch=2,
        grid=(n_tok // tm,),
        in_specs=[pl.BlockSpec((SUBLANES, tm), lambda i, *_: (0, i)),
                  pl.BlockSpec((tm, d), lambda i, *_: (i, 0))]
                 + ([] if first else [pl.BlockSpec(memory_space=pl.ANY)]),
        out_specs=pl.BlockSpec(memory_space=pl.ANY),
        scratch_shapes=[pltpu.VMEM((2, r_rows, d), F32),
                        pltpu.VMEM((EXPERT_BLOCK // 2, d), F32),
                        pltpu.SemaphoreType.DMA((3,))],
    )
    return pl.pallas_call(
        functools.partial(_dispatch_body, tile0=tile0, zero_tails=first),
        grid_spec=grid_spec,
        out_shape=jax.ShapeDtypeStruct((buf_rows, d), F32),
        input_output_aliases={} if first else {4: 0},
        compiler_params=pltpu.CompilerParams(dimension_semantics=("arbitrary",), has_side_effects=True,
                                             vmem_limit_bytes=VMEM_LIMIT),
        name="dispatch",
    )(tbl, tails, post, h2, *([] if first else [buf]))


def _expert_body(be_ref, nv_ref, x_ref, wg_ref, bg_ref, wu_ref, bu_ref, wd_ref, bd_ref, o_ref,
                 wg_s, wu_s, wd_s):
    i = pl.program_id(0)

    @pl.when(i < nv_ref[0])
    def _():
        @pl.when((i == 0) | (be_ref[i] != be_ref[jnp.maximum(i - 1, 0)]))
        def _():
            wg_s[...] = wg_ref[0].astype(BF16)
            wu_s[...] = wu_ref[0].astype(BF16)
            wd_s[...] = wd_ref[0].astype(BF16)

        x = x_ref[...].astype(BF16)
        gate = jnp.dot(x, wg_s[...], preferred_element_type=F32) + bg_ref[0]
        up = jnp.dot(x, wu_s[...], preferred_element_type=F32) + bu_ref[0]
        gate = jnp.minimum(gate, SWIGLU_LIMIT)
        up = jnp.clip(up, -SWIGLU_LIMIT, SWIGLU_LIMIT)
        glu = gate * jax.nn.sigmoid(SWIGLU_ALPHA * gate)
        o_ref[...] = _bdot((up + 1.0) * glu, wd_s[...]) + bd_ref[0]

    @pl.when(i >= nv_ref[0])
    def _():
        o_ref[...] = jnp.zeros(o_ref.shape, F32)


def _experts(block_e, n_valid, xs, w_gate, b_gate, w_up, b_up, w_down, b_down):
    m_pad, d = xs.shape
    n_blocks = m_pad // EXPERT_BLOCK
    f = w_gate.shape[2]
    blk = lambda i, be, nv: (jnp.maximum(jnp.minimum(i, nv[0] - 1), 0), 0)
    wspec = lambda a, b: pl.BlockSpec((1, a, b), lambda i, be, nv: (be[i], 0, 0))
    grid_spec = pltpu.PrefetchScalarGridSpec(
        num_scalar_prefetch=2,
        grid=(n_blocks,),
        in_specs=[pl.BlockSpec((EXPERT_BLOCK, d), blk),
                  wspec(d, f), wspec(1, f), wspec(d, f), wspec(1, f), wspec(f, d), wspec(1, d)],
        out_specs=pl.BlockSpec((EXPERT_BLOCK, d), lambda i, be, nv: (i, 0)),
        scratch_shapes=[pltpu.VMEM((d, f), BF16), pltpu.VMEM((d, f), BF16), pltpu.VMEM((f, d), BF16)],
    )
    return pl.pallas_call(
        _expert_body,
        grid_spec=grid_spec,
        out_shape=jax.ShapeDtypeStruct((m_pad, d), F32),
        compiler_params=_params(("arbitrary",)),
        name="experts",
    )(block_e, n_valid, xs, w_gate, b_gate[:, None, :], w_up, b_up[:, None, :], w_down, b_down[:, None, :])


def _combine_body(tbl_ref, ys_ref, pos_ref, rw_ref, x1_ref, g2_ref, fw_ref, o_ref, blk_scr, sems, *, tile0):
    step = pl.program_id(0) * pl.num_programs(1) + pl.program_id(1)
    n_steps = pl.num_programs(0) * pl.num_programs(1)
    slot = lax.rem(step, 2)
    tm = pos_ref.shape[0]
    r_rows = blk_scr.shape[1]

    def fetch(tile, into):
        return _segment_copies(tbl_ref, tile0 + tile, blk_scr.at[into], ys_ref, sems.at[into], False, tm)

    @pl.when(step == 0)
    def _():
        blk_scr[...] = jnp.zeros(blk_scr.shape, F32)
        _start_all(fetch(0, 0))

    @pl.when(step + 1 < n_steps)
    def _():
        _start_all(fetch(step + 1, 1 - slot))

    _wait_all(fetch(step, slot))

    pos = pos_ref[...].astype(I32)
    rw = rw_ref[...]
    j = lax.broadcasted_iota(I32, (tm, r_rows), 1)
    pw = jnp.zeros((tm, r_rows), F32)
    for k in range(TOP_K):
        pw = jnp.where(j == pos[:, k:k + 1], rw[:, k:k + 1], pw)
    ffn = _bdot(pw, blk_scr[slot])
    y = x1_ref[...] + g2_ref[...] * ffn.reshape(x1_ref.shape)
    o_ref[...] = y * lax.rsqrt(jnp.mean(y * y, axis=-1, keepdims=True) + NORM_EPS) * fw_ref[...]


def _combine(tbl, ys, pos, rw, x1, g2, final_w, sb, tb, tile0):
    n_seq, seq_len, d = x1.shape
    tm = sb * tb
    nt = seq_len // tb
    r_rows = TOP_K * tm + N_EXPERTS * SUBLANES
    row = lambda s, t: s * nt + t
    grid_spec = pltpu.PrefetchScalarGridSpec(
        num_scalar_prefetch=1,
        grid=(n_seq // sb, nt),
        in_specs=[pl.BlockSpec(memory_space=pl.ANY),
                  pl.BlockSpec((tm, ROUTER_LANES), lambda s, t, *_: (row(s, t), 0)),
                  pl.BlockSpec((tm, ROUTER_LANES), lambda s, t, *_: (row(s, t), 0)),
                  pl.BlockSpec((sb, tb, d), lambda s, t, *_: (s, t, 0)),
                  pl.BlockSpec((sb, 1, d), lambda s, t, *_: (s, 0, 0)),
                  pl.BlockSpec((1, d), lambda s, t, *_: (0, 0))],
        out_specs=pl.BlockSpec((sb, tb, d), lambda s, t, *_: (s, t, 0)),
        scratch_shapes=[pltpu.VMEM((2, r_rows, d), F32), pltpu.SemaphoreType.DMA((2,))],
    )
    return pl.pallas_call(
        functools.partial(_combine_body, tile0=tile0),
        grid_spec=grid_spec,
        out_shape=jax.ShapeDtypeStruct((n_seq, seq_len, d), F32),
        compiler_params=_params(("arbitrary", "arbitrary")),
        name="combine",
    )(tbl, ys, pos, rw, x1, g2, final_w)


def _pad_state(state, rows):
    return jnp.pad(state, ((0, 0), (rows - state.shape[1], 0), (0, 0)))


def kernel(x_prompt, x_sample, c_prompt, c_sample, state_gdn_conv, state_gdn_rec, state_sc_conv, w_ada, b_ada,
           norm1_w, w_in, gdn_conv_w, gdn_a_log, gdn_dt_bias, gdn_norm_w, w_branch_a, sc_conv_w, w_branch_b,
           w_out, norm2_w, w_router, b_router, w_gate, b_gate, w_up, b_up, w_down, b_down, final_norm_w):
    assert w_ada.shape[0] == 1, "single-layer trunk"
    d = D_MODEL
    bp, tp, _ = x_prompt.shape
    bs, ts, _ = x_sample.shape
    n_p, n_s = bp * tp, bs * ts
    n_tok = n_p + n_s

    w_in0 = w_in[0]
    g_lo, g_hi = QKV_W + d, QKV_W + d + N_GATE_COLS
    w_main = jnp.concatenate([w_in0[:, :g_lo], w_in0[:, g_hi:]], axis=1).astype(BF16)
    w_g = w_in0[:, g_lo:g_hi].astype(BF16)
    w_gt = w_g.T
    zeros_h = jnp.zeros((GDN_HEADS,), F32)
    p_row = jnp.stack([jnp.concatenate([zeros_h, gdn_a_log[0]]), jnp.concatenate([zeros_h, gdn_dt_bias[0]])])
    p_col = p_row.T
    w_a = w_branch_a[0].astype(BF16)
    w_b = w_branch_b[0].astype(BF16)
    w_o = w_out[0].astype(BF16)
    w_r = jnp.pad(w_router[0], ((0, 0), (0, ROUTER_LANES - N_EXPERTS)))
    b_r = jnp.pad(b_router[0], (0, ROUTER_LANES - N_EXPERTS), constant_values=-jnp.inf).reshape(1, ROUTER_LANES)

    n_c = bp + bs
    c_rows = -(-n_c // 16) * 16
    c_all = jnp.pad(jnp.concatenate([c_prompt, c_sample], axis=0), ((0, c_rows - n_c), (0, 0)))
    ada = _ada(c_all, w_ada[0], b_ada[0])

    def ada_parts(lo, hi):
        return [ada[lo:hi, j * d:(j + 1) * d].reshape(hi - lo, 1, d) for j in range(6)]

    groups = [
        dict(x=x_prompt, ada=ada_parts(0, bp), n_seq=bp, seq_len=tp,
             conv0=jnp.zeros((bp, SUBLANES, QKV_W), F32),
             rec0=jnp.zeros((bp, GDN_HEADS, GDN_DK, GDN_DV), F32),
             sc0=jnp.zeros((bp, SUBLANES, d), F32)),
        dict(x=x_sample, ada=ada_parts(bp, n_c), n_seq=bs, seq_len=ts,
             conv0=_pad_state(state_gdn_conv[0], SUBLANES),
             rec0=state_gdn_rec[0],
             sc0=_pad_state(state_sc_conv[0], SUBLANES)),
    ]

    counts = jnp.zeros((SUBLANES, ROUTER_LANES), F32)
    for g in groups:
        sh1, sc1, g1, sh2, sc2, g2 = g["ada"]
        n_seq, seq_len = g["n_seq"], g["seq_len"]
        proj, gcol, grow = _proj(g["x"], sc1, sh1, norm1_w, w_main, w_g, w_gt, p_row, p_col)
        on, rec = _gdn(proj, gcol, grow, g["conv0"], g["rec0"], gdn_conv_w[0], gdn_norm_w, n_seq, seq_len)
        x1, h2, pos, rw, counts, tail, post, tbl = _merge(g["x"], on, proj, g["sc0"], g1, sc2, sh2, w_a, w_b, w_o,
                                                          sc_conv_w[0], norm2_w, w_r, b_r, counts)
        proj3 = proj.reshape(n_seq, seq_len, PROJ_MAIN_W)
        sb, tb = _route_tile(n_seq, seq_len)
        g.update(x1=x1, h2=h2, pos=pos, rw=rw, post=post, tbl=tbl, rec=rec, g2=g2, sb=sb, tb=tb, tm=sb * tb,
                 rows_after=counts[0, :N_EXPERTS].astype(I32),
                 new_conv=proj3[:, seq_len - (GDN_CONV - 1):, :QKV_W],
                 new_sc=tail.reshape(n_seq, -1, d)[:, -(SC_CONV - 1):, :])

    rows_e = counts[0, :N_EXPERTS].astype(I32)
    padded = (rows_e + EXPERT_BLOCK - 1) // EXPERT_BLOCK * EXPERT_BLOCK
    pad_end = jnp.cumsum(padded)
    expert0 = (pad_end - padded).astype(I32)
    tbl_all = jnp.concatenate([g["tbl"] for g in groups], axis=0).reshape(-1, SUBLANES, ROUTER_LANES)
    n_tiles = tbl_all.shape[0]
    tbl_all = tbl_all[:, :3, :N_EXPERTS].at[:, 2, :].add(expert0[None, :]).reshape(-1)
    rows_first = groups[0]["rows_after"]
    tails = jnp.concatenate([expert0 + rows_first, padded - rows_first, pad_end[-1:]]).astype(I32)
    max_rows = n_tok * TOP_K + n_tiles * N_EXPERTS * (SUBLANES - 1)
    n_blocks = -(-max_rows // EXPERT_BLOCK) + N_EXPERTS
    block_start = jnp.arange(n_blocks, dtype=I32) * EXPERT_BLOCK
    block_e = jnp.minimum(jnp.sum((pad_end[None, :] <= block_start[:, None]).astype(I32), axis=1), N_EXPERTS - 1)
    n_valid = (pad_end[-1:] // EXPERT_BLOCK).astype(I32)

    buf = None
    tile0 = 0
    for g in groups:
        g["tile0"] = tile0
        buf = _dispatch(tbl_all, tails, g["post"], g["h2"], buf, n_blocks * EXPERT_BLOCK, g["tm"], tile0)
        tile0 += g["h2"].shape[0] // g["tm"]
    ys = _experts(block_e, n_valid, buf, w_gate[0], b_gate[0], w_up[0], b_up[0], w_down[0], b_down[0])
    outs = []
    for g in groups:
        n_seq, seq_len = g["n_seq"], g["seq_len"]
        x1 = g["x1"].reshape(n_seq, seq_len, d)
        outs.append(_combine(tbl_all, ys, g["pos"], g["rw"], x1, g["g2"], final_norm_w.reshape(1, d),
                             g["sb"], g["tb"], g["tile0"]))

    gp, gs = groups
    return (outs[0], outs[1], gp["new_conv"][None], gp["rec"][None], gp["new_sc"][None],
            gs["new_conv"][None], gs["rec"][None], gs["new_sc"][None])
```

```python
import functools

import jax
import jax.numpy as jnp
from jax import lax
from jax.experimental import pallas as pl
from jax.experimental.pallas import tpu as pltpu

F32 = jnp.float32
BF16 = jnp.bfloat16
I32 = jnp.int32
HIGHEST = lax.Precision.HIGHEST

D_MODEL = 1024
GDN_HEADS = 8
GDN_DK = 128
GDN_DV = 128
GDN_QK = GDN_HEADS * GDN_DK
QKV_W = 3 * GDN_QK
GDN_CONV = 4
GDN_CHUNK = 64
GDN_STEP_CHUNKS = 4
SC_CONV = 3
N_EXPERTS = 32
TOP_K = 4
SWIGLU_LIMIT = 7.0
SWIGLU_ALPHA = 1.702
NORM_EPS = 1e-6
N_GATE_COLS = 2 * GDN_HEADS
PROJ_MAIN_W = 9 * D_MODEL
ROUTER_LANES = 128
EXPERT_BLOCK = 512
SUBLANES = 8
VMEM_LIMIT = 56 * 1024 * 1024

_NT = (((1,), (1,)), ((), ()))
_TN = (((0,), (0,)), ((), ()))


def _bdot(a, b):
    return jnp.dot(a.astype(BF16), b.astype(BF16), preferred_element_type=F32)


def _bdot_nt(a, b):
    return lax.dot_general(a.astype(BF16), b.astype(BF16), _NT, preferred_element_type=F32)


def _bdot_tn(a, b):
    return lax.dot_general(a.astype(BF16), b.astype(BF16), _TN, preferred_element_type=F32)


def _silu(x):
    return x * jax.nn.sigmoid(x)


def _softplus(x):
    return jnp.maximum(x, 0.0) + jnp.log1p(jnp.exp(-jnp.abs(x)))


def _seq_tile(n_seq, seq_len, target):
    if seq_len >= target:
        assert seq_len % target == 0
        return 1, target
    sb = min(n_seq, target // seq_len)
    assert n_seq % sb == 0 and seq_len % SUBLANES == 0
    return sb, seq_len


def _route_tile(n_seq, seq_len):
    return _seq_tile(n_seq, seq_len, 512 if seq_len >= 512 else 256)


def _params(sem):
    return pltpu.CompilerParams(dimension_semantics=sem, vmem_limit_bytes=VMEM_LIMIT)


def _ada_body(c_ref, w_ref, b_ref, o_ref):
    o_ref[...] = _bdot(_silu(c_ref[...]), w_ref[...]) + b_ref[...]


def _ada(c, w_ada, b_ada):
    rows, d = c.shape
    n = w_ada.shape[1]
    tn = 1024
    return pl.pallas_call(
        _ada_body,
        grid=(n // tn,),
        in_specs=[pl.BlockSpec((rows, d), lambda j: (0, 0)),
                  pl.BlockSpec((d, tn), lambda j: (0, j)),
                  pl.BlockSpec((1, tn), lambda j: (0, j))],
        out_specs=pl.BlockSpec((rows, tn), lambda j: (0, j)),
        out_shape=jax.ShapeDtypeStruct((rows, n), F32),
        compiler_params=_params(("arbitrary",)),
        name="ada",
    )(c, w_ada, b_ada.reshape(1, n))


def _gates(v, a_log, dt_bias, axis):
    is_beta = lax.broadcasted_iota(I32, v.shape, axis) < GDN_HEADS
    beta = jax.nn.sigmoid(v)
    g = -jnp.exp(a_log) * _softplus(v + dt_bias)
    return jnp.where(is_beta, beta, g)


def _proj_body(x_ref, sc_ref, sh_ref, nw_ref, w_ref, wg_ref, wgt_ref, prow_ref, pcol_ref,
               o_ref, gc_ref, gr_ref, h_scr):
    @pl.when(pl.program_id(2) == 0)
    def _():
        x = x_ref[...]
        y = x * lax.rsqrt(jnp.mean(x * x, axis=-1, keepdims=True) + NORM_EPS) * nw_ref[...]
        h = (y * (1.0 + sc_ref[...]) + sh_ref[...]).reshape(h_scr.shape).astype(BF16)
        h_scr[...] = h
        gc = jnp.dot(h, wg_ref[...], preferred_element_type=F32)
        gr = lax.dot_general(wgt_ref[...], h, _NT, preferred_element_type=F32)
        gc_ref[...] = _gates(gc, prow_ref[0:1, :], prow_ref[1:2, :], 1)
        gr_ref[...] = _gates(gr, pcol_ref[:, 0:1], pcol_ref[:, 1:2], 0)

    o_ref[...] = jnp.dot(h_scr[...], w_ref[...], preferred_element_type=F32)


def _proj(x, sc, sh, norm_w, w_main, w_g, w_gt, p_row, p_col):
    n_seq, seq_len, d = x.shape
    sb, tb = _seq_tile(n_seq, seq_len, 2048 if seq_len >= 2048 else 1024)
    tm = sb * tb
    tn = 1024
    nt = seq_len // tb
    n_tok = n_seq * seq_len
    row = lambda s, t, j: s * nt + t
    return pl.pallas_call(
        _proj_body,
        grid=(n_seq // sb, nt, PROJ_MAIN_W // tn),
        in_specs=[pl.BlockSpec((sb, tb, d), lambda s, t, j: (s, t, 0)),
                  pl.BlockSpec((sb, 1, d), lambda s, t, j: (s, 0, 0)),
                  pl.BlockSpec((sb, 1, d), lambda s, t, j: (s, 0, 0)),
                  pl.BlockSpec((1, d), lambda s, t, j: (0, 0)),
                  pl.BlockSpec((d, tn), lambda s, t, j: (0, j)),
                  pl.BlockSpec((d, N_GATE_COLS), lambda s, t, j: (0, 0)),
                  pl.BlockSpec((N_GATE_COLS, d), lambda s, t, j: (0, 0)),
                  pl.BlockSpec((2, N_GATE_COLS), lambda s, t, j: (0, 0)),
                  pl.BlockSpec((N_GATE_COLS, 2), lambda s, t, j: (0, 0))],
        out_specs=[pl.BlockSpec((tm, tn), lambda s, t, j: (row(s, t, j), j)),
                   pl.BlockSpec((tm, N_GATE_COLS), lambda s, t, j: (row(s, t, j), 0)),
                   pl.BlockSpec((N_GATE_COLS, tm), lambda s, t, j: (0, row(s, t, j)))],
        out_shape=[jax.ShapeDtypeStruct((n_tok, PROJ_MAIN_W), F32),
                   jax.ShapeDtypeStruct((n_tok, N_GATE_COLS), F32),
                   jax.ShapeDtypeStruct((N_GATE_COLS, n_tok), F32)],
        scratch_shapes=[pltpu.VMEM((tm, d), BF16)],
        compiler_params=_params(("arbitrary", "arbitrary", "arbitrary")),
        name="proj",
    )(x, sc, sh, norm_w, w_main, w_g, w_gt, p_row, p_col)


def _shift_rows(x, hist, s, per_seq):
    rows, width = x.shape
    x3 = x.reshape(rows // SUBLANES, SUBLANES, width)
    xr = pltpu.roll(x3, s, 1)
    if per_seq:
        src = pltpu.roll(hist.reshape(x3.shape), s, 1)
    else:
        hr = pltpu.roll(hist.reshape(1, SUBLANES, width), s, 1)
        src = hr if rows == SUBLANES else jnp.concatenate([hr, xr[:-1]], axis=0)
    sub = lax.broadcasted_iota(I32, x3.shape, 1)
    return jnp.where(sub < s, src, xr).reshape(rows, width)


def _causal_conv(x, hist, w, per_seq):
    taps = w.shape[0]
    acc = x * w[taps - 1:taps, :]
    for s in range(1, taps):
        acc = acc + _shift_rows(x, hist, s, per_seq) * w[taps - 1 - s:taps - s, :]
    return acc


def _gdn_body(qkv_ref, prev_ref, conv0_ref, z_ref, gc_ref, gr_ref, s0_ref, cw_ref, nw_ref,
              o_ref, s_ref, *, chunk):
    t = pl.program_id(1)

    @pl.when(t == 0)
    def _():
        s_ref[...] = s0_ref[...]

    stacked = s_ref.shape[0] > 1
    x = qkv_ref[...]
    if stacked:
        hist = conv0_ref[...].reshape(x.shape)
    else:
        hist = jnp.where(t == 0, conv0_ref[0], prev_ref[...])
    qkvc = _silu(_causal_conv(x, hist, cw_ref[...], stacked))

    gcol = gc_ref[...]
    n_chunks = gr_ref.shape[0]
    ri = lax.broadcasted_iota(I32, (chunk, chunk), 0)
    ci = lax.broadcasted_iota(I32, (chunk, chunk), 1)
    causal = ri >= ci
    strict = ri > ci
    tri = causal.astype(F32)
    eye = (ri == ci).astype(F32)
    n_lvl = chunk.bit_length() - 1
    blk = [lax.shift_right_logical(ri, l) == lax.shift_right_logical(ci, l) for l in range(1, n_lvl + 1)]
    pair = [blk[l] & jnp.logical_not(blk[l - 1]) for l in range(1, n_lvl)]

    units = [(c, h) for c in range(n_chunks) for h in range(GDN_HEADS)]
    every = range(len(units))
    q, k, v, beta, dcol, dlast, decay, edec = [], [], [], [], [], [], [], []
    for c in range(n_chunks):
        rows = slice(c * chunk, (c + 1) * chunk)
        dec_col = jnp.dot(tri, gcol[rows, GDN_HEADS:], precision=HIGHEST, preferred_element_type=F32)
        dec_row = lax.dot_general(gr_ref[c][GDN_HEADS:, :], tri, _NT, precision=HIGHEST,
                                  preferred_element_type=F32)
        for h in range(GDN_HEADS):
            qh = qkvc[rows, h * GDN_DK:(h + 1) * GDN_DK]
            kh = qkvc[rows, GDN_QK + h * GDN_DK:GDN_QK + (h + 1) * GDN_DK]
            q.append(qh * (lax.rsqrt(jnp.sum(qh * qh, axis=-1, keepdims=True) + 1e-6) * (GDN_DK ** -0.5)))
            k.append(kh * lax.rsqrt(jnp.sum(kh * kh, axis=-1, keepdims=True) + 1e-6))
            v.append(qkvc[rows, 2 * GDN_QK + h * GDN_DV:2 * GDN_QK + (h + 1) * GDN_DV])
            beta.append(gcol[rows, h:h + 1])
            dcol.append(dec_col[:, h:h + 1])
            dlast.append(dec_col[chunk - 1:chunk, h:h + 1])
            decay.append(jnp.where(causal, jnp.exp(dcol[-1] - dec_row[h:h + 1, :]), 0.0))
            edec.append(jnp.exp(dcol[-1]))
    kk = [_bdot_nt(k[i], k[i]) for i in every]
    qk = [_bdot_nt(q[i], k[i]) * decay[i] for i in every]
    a = [jnp.where(strict, beta[i] * kk[i] * decay[i], 0.0) for i in every]
    inv = [eye - jnp.where(blk[0], a[i], 0.0) for i in every]
    for lower_left in pair:
        right = [_bdot(jnp.where(lower_left, a[i], 0.0), inv[i]) for i in every]
        inv = [inv[i] - _bdot(inv[i], right[i]) for i in every]
    sol = [_bdot(inv[i], jnp.concatenate([beta[i] * v[i], (beta[i] * edec[i]) * k[i]], axis=1)) for i in every]
    seq_of = (lambda c: c) if stacked else (lambda c: 0)
    waves = [list(every)] if stacked else [[c * GDN_HEADS + h for h in range(GDN_HEADS)] for c in range(n_chunks)]
    state = {(seq_of(c), h): s_ref[seq_of(c), h] for c, h in units}
    for wave in waves:
        key = {i: (seq_of(units[i][0]), units[i][1]) for i in wave}
        ws = {i: _bdot(jnp.concatenate([sol[i][:, GDN_DV:], q[i] * edec[i]], axis=0), state[key[i]]) for i in wave}
        u = {i: sol[i][:, :GDN_DV] - ws[i][:chunk] for i in wave}
        o = {i: ws[i][chunk:] + _bdot(qk[i], u[i]) for i in wave}
        upd = {i: _bdot_tn(k[i] * jnp.exp(dlast[i] - dcol[i]), u[i]) for i in wave}
        for i in wave:
            c, h = units[i]
            state[key[i]] = state[key[i]] * jnp.exp(dlast[i]) + upd[i]
            rows = slice(c * chunk, (c + 1) * chunk)
            on = o[i] * lax.rsqrt(jnp.mean(o[i] * o[i], axis=-1, keepdims=True) + NORM_EPS) * nw_ref[...]
            o_ref[rows, h * GDN_DV:(h + 1) * GDN_DV] = on * _silu(z_ref[rows, h * GDN_DV:(h + 1) * GDN_DV])
    for (b, h), value in state.items():
        s_ref[b, h] = value


def _gdn(proj, gcol, grow, conv0, state0, conv_w, norm_w, n_seq, seq_len):
    chunk = min(GDN_CHUNK, seq_len)
    assert seq_len % chunk == 0 and chunk % SUBLANES == 0
    n_tok = n_seq * seq_len
    stacked = seq_len == SUBLANES and n_seq % GDN_STEP_CHUNKS == 0
    if stacked:
        n_step_chunks, sb, nt = GDN_STEP_CHUNKS, GDN_STEP_CHUNKS, 1
    else:
        n_step_chunks = GDN_STEP_CHUNKS if (seq_len // chunk) % GDN_STEP_CHUNKS == 0 else 1
        sb, nt = 1, seq_len // (n_step_chunks * chunk)
    step_rows = n_step_chunks * chunk
    cpb = step_rows // SUBLANES
    row = lambda s, t: s * nt + t
    grow_chunks = grow.reshape(N_GATE_COLS, n_tok // chunk, chunk).transpose(1, 0, 2)
    return pl.pallas_call(
        functools.partial(_gdn_body, chunk=chunk),
        grid=(n_seq // sb, nt),
        in_specs=[pl.BlockSpec((step_rows, QKV_W), lambda s, t: (row(s, t), 0)),
                  pl.BlockSpec((SUBLANES, QKV_W), lambda s, t: (jnp.maximum(row(s, t) * cpb - 1, 0), 0)),
                  pl.BlockSpec((sb, SUBLANES, QKV_W), lambda s, t: (s, 0, 0)),
                  pl.BlockSpec((step_rows, D_MODEL), lambda s, t: (row(s, t), 3)),
                  pl.BlockSpec((step_rows, N_GATE_COLS), lambda s, t: (row(s, t), 0)),
                  pl.BlockSpec((n_step_chunks, N_GATE_COLS, chunk), lambda s, t: (row(s, t), 0, 0)),
                  pl.BlockSpec((sb, GDN_HEADS, GDN_DK, GDN_DV), lambda s, t: (s, 0, 0, 0)),
                  pl.BlockSpec((GDN_CONV, QKV_W), lambda s, t: (0, 0)),
                  pl.BlockSpec((1, GDN_DV), lambda s, t: (0, 0))],
        out_specs=[pl.BlockSpec((step_rows, D_MODEL), lambda s, t: (row(s, t), 0)),
                   pl.BlockSpec((sb, GDN_HEADS, GDN_DK, GDN_DV), lambda s, t: (s, 0, 0, 0))],
        out_shape=[jax.ShapeDtypeStruct((n_tok, D_MODEL), F32),
                   jax.ShapeDtypeStruct((n_seq, GDN_HEADS, GDN_DK, GDN_DV), F32)],
        compiler_params=_params(("arbitrary", "arbitrary")),
        name="gdn",
    )(proj, proj, conv0, proj, gcol, grow_chunks, state0, conv_w, norm_w)


def _merge_body(x_ref, on_ref, b_ref, c_ref, h_ref, ga_ref, gb_ref, cprev_ref, hprev_ref, sc0_ref,
                g1_ref, sc2_ref, sh2_ref, wa_ref, wb_ref, wo_ref, cw_ref, n2_ref, wr_ref, br_ref, cnt0_ref,
                x1_ref, h2_ref, pos_ref, rw_ref, cnt_ref, tail_ref, post_ref, tbl_ref, cnt_scr,
                *, per_seq, n_tiles):
    step = pl.program_id(0) * pl.num_programs(1) + pl.program_id(1)

    @pl.when(step == 0)
    def _():
        cnt_scr[...] = cnt0_ref[...]

    tm, d = on_ref.shape
    pre = c_ref[...] * h_ref[...]
    if per_seq:
        hist = sc0_ref[...].reshape(tm, d)
    else:
        hist = jnp.where(pl.program_id(1) == 0, sc0_ref[0], cprev_ref[...] * hprev_ref[...])
    tail_ref[...] = pre[tm - SUBLANES:, :] if not per_seq else pre
    y_b = _bdot(b_ref[...] * _causal_conv(pre, hist, cw_ref[...], per_seq), wb_ref[...])
    y_a = _bdot(on_ref[...], wa_ref[...])
    merged = jax.nn.sigmoid(ga_ref[...]) * y_a + jax.nn.sigmoid(gb_ref[...]) * y_b
    mo = _bdot(merged, wo_ref[...]).reshape(x_ref.shape)
    x1 = x_ref[...] + g1_ref[...] * mo
    y = x1 * lax.rsqrt(jnp.mean(x1 * x1, axis=-1, keepdims=True) + NORM_EPS) * n2_ref[...]
    h2 = (y * (1.0 + sc2_ref[...]) + sh2_ref[...]).reshape(tm, d)
    x1_ref[...] = x1.reshape(tm, d)
    h2_ref[...] = h2

    h2_hi = h2.astype(BF16)
    h2_lo = (h2 - h2_hi.astype(F32)).astype(BF16)
    w_hi = wr_ref[0]
    logits = (jnp.dot(h2_hi, w_hi, preferred_element_type=F32)
              + (jnp.dot(h2_hi, wr_ref[1], preferred_element_type=F32)
                 + jnp.dot(h2_lo, w_hi, preferred_element_type=F32))) + br_ref[...]
    lane = lax.broadcasted_iota(I32, logits.shape, 1)
    lane_f = lane.astype(F32)
    work = logits
    vals, hots = [], []
    member = jnp.zeros(logits.shape, F32)
    for _ in range(TOP_K):
        m = jnp.max(work, axis=-1, keepdims=True)
        sel = jnp.min(jnp.where(work == m, lane_f, float(N_EXPERTS - 1)), axis=-1, keepdims=True)
        hot = lane_f == sel
        vals.append(m)
        hots.append(hot)
        member = member + hot.astype(F32)
        work = jnp.where(hot, -jnp.inf, work)
    exps = [jnp.exp(v - vals[0]) for v in vals]
    denom = exps[0] + exps[1] + exps[2] + exps[3]
    ti = lax.broadcasted_iota(I32, (tm, tm), 0)
    tj = lax.broadcasted_iota(I32, (tm, tm), 1)
    before = (tj < ti).astype(BF16)
    rank_loc = jnp.dot(before, member.astype(BF16), preferred_element_type=F32)
    cnt = jnp.sum(member, axis=0, keepdims=True).astype(I32)
    seg = lax.shift_left(lax.shift_right_logical(cnt + (SUBLANES - 1), 3), 3)
    seg8 = jnp.broadcast_to(seg.astype(F32), (SUBLANES, ROUTER_LANES))
    ei = lax.broadcasted_iota(I32, (ROUTER_LANES, ROUTER_LANES), 0)
    ej = lax.broadcasted_iota(I32, (ROUTER_LANES, ROUTER_LANES), 1)
    base8 = jnp.dot(seg8, (ei < ej).astype(F32), precision=HIGHEST, preferred_element_type=F32)
    row_all = base8[0:1, :] + rank_loc
    pos = jnp.zeros((tm, ROUTER_LANES), F32)
    rw = jnp.zeros((tm, ROUTER_LANES), F32)
    for kk in range(TOP_K):
        pos_k = jnp.sum(jnp.where(hots[kk], row_all, 0.0), axis=-1, keepdims=True)
        pos = jnp.where(lane == kk, pos_k, pos)
        rw = jnp.where(lane == kk, exps[kk] / denom, rw)
    pos_ref[...] = pos
    rw_ref[...] = rw
    pick = (lax.broadcasted_iota(I32, (SUBLANES, ROUTER_LANES), 0)
            == lax.broadcasted_iota(I32, (SUBLANES, ROUTER_LANES), 1)).astype(F32)
    post_ref[...] = lax.dot_general(pick, pos, _NT, precision=HIGHEST, preferred_element_type=F32)
    sub = lax.broadcasted_iota(I32, (SUBLANES, ROUTER_LANES), 0)
    tbl_ref[...] = jnp.where(sub == 0, base8.astype(I32),
                             jnp.where(sub == 1, seg8.astype(I32), cnt_scr[...].astype(I32)))
    cnt_scr[...] = cnt_scr[...] + seg8

    @pl.when(step == n_tiles - 1)
    def _():
        cnt_ref[...] = cnt_scr[...]


def _merge(x, on, proj, sc0, g1, sc2, sh2, w_a, w_b, w_o, conv_w, norm2_w, w_r, b_r, cnt0):
    n_seq, seq_len, d = x.shape
    sb, tb = _route_tile(n_seq, seq_len)
    per_seq = sb > 1 or tb == SUBLANES
    if per_seq:
        assert tb == SUBLANES
    tm = sb * tb
    nt = seq_len // tb
    ns = n_seq // sb
    n_tok = n_seq * seq_len
    rpb = tm // SUBLANES
    row = lambda s, t: s * nt + t
    prev = lambda s, t: jnp.maximum(row(s, t) * rpb - 1, 0)
    col = lambda j: (lambda s, t: (row(s, t), j))
    full = lambda shape: pl.BlockSpec(shape, lambda s, t: (0,) * len(shape))
    ada = pl.BlockSpec((sb, 1, d), lambda s, t: (s, 0, 0))
    tok = pl.BlockSpec((tm, d), lambda s, t: (row(s, t), 0))
    lanes = pl.BlockSpec((tm, ROUTER_LANES), lambda s, t: (row(s, t), 0))
    tail_rows = tm if per_seq else SUBLANES
    return pl.pallas_call(
        functools.partial(_merge_body, per_seq=per_seq, n_tiles=ns * nt),
        grid=(ns, nt),
        in_specs=[pl.BlockSpec((sb, tb, d), lambda s, t: (s, t, 0)),
                  tok,
                  pl.BlockSpec((tm, d), col(4)), pl.BlockSpec((tm, d), col(5)), pl.BlockSpec((tm, d), col(6)),
                  pl.BlockSpec((tm, d), col(7)), pl.BlockSpec((tm, d), col(8)),
                  pl.BlockSpec((SUBLANES, d), lambda s, t: (prev(s, t), 5)),
                  pl.BlockSpec((SUBLANES, d), lambda s, t: (prev(s, t), 6)),
                  pl.BlockSpec((sb, SUBLANES, d), lambda s, t: (s, 0, 0)),
                  ada, ada, ada,
                  full((d, d)), full((d, d)), full((d, d)),
                  full((SC_CONV, d)), full((1, d)), full((2, d, ROUTER_LANES)), full((1, ROUTER_LANES)),
                  full((SUBLANES, ROUTER_LANES))],
        out_specs=[tok, tok, lanes, lanes, full((SUBLANES, ROUTER_LANES)),
                   pl.BlockSpec((tail_rows, d), lambda s, t: (row(s, t), 0)),
                   pl.BlockSpec((SUBLANES, tm), lambda s, t: (0, row(s, t))),
                   pl.BlockSpec((SUBLANES, ROUTER_LANES), lambda s, t: (row(s, t), 0))],
        out_shape=[jax.ShapeDtypeStruct((n_tok, d), F32),
                   jax.ShapeDtypeStruct((n_tok, d), F32),
                   jax.ShapeDtypeStruct((n_tok, ROUTER_LANES), F32),
                   jax.ShapeDtypeStruct((n_tok, ROUTER_LANES), F32),
                   jax.ShapeDtypeStruct((SUBLANES, ROUTER_LANES), F32),
                   jax.ShapeDtypeStruct((ns * nt * tail_rows, d), F32),
                   jax.ShapeDtypeStruct((SUBLANES, n_tok), F32),
                   jax.ShapeDtypeStruct((ns * nt * SUBLANES, ROUTER_LANES), I32)],
        scratch_shapes=[pltpu.VMEM((SUBLANES, ROUTER_LANES), F32)],
        compiler_params=_params(("arbitrary", "arbitrary")),
        name="merge",
    )(x, on, proj, proj, proj, proj, proj, proj, proj, sc0, g1, sc2, sh2, w_a, w_b, w_o, conv_w, norm2_w,
      w_r, b_r, cnt0)


def _pow2_pieces(length, max_rows):
    out = []
    rows = max_rows
    while rows >= SUBLANES:
        shift = rows.bit_length()
        offset = lax.shift_left(lax.shift_right_logical(length, shift), shift)
        out.append(((length & rows) != 0, offset, rows))
        rows //= 2
    return out


def _segment_copies(tbl_ref, tile_id, local_ref, global_ref, sem, to_global, max_rows):
    out = []
    base = tile_id * (3 * N_EXPERTS)
    for e in range(N_EXPERTS):
        local0 = tbl_ref[base + e]
        length = tbl_ref[base + N_EXPERTS + e]
        global0 = tbl_ref[base + 2 * N_EXPERTS + e]
        for pred, offset, rows in _pow2_pieces(length, max_rows):
            loc = local_ref.at[pl.ds(pl.multiple_of(local0 + offset, SUBLANES), rows)]
            glo = global_ref.at[pl.ds(pl.multiple_of(global0 + offset, SUBLANES), rows)]
            cp = pltpu.make_async_copy(loc, glo, sem) if to_global else pltpu.make_async_copy(glo, loc, sem)
            out.append((pred, cp))
    return out


def _start_all(copies):
    for pred, cp in copies:
        pl.when(pred)(cp.start)


def _wait_all(copies):
    for pred, cp in copies:
        pl.when(pred)(cp.wait)


def _dispatch_body(tbl_ref, tails_ref, post_ref, h2_ref, *refs, tile0, zero_tails):
    buf_ref, sorted_scr, zero_scr, sems = refs[-4:]
    i = pl.program_id(0)
    slot = lax.rem(i, 2)
    tm = h2_ref.shape[0]
    r_rows = sorted_scr.shape[1]

    if zero_tails:
        @pl.when(i == 0)
        def _():
            zero_scr[...] = jnp.zeros(zero_scr.shape, F32)
            z_rows = zero_scr.shape[0]

            def fill(b, carry):
                cp = pltpu.make_async_copy(zero_scr, buf_ref.at[pl.ds(pl.multiple_of(b * SUBLANES, SUBLANES), z_rows)],
                                           sems.at[2])
                cp.start()
                cp.wait()
                return carry

            tails = []
            for e in range(N_EXPERTS):
                start, length = tails_ref[e], tails_ref[N_EXPERTS + e]
                whole = length // z_rows
                lax.fori_loop(0, whole, lambda b, c, s=start: fill(s // SUBLANES + b * (z_rows // SUBLANES), c), 0)
                rest0 = start + whole * z_rows
                for pred, offset, rows in _pow2_pieces(length - whole * z_rows, z_rows // 2):
                    at = pl.multiple_of(rest0 + offset, SUBLANES)
                    tails.append((pred, pltpu.make_async_copy(zero_scr.at[pl.ds(0, rows)],
                                                              buf_ref.at[pl.ds(at, rows)], sems.at[2])))
            _start_all(tails)
            _wait_all(tails)
            lax.fori_loop(tails_ref[2 * N_EXPERTS] // z_rows, buf_ref.shape[0] // z_rows,
                          lambda b, c: fill(b * (z_rows // SUBLANES), c), 0)

    pos = post_ref[...].astype(I32)
    j = lax.broadcasted_iota(I32, (r_rows, tm), 0)
    onehot = jnp.zeros((r_rows, tm), F32)
    for k in range(TOP_K):
        onehot = jnp.where(j == pos[k:k + 1, :], 1.0, onehot)
    sorted_scr[slot] = _bdot(onehot, h2_ref[...])

    _start_all(_segment_copies(tbl_ref, tile0 + i, sorted_scr.at[slot], buf_ref, sems.at[slot], True, tm))

    @pl.when(i > 0)
    def _():
        _wait_all(_segment_copies(tbl_ref, tile0 + i - 1, sorted_scr.at[1 - slot], buf_ref, sems.at[1 - slot],
                                  True, tm))

    @pl.when(i == pl.num_programs(0) - 1)
    def _():
        _wait_all(_segment_copies(tbl_ref, tile0 + i, sorted_scr.at[slot], buf_ref, sems.at[slot], True, tm))


def _dispatch(tbl, tails, post, h2, buf, buf_rows, tm, tile0):
    n_tok, d = h2.shape
    r_rows = TOP_K * tm + N_EXPERTS * SUBLANES
    first = buf is None
    grid_spec = pltpu.PrefetchScalarGridSpec(
        num_scalar_prefetch=2,
        grid=(n_tok // tm,),
        in_specs=[pl.BlockSpec((SUBLANES, tm), lambda i, *_: (0, i)),
                  pl.BlockSpec((tm, d), lambda i, *_: (i, 0))]
                 + ([] if first else [pl.BlockSpec(memory_space=pl.ANY)]),
        out_specs=pl.BlockSpec(memory_space=pl.ANY),
        scratch_shapes=[pltpu.VMEM((2, r_rows, d), F32),
                        pltpu.VMEM((EXPERT_BLOCK // 2, d), F32),
                        pltpu.SemaphoreType.DMA((3,))],
    )
    return pl.pallas_call(
        functools.partial(_dispatch_body, tile0=tile0, zero_tails=first),
        grid_spec=grid_spec,
        out_shape=jax.ShapeDtypeStruct((buf_rows, d), F32),
        input_output_aliases={} if first else {4: 0},
        compiler_params=pltpu.CompilerParams(dimension_semantics=("arbitrary",), has_side_effects=True,
                                             vmem_limit_bytes=VMEM_LIMIT),
        name="dispatch",
    )(tbl, tails, post, h2, *([] if first else [buf]))


def _expert_body(be_ref, nv_ref, x_ref, wg_ref, bg_ref, wu_ref, bu_ref, wd_ref, bd_ref, o_ref,
                 wg_s, wu_s, wd_s):
    i = pl.program_id(0)

    @pl.when(i < nv_ref[0])
    def _():
        @pl.when((i == 0) | (be_ref[i] != be_ref[jnp.maximum(i - 1, 0)]))
        def _():
            wg_s[...] = wg_ref[0].astype(BF16)
            wu_s[...] = wu_ref[0].astype(BF16)
            wd_s[...] = wd_ref[0].astype(BF16)

        x = x_ref[...].astype(BF16)
        gate = jnp.dot(x, wg_s[...], preferred_element_type=F32) + bg_ref[0]
        up = jnp.dot(x, wu_s[...], preferred_element_type=F32) + bu_ref[0]
        gate = jnp.minimum(gate, SWIGLU_LIMIT)
        up = jnp.clip(up, -SWIGLU_LIMIT, SWIGLU_LIMIT)
        glu = gate * jax.nn.sigmoid(SWIGLU_ALPHA * gate)
        o_ref[...] = _bdot((up + 1.0) * glu, wd_s[...]) + bd_ref[0]

    @pl.when(i >= nv_ref[0])
    def _():
        o_ref[...] = jnp.zeros(o_ref.shape, F32)


def _experts(block_e, n_valid, xs, w_gate, b_gate, w_up, b_up, w_down, b_down):
    m_pad, d = xs.shape
    n_blocks = m_pad // EXPERT_BLOCK
    f = w_gate.shape[2]
    blk = lambda i, be, nv: (jnp.maximum(jnp.minimum(i, nv[0] - 1), 0), 0)
    wspec = lambda a, b: pl.BlockSpec((1, a, b), lambda i, be, nv: (be[i], 0, 0))
    grid_spec = pltpu.PrefetchScalarGridSpec(
        num_scalar_prefetch=2,
        grid=(n_blocks,),
        in_specs=[pl.BlockSpec((EXPERT_BLOCK, d), blk),
                  wspec(d, f), wspec(1, f), wspec(d, f), wspec(1, f), wspec(f, d), wspec(1, d)],
        out_specs=pl.BlockSpec((EXPERT_BLOCK, d), lambda i, be, nv: (i, 0)),
        scratch_shapes=[pltpu.VMEM((d, f), BF16), pltpu.VMEM((d, f), BF16), pltpu.VMEM((f, d), BF16)],
    )
    return pl.pallas_call(
        _expert_body,
        grid_spec=grid_spec,
        out_shape=jax.ShapeDtypeStruct((m_pad, d), F32),
        compiler_params=_params(("arbitrary",)),
        name="experts",
    )(block_e, n_valid, xs, w_gate, b_gate[:, None, :], w_up, b_up[:, None, :], w_down, b_down[:, None, :])


def _combine_body(tbl_ref, ys_ref, pos_ref, rw_ref, x1_ref, g2_ref, fw_ref, o_ref, blk_scr, sems, *, tile0):
    step = pl.program_id(0) * pl.num_programs(1) + pl.program_id(1)
    n_steps = pl.num_programs(0) * pl.num_programs(1)
    slot = lax.rem(step, 2)
    tm = pos_ref.shape[0]
    r_rows = blk_scr.shape[1]

    def fetch(tile, into):
        return _segment_copies(tbl_ref, tile0 + tile, blk_scr.at[into], ys_ref, sems.at[into], False, tm)

    @pl.when(step == 0)
    def _():
        blk_scr[...] = jnp.zeros(blk_scr.shape, F32)
        _start_all(fetch(0, 0))

    @pl.when(step + 1 < n_steps)
    def _():
        _start_all(fetch(step + 1, 1 - slot))

    _wait_all(fetch(step, slot))

    pos = pos_ref[...].astype(I32)
    rw = rw_ref[...]
    j = lax.broadcasted_iota(I32, (tm, r_rows), 1)
    pw = jnp.zeros((tm, r_rows), F32)
    for k in range(TOP_K):
        pw = jnp.where(j == pos[:, k:k + 1], rw[:, k:k + 1], pw)
    ffn = _bdot(pw, blk_scr[slot])
    y = x1_ref[...] + g2_ref[...] * ffn.reshape(x1_ref.shape)
    o_ref[...] = y * lax.rsqrt(jnp.mean(y * y, axis=-1, keepdims=True) + NORM_EPS) * fw_ref[...]


def _combine(tbl, ys, pos, rw, x1, g2, final_w, sb, tb, tile0):
    n_seq, seq_len, d = x1.shape
    tm = sb * tb
    nt = seq_len // tb
    r_rows = TOP_K * tm + N_EXPERTS * SUBLANES
    row = lambda s, t: s * nt + t
    grid_spec = pltpu.PrefetchScalarGridSpec(
        num_scalar_prefetch=1,
        grid=(n_seq // sb, nt),
        in_specs=[pl.BlockSpec(memory_space=pl.ANY),
                  pl.BlockSpec((tm, ROUTER_LANES), lambda s, t, *_: (row(s, t), 0)),
                  pl.BlockSpec((tm, ROUTER_LANES), lambda s, t, *_: (row(s, t), 0)),
                  pl.BlockSpec((sb, tb, d), lambda s, t, *_: (s, t, 0)),
                  pl.BlockSpec((sb, 1, d), lambda s, t, *_: (s, 0, 0)),
                  pl.BlockSpec((1, d), lambda s, t, *_: (0, 0))],
        out_specs=pl.BlockSpec((sb, tb, d), lambda s, t, *_: (s, t, 0)),
        scratch_shapes=[pltpu.VMEM((2, r_rows, d), F32), pltpu.SemaphoreType.DMA((2,))],
    )
    return pl.pallas_call(
        functools.partial(_combine_body, tile0=tile0),
        grid_spec=grid_spec,
        out_shape=jax.ShapeDtypeStruct((n_seq, seq_len, d), F32),
        compiler_params=_params(("arbitrary", "arbitrary")),
        name="combine",
    )(tbl, ys, pos, rw, x1, g2, final_w)


def _pad_state(state, rows):
    return jnp.pad(state, ((0, 0), (rows - state.shape[1], 0), (0, 0)))


def kernel(x_prompt, x_sample, c_prompt, c_sample, state_gdn_conv, state_gdn_rec, state_sc_conv, w_ada, b_ada,
           norm1_w, w_in, gdn_conv_w, gdn_a_log, gdn_dt_bias, gdn_norm_w, w_branch_a, sc_conv_w, w_branch_b,
           w_out, norm2_w, w_router, b_router, w_gate, b_gate, w_up, b_up, w_down, b_down, final_norm_w):
    assert w_ada.shape[0] == 1, "single-layer trunk"
    d = D_MODEL
    bp, tp, _ = x_prompt.shape
    bs, ts, _ = x_sample.shape
    n_p, n_s = bp * tp, bs * ts
    n_tok = n_p + n_s

    w_in0 = w_in[0]
    g_lo, g_hi = QKV_W + d, QKV_W + d + N_GATE_COLS
    w_main = jnp.concatenate([w_in0[:, :g_lo], w_in0[:, g_hi:]], axis=1).astype(BF16)
    w_g = w_in0[:, g_lo:g_hi].astype(BF16)
    w_gt = w_g.T
    zeros_h = jnp.zeros((GDN_HEADS,), F32)
    p_row = jnp.stack([jnp.concatenate([zeros_h, gdn_a_log[0]]), jnp.concatenate([zeros_h, gdn_dt_bias[0]])])
    p_col = p_row.T
    w_a = w_branch_a[0].astype(BF16)
    w_b = w_branch_b[0].astype(BF16)
    w_o = w_out[0].astype(BF16)
    w_r = jnp.pad(w_router[0], ((0, 0), (0, ROUTER_LANES - N_EXPERTS)))
    w_r_hi = w_r.astype(BF16)
    w_r = jnp.stack([w_r_hi, (w_r - w_r_hi.astype(F32)).astype(BF16)])
    b_r = jnp.pad(b_router[0], (0, ROUTER_LANES - N_EXPERTS), constant_values=-jnp.inf).reshape(1, ROUTER_LANES)

    n_c = bp + bs
    c_rows = -(-n_c // 16) * 16
    c_all = jnp.pad(jnp.concatenate([c_prompt, c_sample], axis=0), ((0, c_rows - n_c), (0, 0)))
    ada = _ada(c_all, w_ada[0], b_ada[0])

    def ada_parts(lo, hi):
        return [ada[lo:hi, j * d:(j + 1) * d].reshape(hi - lo, 1, d) for j in range(6)]

    groups = [
        dict(x=x_prompt, ada=ada_parts(0, bp), n_seq=bp, seq_len=tp,
             conv0=jnp.zeros((bp, SUBLANES, QKV_W), F32),
             rec0=jnp.zeros((bp, GDN_HEADS, GDN_DK, GDN_DV), F32),
             sc0=jnp.zeros((bp, SUBLANES, d), F32)),
        dict(x=x_sample, ada=ada_parts(bp, n_c), n_seq=bs, seq_len=ts,
             conv0=_pad_state(state_gdn_conv[0], SUBLANES),
             rec0=state_gdn_rec[0],
             sc0=_pad_state(state_sc_conv[0], SUBLANES)),
    ]

    counts = jnp.zeros((SUBLANES, ROUTER_LANES), F32)
    for g in groups:
        sh1, sc1, g1, sh2, sc2, g2 = g["ada"]
        n_seq, seq_len = g["n_seq"], g["seq_len"]
        proj, gcol, grow = _proj(g["x"], sc1, sh1, norm1_w, w_main, w_g, w_gt, p_row, p_col)
        on, rec = _gdn(proj, gcol, grow, g["conv0"], g["rec0"], gdn_conv_w[0], gdn_norm_w, n_seq, seq_len)
        x1, h2, pos, rw, counts, tail, post, tbl = _merge(g["x"], on, proj, g["sc0"], g1, sc2, sh2, w_a, w_b, w_o,
                                                          sc_conv_w[0], norm2_w, w_r, b_r, counts)
        proj3 = proj.reshape(n_seq, seq_len, PROJ_MAIN_W)
        sb, tb = _route_tile(n_seq, seq_len)
        g.update(x1=x1, h2=h2, pos=pos, rw=rw, post=post, tbl=tbl, rec=rec, g2=g2, sb=sb, tb=tb, tm=sb * tb,
                 rows_after=counts[0, :N_EXPERTS].astype(I32),
                 new_conv=proj3[:, seq_len - (GDN_CONV - 1):, :QKV_W],
                 new_sc=tail.reshape(n_seq, -1, d)[:, -(SC_CONV - 1):, :])

    rows_e = counts[0, :N_EXPERTS].astype(I32)
    padded = (rows_e + EXPERT_BLOCK - 1) // EXPERT_BLOCK * EXPERT_BLOCK
    pad_end = jnp.cumsum(padded)
    expert0 = (pad_end - padded).astype(I32)
    tbl_all = jnp.concatenate([g["tbl"] for g in groups], axis=0).reshape(-1, SUBLANES, ROUTER_LANES)
    n_tiles = tbl_all.shape[0]
    tbl_all = tbl_all[:, :3, :N_EXPERTS].at[:, 2, :].add(expert0[None, :]).reshape(-1)
    rows_first = groups[0]["rows_after"]
    tails = jnp.concatenate([expert0 + rows_first, padded - rows_first, pad_end[-1:]]).astype(I32)
    max_rows = n_tok * TOP_K + n_tiles * N_EXPERTS * (SUBLANES - 1)
    n_blocks = -(-max_rows // EXPERT_BLOCK) + N_EXPERTS
    block_start = jnp.arange(n_blocks, dtype=I32) * EXPERT_BLOCK
    block_e = jnp.minimum(jnp.sum((pad_end[None, :] <= block_start[:, None]).astype(I32), axis=1), N_EXPERTS - 1)
    n_valid = (pad_end[-1:] // EXPERT_BLOCK).astype(I32)

    buf = None
    tile0 = 0
    for g in groups:
        g["tile0"] = tile0
        buf = _dispatch(tbl_all, tails, g["post"], g["h2"], buf, n_blocks * EXPERT_BLOCK, g["tm"], tile0)
        tile0 += g["h2"].shape[0] // g["tm"]
    ys = _experts(block_e, n_valid, buf, w_gate[0], b_gate[0], w_up[0], b_up[0], w_down[0], b_down[0])
    outs = []
    for g in groups:
        n_seq, seq_len = g["n_seq"], g["seq_len"]
        x1 = g["x1"].reshape(n_seq, seq_len, d)
        outs.append(_combine(tbl_all, ys, g["pos"], g["rw"], x1, g["g2"], final_norm_w.reshape(1, d),
                             g["sb"], g["tb"], g["tile0"]))

    gp, gs = groups
    return (outs[0], outs[1], gp["new_conv"][None], gp["rec"][None], gp["new_sc"][None],
            gs["new_conv"][None], gs["rec"][None], gs["new_sc"][None])
```

```python
import functools

import jax
import jax.numpy as jnp
from jax import lax
from jax.experimental import pallas as pl
from jax.experimental.pallas import tpu as pltpu

F32 = jnp.float32
BF16 = jnp.bfloat16
I32 = jnp.int32
HIGHEST = lax.Precision.HIGHEST

D_MODEL = 1024
GDN_HEADS = 8
GDN_DK = 128
GDN_DV = 128
GDN_QK = GDN_HEADS * GDN_DK
QKV_W = 3 * GDN_QK
GDN_CONV = 4
GDN_CHUNK = 64
GDN_STEP_CHUNKS = 4
SC_CONV = 3
N_EXPERTS = 32
TOP_K = 4
SWIGLU_LIMIT = 7.0
SWIGLU_ALPHA = 1.702
NORM_EPS = 1e-6
N_GATE_COLS = 2 * GDN_HEADS
ROUTER_LANES = 128
EXPERT_BLOCK = 512
SUBLANES = 8
VMEM_LIMIT = 56 * 1024 * 1024

_NT = (((1,), (1,)), ((), ()))
_TN = (((0,), (0,)), ((), ()))


def _bdot(a, b):
    return jnp.dot(a.astype(BF16), b.astype(BF16), preferred_element_type=F32)


def _bdot_nt(a, b):
    return lax.dot_general(a.astype(BF16), b.astype(BF16), _NT, preferred_element_type=F32)


def _bdot_tn(a, b):
    return lax.dot_general(a.astype(BF16), b.astype(BF16), _TN, preferred_element_type=F32)


def _silu(x):
    return x * jax.nn.sigmoid(x)


def _softplus(x):
    return jnp.maximum(x, 0.0) + jnp.log1p(jnp.exp(-jnp.abs(x)))


def _seq_tile(n_seq, seq_len, target):
    if seq_len >= target:
        assert seq_len % target == 0
        return 1, target
    sb = min(n_seq, target // seq_len)
    assert n_seq % sb == 0 and seq_len % SUBLANES == 0
    return sb, seq_len


def _route_tile(n_seq, seq_len):
    return _seq_tile(n_seq, seq_len, 512 if seq_len >= 512 else 256)


def _params(sem):
    return pltpu.CompilerParams(dimension_semantics=sem, vmem_limit_bytes=VMEM_LIMIT)


def _ada_body(c_ref, w_ref, b_ref, o_ref):
    o_ref[...] = _bdot(_silu(c_ref[...]), w_ref[...]) + b_ref[...]


def _ada(c, w_ada, b_ada):
    rows, d = c.shape
    n = w_ada.shape[1]
    tn = 1024
    return pl.pallas_call(
        _ada_body,
        grid=(n // tn,),
        in_specs=[pl.BlockSpec((rows, d), lambda j: (0, 0)),
                  pl.BlockSpec((d, tn), lambda j: (0, j)),
                  pl.BlockSpec((1, tn), lambda j: (0, j))],
        out_specs=pl.BlockSpec((rows, tn), lambda j: (0, j)),
        out_shape=jax.ShapeDtypeStruct((rows, n), F32),
        compiler_params=_params(("arbitrary",)),
        name="ada",
    )(c, w_ada, b_ada.reshape(1, n))


def _gates(v, a_log, dt_bias, axis):
    is_beta = lax.broadcasted_iota(I32, v.shape, axis) < GDN_HEADS
    beta = jax.nn.sigmoid(v)
    g = -jnp.exp(a_log) * _softplus(v + dt_bias)
    return jnp.where(is_beta, beta, g)


def _ada_norm(x_ref, sc_ref, sh_ref, nw_ref):
    x = x_ref[...]
    y = x * lax.rsqrt(jnp.mean(x * x, axis=-1, keepdims=True) + NORM_EPS) * nw_ref[...]
    h = y * (1.0 + sc_ref[...]) + sh_ref[...]
    return h.reshape(x.shape[0] * x.shape[1], x.shape[2]).astype(BF16)


def _resident(shape):
    return pl.BlockSpec(shape, lambda *_: (0,) * len(shape), pipeline_mode=pl.Buffered(1))


def _shift_rows(x, hist, s, per_seq):
    rows, width = x.shape
    x3 = x.reshape(rows // SUBLANES, SUBLANES, width)
    xr = pltpu.roll(x3, s, 1)
    if per_seq:
        src = pltpu.roll(hist.reshape(x3.shape), s, 1)
    else:
        hr = pltpu.roll(hist.reshape(1, SUBLANES, width), s, 1)
        src = hr if rows == SUBLANES else jnp.concatenate([hr, xr[:-1]], axis=0)
    sub = lax.broadcasted_iota(I32, x3.shape, 1)
    return jnp.where(sub < s, src, xr).reshape(rows, width)


def _causal_conv(x, hist, w, per_seq):
    taps = w.shape[0]
    acc = x * w[taps - 1:taps, :]
    for s in range(1, taps):
        acc = acc + _shift_rows(x, hist, s, per_seq) * w[taps - 1 - s:taps - s, :]
    return acc


def _gdn_body(x_ref, sc_ref, sh_ref, n1_ref, wq_ref, wg_ref, wgt_ref, prow_ref, pcol_ref, conv0_ref, s0_ref,
              cw_ref, nw_ref, o_ref, s_ref, conv_ref, *rest, chunk, n_chunks):
    stacked = s_ref.shape[0] > 1
    hist_scr = rest[-1]

    @pl.when(pl.program_id(1) == 0)
    def _():
        s_ref[...] = s0_ref[...]
        if not stacked:
            hist_scr[...] = conv0_ref[0]

    h_in = _ada_norm(x_ref, sc_ref, sh_ref, n1_ref)
    n_own = QKV_W + D_MODEL
    proj = jnp.dot(h_in, wq_ref[:, :n_own], preferred_element_type=F32)
    x = proj[:, :QKV_W]
    z = proj[:, QKV_W:]
    if len(rest) == 2:
        rest[0][...] = jnp.dot(h_in, wq_ref[:, n_own:], preferred_element_type=F32)
    if stacked:
        hist = conv0_ref[...].reshape(x.shape)
        conv_ref[...] = x.reshape(conv_ref.shape)
    else:
        hist = hist_scr[...]
        hist_scr[...] = x[x.shape[0] - SUBLANES:, :]
        conv_ref[0] = x[x.shape[0] - SUBLANES:, :]
    qkvc = _silu(_causal_conv(x, hist, cw_ref[...], stacked))

    gcol = _gates(jnp.dot(h_in, wg_ref[...], preferred_element_type=F32), prow_ref[0:1, :], prow_ref[1:2, :], 1)
    ri = lax.broadcasted_iota(I32, (chunk, chunk), 0)
    ci = lax.broadcasted_iota(I32, (chunk, chunk), 1)
    causal = ri >= ci
    strict = ri > ci
    tri = causal.astype(F32)
    eye = (ri == ci).astype(F32)
    n_lvl = chunk.bit_length() - 1
    blk = [lax.shift_right_logical(ri, l) == lax.shift_right_logical(ci, l) for l in range(1, n_lvl + 1)]
    pair = [blk[l] & jnp.logical_not(blk[l - 1]) for l in range(1, n_lvl)]

    units = [(c, h) for c in range(n_chunks) for h in range(GDN_HEADS)]
    every = range(len(units))
    q, k, v, beta, dcol, dlast, decay, edec = [], [], [], [], [], [], [], []
    for c in range(n_chunks):
        rows = slice(c * chunk, (c + 1) * chunk)
        dec_col = jnp.dot(tri, gcol[rows, GDN_HEADS:], precision=HIGHEST, preferred_element_type=F32)
        grow = _gates(lax.dot_general(wgt_ref[...], h_in[rows, :], _NT, preferred_element_type=F32),
                      pcol_ref[:, 0:1], pcol_ref[:, 1:2], 0)
        dec_row = lax.dot_general(grow[GDN_HEADS:, :], tri, _NT, precision=HIGHEST, preferred_element_type=F32)
        for h in range(GDN_HEADS):
            qh = qkvc[rows, h * GDN_DK:(h + 1) * GDN_DK]
            kh = qkvc[rows, GDN_QK + h * GDN_DK:GDN_QK + (h + 1) * GDN_DK]
            q.append(qh * (lax.rsqrt(jnp.sum(qh * qh, axis=-1, keepdims=True) + 1e-6) * (GDN_DK ** -0.5)))
            k.append(kh * lax.rsqrt(jnp.sum(kh * kh, axis=-1, keepdims=True) + 1e-6))
            v.append(qkvc[rows, 2 * GDN_QK + h * GDN_DV:2 * GDN_QK + (h + 1) * GDN_DV])
            beta.append(gcol[rows, h:h + 1])
            dcol.append(dec_col[:, h:h + 1])
            dlast.append(dec_col[chunk - 1:chunk, h:h + 1])
            decay.append(jnp.where(causal, jnp.exp(dcol[-1] - dec_row[h:h + 1, :]), 0.0))
            edec.append(jnp.exp(dcol[-1]))
    kk = [_bdot_nt(k[i], k[i]) for i in every]
    qk = [_bdot_nt(q[i], k[i]) * decay[i] for i in every]
    a = [jnp.where(strict, beta[i] * kk[i] * decay[i], 0.0) for i in every]
    inv = [eye - jnp.where(blk[0], a[i], 0.0) for i in every]
    for lower_left in pair:
        right = [_bdot(jnp.where(lower_left, a[i], 0.0), inv[i]) for i in every]
        inv = [inv[i] - _bdot(inv[i], right[i]) for i in every]
    sol = [_bdot(inv[i], jnp.concatenate([beta[i] * v[i], (beta[i] * edec[i]) * k[i]], axis=1)) for i in every]
    seq_of = (lambda c: c) if stacked else (lambda c: 0)
    waves = [list(every)] if stacked else [[c * GDN_HEADS + h for h in range(GDN_HEADS)] for c in range(n_chunks)]
    state = {(seq_of(c), h): s_ref[seq_of(c), h] for c, h in units}
    for wave in waves:
        key = {i: (seq_of(units[i][0]), units[i][1]) for i in wave}
        ws = {i: _bdot(jnp.concatenate([sol[i][:, GDN_DV:], q[i] * edec[i]], axis=0), state[key[i]]) for i in wave}
        u = {i: sol[i][:, :GDN_DV] - ws[i][:chunk] for i in wave}
        o = {i: ws[i][chunk:] + _bdot(qk[i], u[i]) for i in wave}
        upd = {i: _bdot_tn(k[i] * jnp.exp(dlast[i] - dcol[i]), u[i]) for i in wave}
        for i in wave:
            c, h = units[i]
            state[key[i]] = state[key[i]] * jnp.exp(dlast[i]) + upd[i]
            rows = slice(c * chunk, (c + 1) * chunk)
            on = o[i] * lax.rsqrt(jnp.mean(o[i] * o[i], axis=-1, keepdims=True) + NORM_EPS) * nw_ref[...]
            o_ref[rows, h * GDN_DV:(h + 1) * GDN_DV] = on * _silu(z[rows, h * GDN_DV:(h + 1) * GDN_DV])
    for (b, h), value in state.items():
        s_ref[b, h] = value


def _gdn(x, sc, sh, norm1_w, w_proj, w_g, w_gt, p_row, p_col, conv0, state0, conv_w, norm_w):
    n_seq, seq_len, d = x.shape
    chunk = min(GDN_CHUNK, seq_len)
    assert seq_len % chunk == 0 and chunk % SUBLANES == 0
    n_tok = n_seq * seq_len
    stacked = seq_len == SUBLANES and n_seq % GDN_STEP_CHUNKS == 0
    if stacked:
        n_step_chunks, sb, nt = GDN_STEP_CHUNKS, GDN_STEP_CHUNKS, 1
    else:
        n_step_chunks = GDN_STEP_CHUNKS if (seq_len // chunk) % GDN_STEP_CHUNKS == 0 else 1
        sb, nt = 1, seq_len // (n_step_chunks * chunk)
    step_rows = n_step_chunks * chunk
    tb = step_rows // sb
    row = lambda s, t: s * nt + t
    ada = pl.BlockSpec((sb, 1, d), lambda s, t: (s, 0, 0))
    extra = w_proj.shape[1] - (QKV_W + d)
    extra_spec = [pl.BlockSpec((step_rows, extra), lambda s, t: (row(s, t), 0))] if extra else []
    extra_shape = [jax.ShapeDtypeStruct((n_tok, extra), F32)] if extra else []
    return pl.pallas_call(
        functools.partial(_gdn_body, chunk=chunk, n_chunks=n_step_chunks),
        grid=(n_seq // sb, nt),
        in_specs=[pl.BlockSpec((sb, tb, d), lambda s, t: (s, t, 0)),
                  ada, ada, _resident((1, d)),
                  _resident(w_proj.shape), _resident((d, N_GATE_COLS)), _resident((N_GATE_COLS, d)),
                  _resident((2, N_GATE_COLS)), _resident((N_GATE_COLS, 2)),
                  pl.BlockSpec((sb, SUBLANES, QKV_W), lambda s, t: (s, 0, 0)),
                  pl.BlockSpec((sb, GDN_HEADS, GDN_DK, GDN_DV), lambda s, t: (s, 0, 0, 0)),
                  _resident((GDN_CONV, QKV_W)), _resident((1, GDN_DV))],
        out_specs=[pl.BlockSpec((step_rows, d), lambda s, t: (row(s, t), 0)),
                   pl.BlockSpec((sb, GDN_HEADS, GDN_DK, GDN_DV), lambda s, t: (s, 0, 0, 0)),
                   pl.BlockSpec((sb, SUBLANES, QKV_W), lambda s, t: (s, 0, 0))] + extra_spec,
        out_shape=[jax.ShapeDtypeStruct((n_tok, d), F32),
                   jax.ShapeDtypeStruct((n_seq, GDN_HEADS, GDN_DK, GDN_DV), F32),
                   jax.ShapeDtypeStruct((n_seq, SUBLANES, QKV_W), F32)] + extra_shape,
        scratch_shapes=[pltpu.VMEM((SUBLANES, QKV_W), F32)],
        compiler_params=_params(("arbitrary", "arbitrary")),
        name="gdn",
    )(x, sc, sh, norm1_w, w_proj, w_g, w_gt, p_row, p_col, conv0, state0, conv_w, norm_w)


def _merge_body(x_ref, on_ref, *refs, per_seq, n_tiles, projected):
    n_head = 5 if projected else 4
    (sc0_ref, g1_ref, sc2_ref, sh2_ref, wa_ref, wb_ref, wo_ref, cw_ref, n2_ref, wr_ref, br_ref, cnt0_ref,
     x1_ref, h2_ref, pos_ref, rw_ref, cnt_ref, tail_ref, post_ref, tbl_ref, cnt_scr, hist_scr) = refs[n_head:]
    step = pl.program_id(0) * pl.num_programs(1) + pl.program_id(1)

    @pl.when(step == 0)
    def _():
        cnt_scr[...] = cnt0_ref[...]

    tm, d = on_ref.shape
    if projected:
        sc_b, sc_c, sc_h, gate_a, gate_b = (r[...] for r in refs[:n_head])
    else:
        sc1_ref, sh1_ref, n1_ref, w5_ref = refs[:n_head]
        p5 = jnp.dot(_ada_norm(x_ref, sc1_ref, sh1_ref, n1_ref), w5_ref[...], preferred_element_type=F32)
        sc_b, sc_c, sc_h, gate_a, gate_b = (p5[:, j * d:(j + 1) * d] for j in range(5))
    pre = sc_c * sc_h
    if per_seq:
        hist = sc0_ref[...].reshape(tm, d)
        tail_ref[...] = pre
    else:
        @pl.when(pl.program_id(1) == 0)
        def _():
            hist_scr[...] = sc0_ref[0]

        hist = hist_scr[...]
        hist_scr[...] = pre[tm - SUBLANES:, :]
        tail_ref[...] = pre[tm - SUBLANES:, :]
    y_b = _bdot(sc_b * _causal_conv(pre, hist, cw_ref[...], per_seq), wb_ref[...])
    y_a = _bdot(on_ref[...], wa_ref[...])
    merged = jax.nn.sigmoid(gate_a) * y_a + jax.nn.sigmoid(gate_b) * y_b
    mo = _bdot(merged, wo_ref[...]).reshape(x_ref.shape)
    x1 = x_ref[...] + g1_ref[...] * mo
    y = x1 * lax.rsqrt(jnp.mean(x1 * x1, axis=-1, keepdims=True) + NORM_EPS) * n2_ref[...]
    h2 = (y * (1.0 + sc2_ref[...]) + sh2_ref[...]).reshape(tm, d)
    x1_ref[...] = x1.reshape(tm, d)
    h2_ref[...] = h2

    h2_hi = h2.astype(BF16)
    h2_lo = (h2 - h2_hi.astype(F32)).astype(BF16)
    w_hi = wr_ref[0]
    logits = (jnp.dot(h2_hi, w_hi, preferred_element_type=F32)
              + (jnp.dot(h2_hi, wr_ref[1], preferred_element_type=F32)
                 + jnp.dot(h2_lo, w_hi, preferred_element_type=F32))) + br_ref[...]
    lane = lax.broadcasted_iota(I32, logits.shape, 1)
    lane_f = lane.astype(F32)
    work = logits
    vals, hots = [], []
    member = jnp.zeros(logits.shape, F32)
    for _ in range(TOP_K):
        m = jnp.max(work, axis=-1, keepdims=True)
        sel = jnp.min(jnp.where(work == m, lane_f, float(N_EXPERTS - 1)), axis=-1, keepdims=True)
        hot = lane_f == sel
        vals.append(m)
        hots.append(hot)
        member = member + hot.astype(F32)
        work = jnp.where(hot, -jnp.inf, work)
    exps = [jnp.exp(v - vals[0]) for v in vals]
    denom = exps[0] + exps[1] + exps[2] + exps[3]
    ti = lax.broadcasted_iota(I32, (tm, tm), 0)
    tj = lax.broadcasted_iota(I32, (tm, tm), 1)
    before = (tj < ti).astype(BF16)
    rank_loc = jnp.dot(before, member.astype(BF16), preferred_element_type=F32)
    cnt = jnp.sum(member, axis=0, keepdims=True).astype(I32)
    seg = lax.shift_left(lax.shift_right_logical(cnt + (SUBLANES - 1), 3), 3)
    seg8 = jnp.broadcast_to(seg.astype(F32), (SUBLANES, ROUTER_LANES))
    ei = lax.broadcasted_iota(I32, (ROUTER_LANES, ROUTER_LANES), 0)
    ej = lax.broadcasted_iota(I32, (ROUTER_LANES, ROUTER_LANES), 1)
    base8 = jnp.dot(seg8, (ei < ej).astype(F32), precision=HIGHEST, preferred_element_type=F32)
    row_all = base8[0:1, :] + rank_loc
    pos = jnp.zeros((tm, ROUTER_LANES), F32)
    rw = jnp.zeros((tm, ROUTER_LANES), F32)
    for kk in range(TOP_K):
        pos_k = jnp.sum(jnp.where(hots[kk], row_all, 0.0), axis=-1, keepdims=True)
        pos = jnp.where(lane == kk, pos_k, pos)
        rw = jnp.where(lane == kk, exps[kk] / denom, rw)
    pos_ref[...] = pos
    rw_ref[...] = rw
    pick = (lax.broadcasted_iota(I32, (SUBLANES, ROUTER_LANES), 0)
            == lax.broadcasted_iota(I32, (SUBLANES, ROUTER_LANES), 1)).astype(F32)
    post_ref[...] = lax.dot_general(pick, pos, _NT, precision=HIGHEST, preferred_element_type=F32)
    sub = lax.broadcasted_iota(I32, (SUBLANES, ROUTER_LANES), 0)
    tbl_ref[...] = jnp.where(sub == 0, base8.astype(I32),
                             jnp.where(sub == 1, seg8.astype(I32), cnt_scr[...].astype(I32)))
    cnt_scr[...] = cnt_scr[...] + seg8

    @pl.when(step == n_tiles - 1)
    def _():
        cnt_ref[...] = cnt_scr[...]


def _merge(x, on, p5, sc1, sh1, norm1_w, w5, sc0, g1, sc2, sh2, w_a, w_b, w_o, conv_w, norm2_w, w_r, b_r, cnt0):
    n_seq, seq_len, d = x.shape
    sb, tb = _route_tile(n_seq, seq_len)
    per_seq = sb > 1 or tb == SUBLANES
    if per_seq:
        assert tb == SUBLANES
    tm = sb * tb
    nt = seq_len // tb
    ns = n_seq // sb
    n_tok = n_seq * seq_len
    row = lambda s, t: s * nt + t
    full = lambda shape: pl.BlockSpec(shape, lambda s, t: (0,) * len(shape))
    ada = pl.BlockSpec((sb, 1, d), lambda s, t: (s, 0, 0))
    tok = pl.BlockSpec((tm, d), lambda s, t: (row(s, t), 0))
    lanes = pl.BlockSpec((tm, ROUTER_LANES), lambda s, t: (row(s, t), 0))
    tail_rows = tm if per_seq else SUBLANES
    if p5 is None:
        head_specs = [ada, ada, _resident((1, d)), _resident((d, 5 * d))]
        head_args = [sc1, sh1, norm1_w, w5]
    else:
        head_specs = [pl.BlockSpec((tm, d), lambda s, t, j=j: (row(s, t), j)) for j in range(5)]
        head_args = [p5] * 5
    return pl.pallas_call(
        functools.partial(_merge_body, per_seq=per_seq, n_tiles=ns * nt, projected=p5 is not None),
        grid=(ns, nt),
        in_specs=[pl.BlockSpec((sb, tb, d), lambda s, t: (s, t, 0)),
                  tok] + head_specs + [
                  pl.BlockSpec((sb, SUBLANES, d), lambda s, t: (s, 0, 0)),
                  ada, ada, ada,
                  _resident((d, d)), _resident((d, d)), _resident((d, d)),
                  _resident((SC_CONV, d)), _resident((1, d)), _resident((2, d, ROUTER_LANES)),
                  _resident((1, ROUTER_LANES)), _resident((SUBLANES, ROUTER_LANES))],
        out_specs=[tok, tok, lanes, lanes, full((SUBLANES, ROUTER_LANES)),
                   pl.BlockSpec((tail_rows, d), lambda s, t: (row(s, t), 0)),
                   pl.BlockSpec((SUBLANES, tm), lambda s, t: (0, row(s, t))),
                   pl.BlockSpec((SUBLANES, ROUTER_LANES), lambda s, t: (row(s, t), 0))],
        out_shape=[jax.ShapeDtypeStruct((n_tok, d), F32),
                   jax.ShapeDtypeStruct((n_tok, d), F32),
                   jax.ShapeDtypeStruct((n_tok, ROUTER_LANES), F32),
                   jax.ShapeDtypeStruct((n_tok, ROUTER_LANES), F32),
                   jax.ShapeDtypeStruct((SUBLANES, ROUTER_LANES), F32),
                   jax.ShapeDtypeStruct((ns * nt * tail_rows, d), F32),
                   jax.ShapeDtypeStruct((SUBLANES, n_tok), F32),
                   jax.ShapeDtypeStruct((ns * nt * SUBLANES, ROUTER_LANES), I32)],
        scratch_shapes=[pltpu.VMEM((SUBLANES, ROUTER_LANES), F32), pltpu.VMEM((SUBLANES, d), F32)],
        compiler_params=_params(("arbitrary", "arbitrary")),
        name="merge",
    )(x, on, *head_args, sc0, g1, sc2, sh2, w_a, w_b, w_o, conv_w, norm2_w, w_r, b_r, cnt0)


def _pow2_pieces(length, max_rows):
    out = []
    rows = max_rows
    while rows >= SUBLANES:
        shift = rows.bit_length()
        offset = lax.shift_left(lax.shift_right_logical(length, shift), shift)
        out.append(((length & rows) != 0, offset, rows))
        rows //= 2
    return out


def _segment_copies(tbl_ref, tile_id, local_ref, global_ref, sem, to_global, max_rows):
    out = []
    base = tile_id * (3 * N_EXPERTS)
    for e in range(N_EXPERTS):
        local0 = tbl_ref[base + e]
        length = tbl_ref[base + N_EXPERTS + e]
        global0 = tbl_ref[base + 2 * N_EXPERTS + e]
        for pred, offset, rows in _pow2_pieces(length, max_rows):
            loc = local_ref.at[pl.ds(pl.multiple_of(local0 + offset, SUBLANES), rows)]
            glo = global_ref.at[pl.ds(pl.multiple_of(global0 + offset, SUBLANES), rows)]
            cp = pltpu.make_async_copy(loc, glo, sem) if to_global else pltpu.make_async_copy(glo, loc, sem)
            out.append((pred, cp))
    return out


def _start_all(copies):
    for pred, cp in copies:
        pl.when(pred)(cp.start)


def _wait_all(copies):
    for pred, cp in copies:
        pl.when(pred)(cp.wait)


def _dispatch_body(tbl_ref, tails_ref, post_ref, h2_ref, *refs, tile0, zero_tails):
    buf_ref, sorted_scr, zero_scr, sems = refs[-4:]
    i = pl.program_id(0)
    slot = lax.rem(i, 2)
    tm = h2_ref.shape[0]
    r_rows = sorted_scr.shape[1]

    if zero_tails:
        @pl.when(i == 0)
        def _():
            zero_scr[...] = jnp.zeros(zero_scr.shape, F32)
            z_rows = zero_scr.shape[0]

            def fill(b, carry):
                cp = pltpu.make_async_copy(zero_scr, buf_ref.at[pl.ds(pl.multiple_of(b * SUBLANES, SUBLANES), z_rows)],
                                           sems.at[2])
                cp.start()
                cp.wait()
                return carry

            tails = []
            for e in range(N_EXPERTS):
                start, length = tails_ref[e], tails_ref[N_EXPERTS + e]
                whole = length // z_rows
                lax.fori_loop(0, whole, lambda b, c, s=start: fill(s // SUBLANES + b * (z_rows // SUBLANES), c), 0)
                rest0 = start + whole * z_rows
                for pred, offset, rows in _pow2_pieces(length - whole * z_rows, z_rows // 2):
                    at = pl.multiple_of(rest0 + offset, SUBLANES)
                    tails.append((pred, pltpu.make_async_copy(zero_scr.at[pl.ds(0, rows)],
                                                              buf_ref.at[pl.ds(at, rows)], sems.at[2])))
            _start_all(tails)
            _wait_all(tails)
            lax.fori_loop(tails_ref[2 * N_EXPERTS] // z_rows, buf_ref.shape[0] // z_rows,
                          lambda b, c: fill(b * (z_rows // SUBLANES), c), 0)

    pos = post_ref[...].astype(I32)
    j = lax.broadcasted_iota(I32, (r_rows, tm), 0)
    onehot = jnp.zeros((r_rows, tm), F32)
    for k in range(TOP_K):
        onehot = jnp.where(j == pos[k:k + 1, :], 1.0, onehot)
    sorted_scr[slot] = _bdot(onehot, h2_ref[...])

    _start_all(_segment_copies(tbl_ref, tile0 + i, sorted_scr.at[slot], buf_ref, sems.at[slot], True, tm))

    @pl.when(i > 0)
    def _():
        _wait_all(_segment_copies(tbl_ref, tile0 + i - 1, sorted_scr.at[1 - slot], buf_ref, sems.at[1 - slot],
                                  True, tm))

    @pl.when(i == pl.num_programs(0) - 1)
    def _():
        _wait_all(_segment_copies(tbl_ref, tile0 + i, sorted_scr.at[slot], buf_ref, sems.at[slot], True, tm))


def _dispatch(tbl, tails, post, h2, buf, buf_rows, tm, tile0):
    n_tok, d = h2.shape
    r_rows = TOP_K * tm + N_EXPERTS * SUBLANES
    first = buf is None
    grid_spec = pltpu.PrefetchScalarGridSpec(
        num_scalar_prefetch=2,
        grid=(n_tok // tm,),
        in_specs=[pl.BlockSpec((SUBLANES, tm), lambda i, *_: (0, i)),
                  pl.BlockSpec((tm, d), lambda i, *_: (i, 0))]
                 + ([] if first else [pl.BlockSpec(memory_space=pl.ANY)]),
        out_specs=pl.BlockSpec(memory_space=pl.ANY),
        scratch_shapes=[pltpu.VMEM((2, r_rows, d), F32),
                        pltpu.VMEM((EXPERT_BLOCK // 2, d), F32),
                        pltpu.SemaphoreType.DMA((3,))],
    )
    return pl.pallas_call(
        functools.partial(_dispatch_body, tile0=tile0, zero_tails=first),
        grid_spec=grid_spec,
        out_shape=jax.ShapeDtypeStruct((buf_rows, d), F32),
        input_output_aliases={} if first else {4: 0},
        compiler_params=pltpu.CompilerParams(dimension_semantics=("arbitrary",), has_side_effects=True,
                                             vmem_limit_bytes=VMEM_LIMIT),
        name="dispatch",
    )(tbl, tails, post, h2, *([] if first else [buf]))


def _expert_body(be_ref, nv_ref, x_ref, wg_ref, bg_ref, wu_ref, bu_ref, wd_ref, bd_ref, o_ref,
                 wg_s, wu_s, wd_s):
    i = pl.program_id(0)

    @pl.when(i < nv_ref[0])
    def _():
        @pl.when((i == 0) | (be_ref[i] != be_ref[jnp.maximum(i - 1, 0)]))
        def _():
            wg_s[...] = wg_ref[0].astype(BF16)
            wu_s[...] = wu_ref[0].astype(BF16)
            wd_s[...] = wd_ref[0].astype(BF16)

        x = x_ref[...].astype(BF16)
        gate = jnp.dot(x, wg_s[...], preferred_element_type=F32) + bg_ref[0]
        up = jnp.dot(x, wu_s[...], preferred_element_type=F32) + bu_ref[0]
        gate = jnp.minimum(gate, SWIGLU_LIMIT)
        up = jnp.clip(up, -SWIGLU_LIMIT, SWIGLU_LIMIT)
        glu = gate * jax.nn.sigmoid(SWIGLU_ALPHA * gate)
        o_ref[...] = _bdot((up + 1.0) * glu, wd_s[...]) + bd_ref[0]

    @pl.when(i >= nv_ref[0])
    def _():
        o_ref[...] = jnp.zeros(o_ref.shape, F32)


def _experts(block_e, n_valid, xs, w_gate, b_gate, w_up, b_up, w_down, b_down):
    m_pad, d = xs.shape
    n_blocks = m_pad // EXPERT_BLOCK
    f = w_gate.shape[2]
    blk = lambda i, be, nv: (jnp.maximum(jnp.minimum(i, nv[0] - 1), 0), 0)
    wspec = lambda a, b: pl.BlockSpec((1, a, b), lambda i, be, nv: (be[i], 0, 0))
    grid_spec = pltpu.PrefetchScalarGridSpec(
        num_scalar_prefetch=2,
        grid=(n_blocks,),
        in_specs=[pl.BlockSpec((EXPERT_BLOCK, d), blk),
                  wspec(d, f), wspec(1, f), wspec(d, f), wspec(1, f), wspec(f, d), wspec(1, d)],
        out_specs=pl.BlockSpec((EXPERT_BLOCK, d), lambda i, be, nv: (i, 0)),
        scratch_shapes=[pltpu.VMEM((d, f), BF16), pltpu.VMEM((d, f), BF16), pltpu.VMEM((f, d), BF16)],
    )
    return pl.pallas_call(
        _expert_body,
        grid_spec=grid_spec,
        out_shape=jax.ShapeDtypeStruct((m_pad, d), F32),
        compiler_params=_params(("arbitrary",)),
        name="experts",
    )(block_e, n_valid, xs, w_gate, b_gate[:, None, :], w_up, b_up[:, None, :], w_down, b_down[:, None, :])


def _combine_body(tbl_ref, ys_ref, pos_ref, rw_ref, x1_ref, g2_ref, fw_ref, o_ref, blk_scr, sems, *, tile0):
    step = pl.program_id(0) * pl.num_programs(1) + pl.program_id(1)
    n_steps = pl.num_programs(0) * pl.num_programs(1)
    slot = lax.rem(step, 2)
    tm = pos_ref.shape[0]
    r_rows = blk_scr.shape[1]

    def fetch(tile, into):
        return _segment_copies(tbl_ref, tile0 + tile, blk_scr.at[into], ys_ref, sems.at[into], False, tm)

    @pl.when(step == 0)
    def _():
        blk_scr[...] = jnp.zeros(blk_scr.shape, F32)
        _start_all(fetch(0, 0))

    @pl.when(step + 1 < n_steps)
    def _():
        _start_all(fetch(step + 1, 1 - slot))

    _wait_all(fetch(step, slot))

    pos = pos_ref[...].astype(I32)
    rw = rw_ref[...]
    j = lax.broadcasted_iota(I32, (tm, r_rows), 1)
    pw = jnp.zeros((tm, r_rows), F32)
    for k in range(TOP_K):
        pw = jnp.where(j == pos[:, k:k + 1], rw[:, k:k + 1], pw)
    ffn = _bdot(pw, blk_scr[slot])
    y = x1_ref[...] + g2_ref[...] * ffn.reshape(x1_ref.shape)
    o_ref[...] = y * lax.rsqrt(jnp.mean(y * y, axis=-1, keepdims=True) + NORM_EPS) * fw_ref[...]


def _combine(tbl, ys, pos, rw, x1, g2, final_w, sb, tb, tile0):
    n_seq, seq_len, d = x1.shape
    tm = sb * tb
    nt = seq_len // tb
    r_rows = TOP_K * tm + N_EXPERTS * SUBLANES
    row = lambda s, t: s * nt + t
    grid_spec = pltpu.PrefetchScalarGridSpec(
        num_scalar_prefetch=1,
        grid=(n_seq // sb, nt),
        in_specs=[pl.BlockSpec(memory_space=pl.ANY),
                  pl.BlockSpec((tm, ROUTER_LANES), lambda s, t, *_: (row(s, t), 0)),
                  pl.BlockSpec((tm, ROUTER_LANES), lambda s, t, *_: (row(s, t), 0)),
                  pl.BlockSpec((sb, tb, d), lambda s, t, *_: (s, t, 0)),
                  pl.BlockSpec((sb, 1, d), lambda s, t, *_: (s, 0, 0)),
                  pl.BlockSpec((1, d), lambda s, t, *_: (0, 0))],
        out_specs=pl.BlockSpec((sb, tb, d), lambda s, t, *_: (s, t, 0)),
        scratch_shapes=[pltpu.VMEM((2, r_rows, d), F32), pltpu.SemaphoreType.DMA((2,))],
    )
    return pl.pallas_call(
        functools.partial(_combine_body, tile0=tile0),
        grid_spec=grid_spec,
        out_shape=jax.ShapeDtypeStruct((n_seq, seq_len, d), F32),
        compiler_params=_params(("arbitrary", "arbitrary")),
        name="combine",
    )(tbl, ys, pos, rw, x1, g2, final_w)


def _pad_state(state, rows):
    return jnp.pad(state, ((0, 0), (rows - state.shape[1], 0), (0, 0)))


def kernel(x_prompt, x_sample, c_prompt, c_sample, state_gdn_conv, state_gdn_rec, state_sc_conv, w_ada, b_ada,
           norm1_w, w_in, gdn_conv_w, gdn_a_log, gdn_dt_bias, gdn_norm_w, w_branch_a, sc_conv_w, w_branch_b,
           w_out, norm2_w, w_router, b_router, w_gate, b_gate, w_up, b_up, w_down, b_down, final_norm_w):
    assert w_ada.shape[0] == 1, "single-layer trunk"
    d = D_MODEL
    bp, tp, _ = x_prompt.shape
    bs, ts, _ = x_sample.shape
    n_p, n_s = bp * tp, bs * ts
    n_tok = n_p + n_s

    w_in0 = w_in[0]
    g_lo, g_hi = QKV_W + d, QKV_W + d + N_GATE_COLS
    w_qkvz = w_in0[:, :g_lo].astype(BF16)
    w5 = w_in0[:, g_hi:].astype(BF16)
    w_all = jnp.concatenate([w_qkvz, w5], axis=1)
    w_g = w_in0[:, g_lo:g_hi].astype(BF16)
    w_gt = w_g.T
    zeros_h = jnp.zeros((GDN_HEADS,), F32)
    p_row = jnp.stack([jnp.concatenate([zeros_h, gdn_a_log[0]]), jnp.concatenate([zeros_h, gdn_dt_bias[0]])])
    p_col = p_row.T
    w_a = w_branch_a[0].astype(BF16)
    w_b = w_branch_b[0].astype(BF16)
    w_o = w_out[0].astype(BF16)
    w_r = jnp.pad(w_router[0], ((0, 0), (0, ROUTER_LANES - N_EXPERTS)))
    w_r_hi = w_r.astype(BF16)
    w_r = jnp.stack([w_r_hi, (w_r - w_r_hi.astype(F32)).astype(BF16)])
    b_r = jnp.pad(b_router[0], (0, ROUTER_LANES - N_EXPERTS), constant_values=-jnp.inf).reshape(1, ROUTER_LANES)

    n_c = bp + bs
    c_rows = -(-n_c // 16) * 16
    c_all = jnp.pad(jnp.concatenate([c_prompt, c_sample], axis=0), ((0, c_rows - n_c), (0, 0)))
    ada = _ada(c_all, w_ada[0], b_ada[0])

    def ada_parts(lo, hi):
        return [ada[lo:hi, j * d:(j + 1) * d].reshape(hi - lo, 1, d) for j in range(6)]

    groups = [
        dict(x=x_prompt, ada=ada_parts(0, bp), n_seq=bp, seq_len=tp,
             conv0=jnp.zeros((bp, SUBLANES, QKV_W), F32),
             rec0=jnp.zeros((bp, GDN_HEADS, GDN_DK, GDN_DV), F32),
             sc0=jnp.zeros((bp, SUBLANES, d), F32)),
        dict(x=x_sample, ada=ada_parts(bp, n_c), n_seq=bs, seq_len=ts,
             conv0=_pad_state(state_gdn_conv[0], SUBLANES),
             rec0=state_gdn_rec[0],
             sc0=_pad_state(state_sc_conv[0], SUBLANES)),
    ]

    counts = jnp.zeros((SUBLANES, ROUTER_LANES), F32)
    for g in groups:
        sh1, sc1, g1, sh2, sc2, g2 = g["ada"]
        n_seq, seq_len = g["n_seq"], g["seq_len"]
        wide = seq_len > SUBLANES
        on, rec, conv_tail, *p5 = _gdn(g["x"], sc1, sh1, norm1_w, w_all if wide else w_qkvz, w_g, w_gt, p_row, p_col,
                                       g["conv0"], g["rec0"], gdn_conv_w[0], gdn_norm_w)
        x1, h2, pos, rw, counts, tail, post, tbl = _merge(g["x"], on, p5[0] if wide else None, sc1, sh1, norm1_w, w5,
                                                          g["sc0"], g1, sc2, sh2, w_a, w_b, w_o, sc_conv_w[0],
                                                          norm2_w, w_r, b_r, counts)
        sb, tb = _route_tile(n_seq, seq_len)
        g.update(x1=x1, h2=h2, pos=pos, rw=rw, post=post, tbl=tbl, rec=rec, g2=g2, sb=sb, tb=tb, tm=sb * tb,
                 rows_after=counts[0, :N_EXPERTS].astype(I32),
                 new_conv=conv_tail[:, SUBLANES - (GDN_CONV - 1):, :],
                 new_sc=tail.reshape(n_seq, -1, d)[:, -(SC_CONV - 1):, :])

    rows_e = counts[0, :N_EXPERTS].astype(I32)
    padded = (rows_e + EXPERT_BLOCK - 1) // EXPERT_BLOCK * EXPERT_BLOCK
    pad_end = jnp.cumsum(padded)
    expert0 = (pad_end - padded).astype(I32)
    tbl_all = jnp.concatenate([g["tbl"] for g in groups], axis=0).reshape(-1, SUBLANES, ROUTER_LANES)
    n_tiles = tbl_all.shape[0]
    tbl_all = tbl_all[:, :3, :N_EXPERTS].at[:, 2, :].add(expert0[None, :]).reshape(-1)
    rows_first = groups[0]["rows_after"]
    tails = jnp.concatenate([expert0 + rows_first, padded - rows_first, pad_end[-1:]]).astype(I32)
    max_rows = n_tok * TOP_K + n_tiles * N_EXPERTS * (SUBLANES - 1)
    n_blocks = -(-max_rows // EXPERT_BLOCK) + N_EXPERTS
    block_start = jnp.arange(n_blocks, dtype=I32) * EXPERT_BLOCK
    block_e = jnp.minimum(jnp.sum((pad_end[None, :] <= block_start[:, None]).astype(I32), axis=1), N_EXPERTS - 1)
    n_valid = (pad_end[-1:] // EXPERT_BLOCK).astype(I32)

    buf = None
    tile0 = 0
    for g in groups:
        g["tile0"] = tile0
        buf = _dispatch(tbl_all, tails, g["post"], g["h2"], buf, n_blocks * EXPERT_BLOCK, g["tm"], tile0)
        tile0 += g["h2"].shape[0] // g["tm"]
    ys = _experts(block_e, n_valid, buf, w_gate[0], b_gate[0], w_up[0], b_up[0], w_down[0], b_down[0])
    outs = []
    for g in groups:
        n_seq, seq_len = g["n_seq"], g["seq_len"]
        x1 = g["x1"].reshape(n_seq, seq_len, d)
        outs.append(_combine(tbl_all, ys, g["pos"], g["rw"], x1, g["g2"], final_norm_w.reshape(1, d),
                             g["sb"], g["tb"], g["tile0"]))

    gp, gs = groups
    return (outs[0], outs[1], gp["new_conv"][None], gp["rec"][None], gp["new_sc"][None],
            gs["new_conv"][None], gs["rec"][None], gs["new_sc"][None])
```

```python
import functools

import jax
import jax.numpy as jnp
from jax import lax
from jax.experimental import pallas as pl
from jax.experimental.pallas import tpu as pltpu

F32 = jnp.float32
BF16 = jnp.bfloat16
I32 = jnp.int32
HIGHEST = lax.Precision.HIGHEST

D_MODEL = 1024
GDN_HEADS = 8
GDN_DK = 128
GDN_DV = 128
GDN_QK = GDN_HEADS * GDN_DK
QKV_W = 3 * GDN_QK
GDN_CONV = 4
GDN_CHUNK = 64
GDN_STEP_CHUNKS = 4
SC_CONV = 3
N_EXPERTS = 32
TOP_K = 4
SWIGLU_LIMIT = 7.0
SWIGLU_ALPHA = 1.702
NORM_EPS = 1e-6
N_GATE_COLS = 2 * GDN_HEADS
ROUTER_LANES = 128
EXPERT_BLOCK = 256
SUBLANES = 8
VMEM_LIMIT = 56 * 1024 * 1024

_NT = (((1,), (1,)), ((), ()))
_TN = (((0,), (0,)), ((), ()))


def _bdot(a, b):
    return jnp.dot(a.astype(BF16), b.astype(BF16), preferred_element_type=F32)


def _bdot_nt(a, b):
    return lax.dot_general(a.astype(BF16), b.astype(BF16), _NT, preferred_element_type=F32)


def _bdot_tn(a, b):
    return lax.dot_general(a.astype(BF16), b.astype(BF16), _TN, preferred_element_type=F32)


def _silu(x):
    return x * jax.nn.sigmoid(x)


def _softplus(x):
    return jnp.maximum(x, 0.0) + jnp.log1p(jnp.exp(-jnp.abs(x)))


def _seq_tile(n_seq, seq_len, target):
    if seq_len >= target:
        assert seq_len % target == 0
        return 1, target
    sb = min(n_seq, target // seq_len)
    assert n_seq % sb == 0 and seq_len % SUBLANES == 0
    return sb, seq_len


def _route_tile(n_seq, seq_len):
    return _seq_tile(n_seq, seq_len, 512 if seq_len >= 512 else 256)


def _params(sem):
    return pltpu.CompilerParams(dimension_semantics=sem, vmem_limit_bytes=VMEM_LIMIT)


def _ada_body(c_ref, w_ref, b_ref, o_ref):
    o_ref[...] = _bdot(_silu(c_ref[...]), w_ref[...]) + b_ref[...]


def _ada(c, w_ada, b_ada):
    rows, d = c.shape
    n = w_ada.shape[1]
    tn = 1024
    return pl.pallas_call(
        _ada_body,
        grid=(n // tn,),
        in_specs=[pl.BlockSpec((rows, d), lambda j: (0, 0)),
                  pl.BlockSpec((d, tn), lambda j: (0, j)),
                  pl.BlockSpec((1, tn), lambda j: (0, j))],
        out_specs=pl.BlockSpec((rows, tn), lambda j: (0, j)),
        out_shape=jax.ShapeDtypeStruct((rows, n), F32),
        compiler_params=_params(("arbitrary",)),
        name="ada",
    )(c, w_ada, b_ada.reshape(1, n))


def _gates(v, a_log, dt_bias, axis):
    is_beta = lax.broadcasted_iota(I32, v.shape, axis) < GDN_HEADS
    beta = jax.nn.sigmoid(v)
    g = -jnp.exp(a_log) * _softplus(v + dt_bias)
    return jnp.where(is_beta, beta, g)


def _ada_norm(x_ref, sc_ref, sh_ref, nw_ref):
    x = x_ref[...]
    y = x * lax.rsqrt(jnp.mean(x * x, axis=-1, keepdims=True) + NORM_EPS) * nw_ref[...]
    h = y * (1.0 + sc_ref[...]) + sh_ref[...]
    return h.reshape(x.shape[0] * x.shape[1], x.shape[2]).astype(BF16)


def _resident(shape):
    return pl.BlockSpec(shape, lambda *_: (0,) * len(shape), pipeline_mode=pl.Buffered(1))


def _shift_rows(x, hist, s, per_seq):
    rows, width = x.shape
    x3 = x.reshape(rows // SUBLANES, SUBLANES, width)
    xr = pltpu.roll(x3, s, 1)
    if per_seq:
        src = pltpu.roll(hist.reshape(x3.shape), s, 1)
    else:
        hr = pltpu.roll(hist.reshape(1, SUBLANES, width), s, 1)
        src = hr if rows == SUBLANES else jnp.concatenate([hr, xr[:-1]], axis=0)
    sub = lax.broadcasted_iota(I32, x3.shape, 1)
    return jnp.where(sub < s, src, xr).reshape(rows, width)


def _causal_conv(x, hist, w, per_seq):
    taps = w.shape[0]
    acc = x * w[taps - 1:taps, :]
    for s in range(1, taps):
        acc = acc + _shift_rows(x, hist, s, per_seq) * w[taps - 1 - s:taps - s, :]
    return acc


def _gdn_body(x_ref, sc_ref, sh_ref, n1_ref, wq_ref, wg_ref, wgt_ref, prow_ref, pcol_ref, conv0_ref, s0_ref,
              cw_ref, nw_ref, o_ref, s_ref, conv_ref, *rest, chunk, n_chunks):
    stacked = s_ref.shape[0] > 1
    hist_scr = rest[-1]

    @pl.when(pl.program_id(1) == 0)
    def _():
        s_ref[...] = s0_ref[...]
        if not stacked:
            hist_scr[...] = conv0_ref[0]

    h_in = _ada_norm(x_ref, sc_ref, sh_ref, n1_ref)
    n_own = QKV_W + D_MODEL
    proj = jnp.dot(h_in, wq_ref[:, :n_own], preferred_element_type=F32)
    x = proj[:, :QKV_W]
    z = proj[:, QKV_W:]
    if len(rest) == 2:
        rest[0][...] = jnp.dot(h_in, wq_ref[:, n_own:], preferred_element_type=F32)
    if stacked:
        hist = conv0_ref[...].reshape(x.shape)
        conv_ref[...] = x.reshape(conv_ref.shape)
    else:
        hist = hist_scr[...]
        hist_scr[...] = x[x.shape[0] - SUBLANES:, :]
        conv_ref[0] = x[x.shape[0] - SUBLANES:, :]
    qkvc = _silu(_causal_conv(x, hist, cw_ref[...], stacked))

    gcol = _gates(jnp.dot(h_in, wg_ref[...], preferred_element_type=F32), prow_ref[0:1, :], prow_ref[1:2, :], 1)
    ri = lax.broadcasted_iota(I32, (chunk, chunk), 0)
    ci = lax.broadcasted_iota(I32, (chunk, chunk), 1)
    causal = ri >= ci
    strict = ri > ci
    tri = causal.astype(F32)
    eye = (ri == ci).astype(F32)
    n_lvl = chunk.bit_length() - 1
    blk = [lax.shift_right_logical(ri, l) == lax.shift_right_logical(ci, l) for l in range(1, n_lvl + 1)]
    pair = [blk[l] & jnp.logical_not(blk[l - 1]) for l in range(1, n_lvl)]

    units = [(c, h) for c in range(n_chunks) for h in range(GDN_HEADS)]
    every = range(len(units))
    q, k, v, beta, dcol, dlast, decay, edec = [], [], [], [], [], [], [], []
    for c in range(n_chunks):
        rows = slice(c * chunk, (c + 1) * chunk)
        dec_col = jnp.dot(tri, gcol[rows, GDN_HEADS:], precision=HIGHEST, preferred_element_type=F32)
        grow = _gates(lax.dot_general(wgt_ref[...], h_in[rows, :], _NT, preferred_element_type=F32),
                      pcol_ref[:, 0:1], pcol_ref[:, 1:2], 0)
        dec_row = lax.dot_general(grow[GDN_HEADS:, :], tri, _NT, precision=HIGHEST, preferred_element_type=F32)
        for h in range(GDN_HEADS):
            qh = qkvc[rows, h * GDN_DK:(h + 1) * GDN_DK]
            kh = qkvc[rows, GDN_QK + h * GDN_DK:GDN_QK + (h + 1) * GDN_DK]
            q.append(qh * (lax.rsqrt(jnp.sum(qh * qh, axis=-1, keepdims=True) + 1e-6) * (GDN_DK ** -0.5)))
            k.append(kh * lax.rsqrt(jnp.sum(kh * kh, axis=-1, keepdims=True) + 1e-6))
            v.append(qkvc[rows, 2 * GDN_QK + h * GDN_DV:2 * GDN_QK + (h + 1) * GDN_DV])
            beta.append(gcol[rows, h:h + 1])
            dcol.append(dec_col[:, h:h + 1])
            dlast.append(dec_col[chunk - 1:chunk, h:h + 1])
            decay.append(jnp.where(causal, jnp.exp(dcol[-1] - dec_row[h:h + 1, :]), 0.0))
            edec.append(jnp.exp(dcol[-1]))
    kk = [_bdot_nt(k[i], k[i]) for i in every]
    qk = [_bdot_nt(q[i], k[i]) * decay[i] for i in every]
    a = [jnp.where(strict, beta[i] * kk[i] * decay[i], 0.0) for i in every]
    inv = [eye - jnp.where(blk[0], a[i], 0.0) for i in every]
    for lower_left in pair:
        right = [_bdot(jnp.where(lower_left, a[i], 0.0), inv[i]) for i in every]
        inv = [inv[i] - _bdot(inv[i], right[i]) for i in every]
    sol = [_bdot(inv[i], jnp.concatenate([beta[i] * v[i], (beta[i] * edec[i]) * k[i]], axis=1)) for i in every]
    seq_of = (lambda c: c) if stacked else (lambda c: 0)
    waves = [list(every)] if stacked else [[c * GDN_HEADS + h for h in range(GDN_HEADS)] for c in range(n_chunks)]
    state = {(seq_of(c), h): s_ref[seq_of(c), h] for c, h in units}
    for wave in waves:
        key = {i: (seq_of(units[i][0]), units[i][1]) for i in wave}
        ws = {i: _bdot(jnp.concatenate([sol[i][:, GDN_DV:], q[i] * edec[i]], axis=0), state[key[i]]) for i in wave}
        u = {i: sol[i][:, :GDN_DV] - ws[i][:chunk] for i in wave}
        o = {i: ws[i][chunk:] + _bdot(qk[i], u[i]) for i in wave}
        upd = {i: _bdot_tn(k[i] * jnp.exp(dlast[i] - dcol[i]), u[i]) for i in wave}
        for i in wave:
            c, h = units[i]
            state[key[i]] = state[key[i]] * jnp.exp(dlast[i]) + upd[i]
            rows = slice(c * chunk, (c + 1) * chunk)
            on = o[i] * lax.rsqrt(jnp.mean(o[i] * o[i], axis=-1, keepdims=True) + NORM_EPS) * nw_ref[...]
            o_ref[rows, h * GDN_DV:(h + 1) * GDN_DV] = on * _silu(z[rows, h * GDN_DV:(h + 1) * GDN_DV])
    for (b, h), value in state.items():
        s_ref[b, h] = value


def _gdn(x, sc, sh, norm1_w, w_proj, w_g, w_gt, p_row, p_col, conv0, state0, conv_w, norm_w):
    n_seq, seq_len, d = x.shape
    chunk = min(GDN_CHUNK, seq_len)
    assert seq_len % chunk == 0 and chunk % SUBLANES == 0
    n_tok = n_seq * seq_len
    stacked = seq_len == SUBLANES and n_seq % GDN_STEP_CHUNKS == 0
    if stacked:
        n_step_chunks, sb, nt = GDN_STEP_CHUNKS, GDN_STEP_CHUNKS, 1
    else:
        n_step_chunks = GDN_STEP_CHUNKS if (seq_len // chunk) % GDN_STEP_CHUNKS == 0 else 1
        sb, nt = 1, seq_len // (n_step_chunks * chunk)
    step_rows = n_step_chunks * chunk
    tb = step_rows // sb
    row = lambda s, t: s * nt + t
    ada = pl.BlockSpec((sb, 1, d), lambda s, t: (s, 0, 0))
    extra = w_proj.shape[1] - (QKV_W + d)
    extra_spec = [pl.BlockSpec((step_rows, extra), lambda s, t: (row(s, t), 0))] if extra else []
    extra_shape = [jax.ShapeDtypeStruct((n_tok, extra), F32)] if extra else []
    return pl.pallas_call(
        functools.partial(_gdn_body, chunk=chunk, n_chunks=n_step_chunks),
        grid=(n_seq // sb, nt),
        in_specs=[pl.BlockSpec((sb, tb, d), lambda s, t: (s, t, 0)),
                  ada, ada, _resident((1, d)),
                  _resident(w_proj.shape), _resident((d, N_GATE_COLS)), _resident((N_GATE_COLS, d)),
                  _resident((2, N_GATE_COLS)), _resident((N_GATE_COLS, 2)),
                  pl.BlockSpec((sb, SUBLANES, QKV_W), lambda s, t: (s, 0, 0)),
                  pl.BlockSpec((sb, GDN_HEADS, GDN_DK, GDN_DV), lambda s, t: (s, 0, 0, 0)),
                  _resident((GDN_CONV, QKV_W)), _resident((1, GDN_DV))],
        out_specs=[pl.BlockSpec((step_rows, d), lambda s, t: (row(s, t), 0)),
                   pl.BlockSpec((sb, GDN_HEADS, GDN_DK, GDN_DV), lambda s, t: (s, 0, 0, 0)),
                   pl.BlockSpec((sb, SUBLANES, QKV_W), lambda s, t: (s, 0, 0))] + extra_spec,
        out_shape=[jax.ShapeDtypeStruct((n_tok, d), F32),
                   jax.ShapeDtypeStruct((n_seq, GDN_HEADS, GDN_DK, GDN_DV), F32),
                   jax.ShapeDtypeStruct((n_seq, SUBLANES, QKV_W), F32)] + extra_shape,
        scratch_shapes=[pltpu.VMEM((SUBLANES, QKV_W), F32)],
        compiler_params=_params(("arbitrary", "arbitrary")),
        name="gdn",
    )(x, sc, sh, norm1_w, w_proj, w_g, w_gt, p_row, p_col, conv0, state0, conv_w, norm_w)


def _merge_body(x_ref, on_ref, *refs, per_seq, n_tiles, projected):
    n_head = 5 if projected else 4
    (sc0_ref, g1_ref, sc2_ref, sh2_ref, wa_ref, wb_ref, wo_ref, cw_ref, n2_ref, wr_ref, br_ref, cnt0_ref,
     x1_ref, h2_ref, pos_ref, rw_ref, cnt_ref, tail_ref, post_ref, tbl_ref, cnt_scr, hist_scr) = refs[n_head:]
    step = pl.program_id(0) * pl.num_programs(1) + pl.program_id(1)

    @pl.when(step == 0)
    def _():
        cnt_scr[...] = cnt0_ref[...]

    tm, d = on_ref.shape
    if projected:
        sc_b, sc_c, sc_h, gate_a, gate_b = (r[...] for r in refs[:n_head])
    else:
        sc1_ref, sh1_ref, n1_ref, w5_ref = refs[:n_head]
        p5 = jnp.dot(_ada_norm(x_ref, sc1_ref, sh1_ref, n1_ref), w5_ref[...], preferred_element_type=F32)
        sc_b, sc_c, sc_h, gate_a, gate_b = (p5[:, j * d:(j + 1) * d] for j in range(5))
    pre = sc_c * sc_h
    if per_seq:
        hist = sc0_ref[...].reshape(tm, d)
        tail_ref[...] = pre
    else:
        @pl.when(pl.program_id(1) == 0)
        def _():
            hist_scr[...] = sc0_ref[0]

        hist = hist_scr[...]
        hist_scr[...] = pre[tm - SUBLANES:, :]
        tail_ref[...] = pre[tm - SUBLANES:, :]
    y_b = _bdot(sc_b * _causal_conv(pre, hist, cw_ref[...], per_seq), wb_ref[...])
    y_a = _bdot(on_ref[...], wa_ref[...])
    merged = jax.nn.sigmoid(gate_a) * y_a + jax.nn.sigmoid(gate_b) * y_b
    mo = _bdot(merged, wo_ref[...]).reshape(x_ref.shape)
    x1 = x_ref[...] + g1_ref[...] * mo
    y = x1 * lax.rsqrt(jnp.mean(x1 * x1, axis=-1, keepdims=True) + NORM_EPS) * n2_ref[...]
    h2 = (y * (1.0 + sc2_ref[...]) + sh2_ref[...]).reshape(tm, d)
    x1_ref[...] = x1.reshape(tm, d)
    h2_ref[...] = h2

    h2_hi = h2.astype(BF16)
    h2_lo = (h2 - h2_hi.astype(F32)).astype(BF16)
    w_hi = wr_ref[0]
    logits = (jnp.dot(h2_hi, w_hi, preferred_element_type=F32)
              + (jnp.dot(h2_hi, wr_ref[1], preferred_element_type=F32)
                 + jnp.dot(h2_lo, w_hi, preferred_element_type=F32))) + br_ref[...]
    lane = lax.broadcasted_iota(I32, logits.shape, 1)
    lane_f = lane.astype(F32)
    work = logits
    vals, hots = [], []
    member = jnp.zeros(logits.shape, F32)
    for _ in range(TOP_K):
        m = jnp.max(work, axis=-1, keepdims=True)
        sel = jnp.min(jnp.where(work == m, lane_f, float(N_EXPERTS - 1)), axis=-1, keepdims=True)
        hot = lane_f == sel
        vals.append(m)
        hots.append(hot)
        member = member + hot.astype(F32)
        work = jnp.where(hot, -jnp.inf, work)
    exps = [jnp.exp(v - vals[0]) for v in vals]
    denom = exps[0] + exps[1] + exps[2] + exps[3]
    ti = lax.broadcasted_iota(I32, (tm, tm), 0)
    tj = lax.broadcasted_iota(I32, (tm, tm), 1)
    before = (tj < ti).astype(BF16)
    rank_loc = jnp.dot(before, member.astype(BF16), preferred_element_type=F32)
    cnt = jnp.sum(member, axis=0, keepdims=True).astype(I32)
    seg = lax.shift_left(lax.shift_right_logical(cnt + (SUBLANES - 1), 3), 3)
    seg8 = jnp.broadcast_to(seg.astype(F32), (SUBLANES, ROUTER_LANES))
    ei = lax.broadcasted_iota(I32, (ROUTER_LANES, ROUTER_LANES), 0)
    ej = lax.broadcasted_iota(I32, (ROUTER_LANES, ROUTER_LANES), 1)
    base8 = jnp.dot(seg8, (ei < ej).astype(F32), precision=HIGHEST, preferred_element_type=F32)
    row_all = base8[0:1, :] + rank_loc
    pos = jnp.zeros((tm, ROUTER_LANES), F32)
    rw = jnp.zeros((tm, ROUTER_LANES), F32)
    for kk in range(TOP_K):
        pos_k = jnp.sum(jnp.where(hots[kk], row_all, 0.0), axis=-1, keepdims=True)
        pos = jnp.where(lane == kk, pos_k, pos)
        rw = jnp.where(lane == kk, exps[kk] / denom, rw)
    pos_ref[...] = pos
    rw_ref[...] = rw
    pick = (lax.broadcasted_iota(I32, (SUBLANES, ROUTER_LANES), 0)
            == lax.broadcasted_iota(I32, (SUBLANES, ROUTER_LANES), 1)).astype(F32)
    post_ref[...] = lax.dot_general(pick, pos, _NT, precision=HIGHEST, preferred_element_type=F32)
    sub = lax.broadcasted_iota(I32, (SUBLANES, ROUTER_LANES), 0)
    tbl_ref[...] = jnp.where(sub == 0, base8.astype(I32),
                             jnp.where(sub == 1, seg8.astype(I32), cnt_scr[...].astype(I32)))
    cnt_scr[...] = cnt_scr[...] + seg8

    @pl.when(step == n_tiles - 1)
    def _():
        cnt_ref[...] = cnt_scr[...]


def _merge(x, on, p5, sc1, sh1, norm1_w, w5, sc0, g1, sc2, sh2, w_a, w_b, w_o, conv_w, norm2_w, w_r, b_r, cnt0):
    n_seq, seq_len, d = x.shape
    sb, tb = _route_tile(n_seq, seq_len)
    per_seq = sb > 1 or tb == SUBLANES
    if per_seq:
        assert tb == SUBLANES
    tm = sb * tb
    nt = seq_len // tb
    ns = n_seq // sb
    n_tok = n_seq * seq_len
    row = lambda s, t: s * nt + t
    full = lambda shape: pl.BlockSpec(shape, lambda s, t: (0,) * len(shape))
    ada = pl.BlockSpec((sb, 1, d), lambda s, t: (s, 0, 0))
    tok = pl.BlockSpec((tm, d), lambda s, t: (row(s, t), 0))
    lanes = pl.BlockSpec((tm, ROUTER_LANES), lambda s, t: (row(s, t), 0))
    tail_rows = tm if per_seq else SUBLANES
    if p5 is None:
        head_specs = [ada, ada, _resident((1, d)), _resident((d, 5 * d))]
        head_args = [sc1, sh1, norm1_w, w5]
    else:
        head_specs = [pl.BlockSpec((tm, d), lambda s, t, j=j: (row(s, t), j)) for j in range(5)]
        head_args = [p5] * 5
    return pl.pallas_call(
        functools.partial(_merge_body, per_seq=per_seq, n_tiles=ns * nt, projected=p5 is not None),
        grid=(ns, nt),
        in_specs=[pl.BlockSpec((sb, tb, d), lambda s, t: (s, t, 0)),
                  tok] + head_specs + [
                  pl.BlockSpec((sb, SUBLANES, d), lambda s, t: (s, 0, 0)),
                  ada, ada, ada,
                  _resident((d, d)), _resident((d, d)), _resident((d, d)),
                  _resident((SC_CONV, d)), _resident((1, d)), _resident((2, d, ROUTER_LANES)),
                  _resident((1, ROUTER_LANES)), _resident((SUBLANES, ROUTER_LANES))],
        out_specs=[tok, tok, lanes, lanes, full((SUBLANES, ROUTER_LANES)),
                   pl.BlockSpec((tail_rows, d), lambda s, t: (row(s, t), 0)),
                   pl.BlockSpec((SUBLANES, tm), lambda s, t: (0, row(s, t))),
                   pl.BlockSpec((SUBLANES, ROUTER_LANES), lambda s, t: (row(s, t), 0))],
        out_shape=[jax.ShapeDtypeStruct((n_tok, d), F32),
                   jax.ShapeDtypeStruct((n_tok, d), F32),
                   jax.ShapeDtypeStruct((n_tok, ROUTER_LANES), F32),
                   jax.ShapeDtypeStruct((n_tok, ROUTER_LANES), F32),
                   jax.ShapeDtypeStruct((SUBLANES, ROUTER_LANES), F32),
                   jax.ShapeDtypeStruct((ns * nt * tail_rows, d), F32),
                   jax.ShapeDtypeStruct((SUBLANES, n_tok), F32),
                   jax.ShapeDtypeStruct((ns * nt * SUBLANES, ROUTER_LANES), I32)],
        scratch_shapes=[pltpu.VMEM((SUBLANES, ROUTER_LANES), F32), pltpu.VMEM((SUBLANES, d), F32)],
        compiler_params=_params(("arbitrary", "arbitrary")),
        name="merge",
    )(x, on, *head_args, sc0, g1, sc2, sh2, w_a, w_b, w_o, conv_w, norm2_w, w_r, b_r, cnt0)


def _pow2_pieces(length, max_rows):
    out = []
    rows = max_rows
    while rows >= SUBLANES:
        shift = rows.bit_length()
        offset = lax.shift_left(lax.shift_right_logical(length, shift), shift)
        out.append(((length & rows) != 0, offset, rows))
        rows //= 2
    return out


def _segment_copies(tbl_ref, tile_id, local_ref, global_ref, sem, to_global, max_rows):
    out = []
    base = tile_id * (3 * N_EXPERTS)
    for e in range(N_EXPERTS):
        local0 = tbl_ref[base + e]
        length = tbl_ref[base + N_EXPERTS + e]
        global0 = tbl_ref[base + 2 * N_EXPERTS + e]
        for pred, offset, rows in _pow2_pieces(length, max_rows):
            loc = local_ref.at[pl.ds(pl.multiple_of(local0 + offset, SUBLANES), rows)]
            glo = global_ref.at[pl.ds(pl.multiple_of(global0 + offset, SUBLANES), rows)]
            cp = pltpu.make_async_copy(loc, glo, sem) if to_global else pltpu.make_async_copy(glo, loc, sem)
            out.append((pred, cp))
    return out


def _start_all(copies):
    for pred, cp in copies:
        pl.when(pred)(cp.start)


def _wait_all(copies):
    for pred, cp in copies:
        pl.when(pred)(cp.wait)


def _dispatch_body(tbl_ref, tails_ref, post_ref, h2_ref, *refs, tile0, zero_tails):
    buf_ref, sorted_scr, zero_scr, sems = refs[-4:]
    i = pl.program_id(0)
    slot = lax.rem(i, 2)
    tm = h2_ref.shape[0]
    r_rows = sorted_scr.shape[1]

    if zero_tails:
        @pl.when(i == 0)
        def _():
            zero_scr[...] = jnp.zeros(zero_scr.shape, F32)
            z_rows = zero_scr.shape[0]

            def fill(b, carry):
                cp = pltpu.make_async_copy(zero_scr, buf_ref.at[pl.ds(pl.multiple_of(b * SUBLANES, SUBLANES), z_rows)],
                                           sems.at[2])
                cp.start()
                cp.wait()
                return carry

            tails = []
            for e in range(N_EXPERTS):
                start, length = tails_ref[e], tails_ref[N_EXPERTS + e]
                whole = length // z_rows
                lax.fori_loop(0, whole, lambda b, c, s=start: fill(s // SUBLANES + b * (z_rows // SUBLANES), c), 0)
                rest0 = start + whole * z_rows
                for pred, offset, rows in _pow2_pieces(length - whole * z_rows, z_rows // 2):
                    at = pl.multiple_of(rest0 + offset, SUBLANES)
                    tails.append((pred, pltpu.make_async_copy(zero_scr.at[pl.ds(0, rows)],
                                                              buf_ref.at[pl.ds(at, rows)], sems.at[2])))
            _start_all(tails)
            _wait_all(tails)
            lax.fori_loop(tails_ref[2 * N_EXPERTS] // z_rows, buf_ref.shape[0] // z_rows,
                          lambda b, c: fill(b * (z_rows // SUBLANES), c), 0)

    pos = post_ref[...].astype(I32)
    j = lax.broadcasted_iota(I32, (r_rows, tm), 0)
    onehot = jnp.zeros((r_rows, tm), F32)
    for k in range(TOP_K):
        onehot = jnp.where(j == pos[k:k + 1, :], 1.0, onehot)
    sorted_scr[slot] = _bdot(onehot, h2_ref[...])

    _start_all(_segment_copies(tbl_ref, tile0 + i, sorted_scr.at[slot], buf_ref, sems.at[slot], True, tm))

    @pl.when(i > 0)
    def _():
        _wait_all(_segment_copies(tbl_ref, tile0 + i - 1, sorted_scr.at[1 - slot], buf_ref, sems.at[1 - slot],
                                  True, tm))

    @pl.when(i == pl.num_programs(0) - 1)
    def _():
        _wait_all(_segment_copies(tbl_ref, tile0 + i, sorted_scr.at[slot], buf_ref, sems.at[slot], True, tm))


def _dispatch(tbl, tails, post, h2, buf, buf_rows, tm, tile0):
    n_tok, d = h2.shape
    r_rows = TOP_K * tm + N_EXPERTS * SUBLANES
    first = buf is None
    grid_spec = pltpu.PrefetchScalarGridSpec(
        num_scalar_prefetch=2,
        grid=(n_tok // tm,),
        in_specs=[pl.BlockSpec((SUBLANES, tm), lambda i, *_: (0, i)),
                  pl.BlockSpec((tm, d), lambda i, *_: (i, 0))]
                 + ([] if first else [pl.BlockSpec(memory_space=pl.ANY)]),
        out_specs=pl.BlockSpec(memory_space=pl.ANY),
        scratch_shapes=[pltpu.VMEM((2, r_rows, d), F32),
                        pltpu.VMEM((EXPERT_BLOCK // 2, d), F32),
                        pltpu.SemaphoreType.DMA((3,))],
    )
    return pl.pallas_call(
        functools.partial(_dispatch_body, tile0=tile0, zero_tails=first),
        grid_spec=grid_spec,
        out_shape=jax.ShapeDtypeStruct((buf_rows, d), F32),
        input_output_aliases={} if first else {4: 0},
        compiler_params=pltpu.CompilerParams(dimension_semantics=("arbitrary",), has_side_effects=True,
                                             vmem_limit_bytes=VMEM_LIMIT),
        name="dispatch",
    )(tbl, tails, post, h2, *([] if first else [buf]))


EXPERT_LOOKAHEAD = (2, 1, 0)
EXPERT_LEAD = max(EXPERT_LOOKAHEAD)
EXPERT_SLOTS = EXPERT_LEAD + 1


def _expert_body(be_ref, ord_ref, nv_ref, x_ref, wg_ref, bg_ref, wu_ref, bu_ref, wd_ref, bd_ref, o_ref,
                 wg_s, wu_s, wd_s):
    del be_ref
    j = pl.program_id(0)
    n_blocks = pl.num_programs(0) - EXPERT_LEAD
    n_valid = nv_ref[0]
    blk = j - EXPERT_LEAD

    for w_ref, w_scr, ahead in zip((wg_ref, wu_ref, wd_ref), (wg_s, wu_s, wd_s), EXPERT_LOOKAHEAD):
        t = blk + ahead
        tc = jnp.clip(t, 0, n_blocks - 1)
        first = (t == 0) | (ord_ref[tc] != ord_ref[jnp.maximum(tc - 1, 0)])

        @pl.when((t >= 0) & (t < n_valid) & first)
        def _(w_ref=w_ref, w_scr=w_scr, tc=tc):
            w_scr[lax.rem(ord_ref[tc], EXPERT_SLOTS)] = w_ref[0].astype(BF16)

    @pl.when((blk >= 0) & (blk < n_valid))
    def _():
        slot = lax.rem(ord_ref[jnp.maximum(blk, 0)], EXPERT_SLOTS)
        x = x_ref[...].astype(BF16)
        gate = jnp.dot(x, wg_s[slot], preferred_element_type=F32) + bg_ref[0]
        up = jnp.dot(x, wu_s[slot], preferred_element_type=F32) + bu_ref[0]
        gate = jnp.minimum(gate, SWIGLU_LIMIT)
        up = jnp.clip(up, -SWIGLU_LIMIT, SWIGLU_LIMIT)
        glu = gate * jax.nn.sigmoid(SWIGLU_ALPHA * gate)
        o_ref[...] = _bdot((up + 1.0) * glu, wd_s[slot]) + bd_ref[0]

    @pl.when(blk >= n_valid)
    def _():
        o_ref[...] = jnp.zeros(o_ref.shape, F32)


def _experts(block_e, block_ord, n_valid, xs, w_gate, b_gate, w_up, b_up, w_down, b_down):
    m_pad, d = xs.shape
    n_blocks = m_pad // EXPERT_BLOCK
    f = w_gate.shape[2]
    at = lambda j, ahead: jnp.clip(j - EXPERT_LEAD + ahead, 0, n_blocks - 1)
    rows_in = lambda j, be, od, nv: (jnp.clip(j - EXPERT_LEAD, 0, jnp.maximum(nv[0] - 1, 0)), 0)
    rows_out = lambda j, be, od, nv: (at(j, 0), 0)
    wspec = lambda a, b, ahead: pl.BlockSpec((1, a, b), lambda j, be, od, nv: (be[at(j, ahead)], 0, 0))
    ahead_g, ahead_u, ahead_d = EXPERT_LOOKAHEAD
    grid_spec = pltpu.PrefetchScalarGridSpec(
        num_scalar_prefetch=3,
        grid=(n_blocks + EXPERT_LEAD,),
        in_specs=[pl.BlockSpec((EXPERT_BLOCK, d), rows_in),
                  wspec(d, f, ahead_g), wspec(1, f, 0), wspec(d, f, ahead_u), wspec(1, f, 0),
                  wspec(f, d, ahead_d), wspec(1, d, 0)],
        out_specs=pl.BlockSpec((EXPERT_BLOCK, d), rows_out),
        scratch_shapes=[pltpu.VMEM((EXPERT_SLOTS, d, f), BF16), pltpu.VMEM((EXPERT_SLOTS, d, f), BF16),
                        pltpu.VMEM((EXPERT_SLOTS, f, d), BF16)],
    )
    return pl.pallas_call(
        _expert_body,
        grid_spec=grid_spec,
        out_shape=jax.ShapeDtypeStruct((m_pad, d), F32),
        compiler_params=_params(("arbitrary",)),
        name="experts",
    )(block_e, block_ord, n_valid, xs, w_gate, b_gate[:, None, :], w_up, b_up[:, None, :], w_down,
      b_down[:, None, :])


def _combine_body(tbl_ref, ys_ref, pos_ref, rw_ref, x1_ref, g2_ref, fw_ref, o_ref, blk_scr, sems, *, tile0):
    step = pl.program_id(0) * pl.num_programs(1) + pl.program_id(1)
    n_steps = pl.num_programs(0) * pl.num_programs(1)
    slot = lax.rem(step, 2)
    tm = pos_ref.shape[0]
    r_rows = blk_scr.shape[1]

    def fetch(tile, into):
        return _segment_copies(tbl_ref, tile0 + tile, blk_scr.at[into], ys_ref, sems.at[into], False, tm)

    @pl.when(step == 0)
    def _():
        blk_scr[...] = jnp.zeros(blk_scr.shape, F32)
        _start_all(fetch(0, 0))

    @pl.when(step + 1 < n_steps)
    def _():
        _start_all(fetch(step + 1, 1 - slot))

    _wait_all(fetch(step, slot))

    pos = pos_ref[...].astype(I32)
    rw = rw_ref[...]
    j = lax.broadcasted_iota(I32, (tm, r_rows), 1)
    pw = jnp.zeros((tm, r_rows), F32)
    for k in range(TOP_K):
        pw = jnp.where(j == pos[:, k:k + 1], rw[:, k:k + 1], pw)
    ffn = _bdot(pw, blk_scr[slot])
    y = x1_ref[...] + g2_ref[...] * ffn.reshape(x1_ref.shape)
    o_ref[...] = y * lax.rsqrt(jnp.mean(y * y, axis=-1, keepdims=True) + NORM_EPS) * fw_ref[...]


def _combine(tbl, ys, pos, rw, x1, g2, final_w, sb, tb, tile0):
    n_seq, seq_len, d = x1.shape
    tm = sb * tb
    nt = seq_len // tb
    r_rows = TOP_K * tm + N_EXPERTS * SUBLANES
    row = lambda s, t: s * nt + t
    grid_spec = pltpu.PrefetchScalarGridSpec(
        num_scalar_prefetch=1,
        grid=(n_seq // sb, nt),
        in_specs=[pl.BlockSpec(memory_space=pl.ANY),
                  pl.BlockSpec((tm, ROUTER_LANES), lambda s, t, *_: (row(s, t), 0)),
                  pl.BlockSpec((tm, ROUTER_LANES), lambda s, t, *_: (row(s, t), 0)),
                  pl.BlockSpec((sb, tb, d), lambda s, t, *_: (s, t, 0)),
                  pl.BlockSpec((sb, 1, d), lambda s, t, *_: (s, 0, 0)),
                  pl.BlockSpec((1, d), lambda s, t, *_: (0, 0))],
        out_specs=pl.BlockSpec((sb, tb, d), lambda s, t, *_: (s, t, 0)),
        scratch_shapes=[pltpu.VMEM((2, r_rows, d), F32), pltpu.SemaphoreType.DMA((2,))],
    )
    return pl.pallas_call(
        functools.partial(_combine_body, tile0=tile0),
        grid_spec=grid_spec,
        out_shape=jax.ShapeDtypeStruct((n_seq, seq_len, d), F32),
        compiler_params=_params(("arbitrary", "arbitrary")),
        name="combine",
    )(tbl, ys, pos, rw, x1, g2, final_w)


def _pad_state(state, rows):
    return jnp.pad(state, ((0, 0), (rows - state.shape[1], 0), (0, 0)))


def kernel(x_prompt, x_sample, c_prompt, c_sample, state_gdn_conv, state_gdn_rec, state_sc_conv, w_ada, b_ada,
           norm1_w, w_in, gdn_conv_w, gdn_a_log, gdn_dt_bias, gdn_norm_w, w_branch_a, sc_conv_w, w_branch_b,
           w_out, norm2_w, w_router, b_router, w_gate, b_gate, w_up, b_up, w_down, b_down, final_norm_w):
    assert w_ada.shape[0] == 1, "single-layer trunk"
    d = D_MODEL
    bp, tp, _ = x_prompt.shape
    bs, ts, _ = x_sample.shape
    n_p, n_s = bp * tp, bs * ts
    n_tok = n_p + n_s

    w_in0 = w_in[0]
    g_lo, g_hi = QKV_W + d, QKV_W + d + N_GATE_COLS
    w_qkvz = w_in0[:, :g_lo].astype(BF16)
    w5 = w_in0[:, g_hi:].astype(BF16)
    w_all = jnp.concatenate([w_qkvz, w5], axis=1)
    w_g = w_in0[:, g_lo:g_hi].astype(BF16)
    w_gt = w_g.T
    zeros_h = jnp.zeros((GDN_HEADS,), F32)
    p_row = jnp.stack([jnp.concatenate([zeros_h, gdn_a_log[0]]), jnp.concatenate([zeros_h, gdn_dt_bias[0]])])
    p_col = p_row.T
    w_a = w_branch_a[0].astype(BF16)
    w_b = w_branch_b[0].astype(BF16)
    w_o = w_out[0].astype(BF16)
    w_r = jnp.pad(w_router[0], ((0, 0), (0, ROUTER_LANES - N_EXPERTS)))
    w_r_hi = w_r.astype(BF16)
    w_r = jnp.stack([w_r_hi, (w_r - w_r_hi.astype(F32)).astype(BF16)])
    b_r = jnp.pad(b_router[0], (0, ROUTER_LANES - N_EXPERTS), constant_values=-jnp.inf).reshape(1, ROUTER_LANES)

    n_c = bp + bs
    c_rows = -(-n_c // 16) * 16
    c_all = jnp.pad(jnp.concatenate([c_prompt, c_sample], axis=0), ((0, c_rows - n_c), (0, 0)))
    ada = _ada(c_all, w_ada[0], b_ada[0])

    def ada_parts(lo, hi):
        return [ada[lo:hi, j * d:(j + 1) * d].reshape(hi - lo, 1, d) for j in range(6)]

    groups = [
        dict(x=x_prompt, ada=ada_parts(0, bp), n_seq=bp, seq_len=tp,
             conv0=jnp.zeros((bp, SUBLANES, QKV_W), F32),
             rec0=jnp.zeros((bp, GDN_HEADS, GDN_DK, GDN_DV), F32),
             sc0=jnp.zeros((bp, SUBLANES, d), F32)),
        dict(x=x_sample, ada=ada_parts(bp, n_c), n_seq=bs, seq_len=ts,
             conv0=_pad_state(state_gdn_conv[0], SUBLANES),
             rec0=state_gdn_rec[0],
             sc0=_pad_state(state_sc_conv[0], SUBLANES)),
    ]

    counts = jnp.zeros((SUBLANES, ROUTER_LANES), F32)
    for g in groups:
        sh1, sc1, g1, sh2, sc2, g2 = g["ada"]
        n_seq, seq_len = g["n_seq"], g["seq_len"]
        wide = seq_len > SUBLANES
        on, rec, conv_tail, *p5 = _gdn(g["x"], sc1, sh1, norm1_w, w_all if wide else w_qkvz, w_g, w_gt, p_row, p_col,
                                       g["conv0"], g["rec0"], gdn_conv_w[0], gdn_norm_w)
        x1, h2, pos, rw, counts, tail, post, tbl = _merge(g["x"], on, p5[0] if wide else None, sc1, sh1, norm1_w, w5,
                                                          g["sc0"], g1, sc2, sh2, w_a, w_b, w_o, sc_conv_w[0],
                                                          norm2_w, w_r, b_r, counts)
        sb, tb = _route_tile(n_seq, seq_len)
        g.update(x1=x1, h2=h2, pos=pos, rw=rw, post=post, tbl=tbl, rec=rec, g2=g2, sb=sb, tb=tb, tm=sb * tb,
                 rows_after=counts[0, :N_EXPERTS].astype(I32),
                 new_conv=conv_tail[:, SUBLANES - (GDN_CONV - 1):, :],
                 new_sc=tail.reshape(n_seq, -1, d)[:, -(SC_CONV - 1):, :])

    rows_e = counts[0, :N_EXPERTS].astype(I32)
    padded = (rows_e + EXPERT_BLOCK - 1) // EXPERT_BLOCK * EXPERT_BLOCK
    pad_end = jnp.cumsum(padded)
    expert0 = (pad_end - padded).astype(I32)
    tbl_all = jnp.concatenate([g["tbl"] for g in groups], axis=0).reshape(-1, SUBLANES, ROUTER_LANES)
    n_tiles = tbl_all.shape[0]
    tbl_all = tbl_all[:, :3, :N_EXPERTS].at[:, 2, :].add(expert0[None, :]).reshape(-1)
    rows_first = groups[0]["rows_after"]
    tails = jnp.concatenate([expert0 + rows_first, padded - rows_first, pad_end[-1:]]).astype(I32)
    max_rows = n_tok * TOP_K + n_tiles * N_EXPERTS * (SUBLANES - 1)
    n_blocks = -(-max_rows // EXPERT_BLOCK) + N_EXPERTS
    block_start = jnp.arange(n_blocks, dtype=I32) * EXPERT_BLOCK
    block_e = jnp.minimum(jnp.sum((pad_end[None, :] <= block_start[:, None]).astype(I32), axis=1), N_EXPERTS - 1)
    n_valid = (pad_end[-1:] // EXPERT_BLOCK).astype(I32)
    block_ord = jnp.cumsum(jnp.concatenate([jnp.zeros((1,), I32),
                                            (block_e[1:] != block_e[:-1]).astype(I32)])).astype(I32)

    buf = None
    tile0 = 0
    for g in groups:
        g["tile0"] = tile0
        buf = _dispatch(tbl_all, tails, g["post"], g["h2"], buf, n_blocks * EXPERT_BLOCK, g["tm"], tile0)
        tile0 += g["h2"].shape[0] // g["tm"]
    ys = _experts(block_e, block_ord, n_valid, buf, w_gate[0], b_gate[0], w_up[0], b_up[0], w_down[0], b_down[0])
    outs = []
    for g in groups:
        n_seq, seq_len = g["n_seq"], g["seq_len"]
        x1 = g["x1"].reshape(n_seq, seq_len, d)
        outs.append(_combine(tbl_all, ys, g["pos"], g["rw"], x1, g["g2"], final_norm_w.reshape(1, d),
                             g["sb"], g["tb"], g["tile0"]))

    gp, gs = groups
    return (outs[0], outs[1], gp["new_conv"][None], gp["rec"][None], gp["new_sc"][None],
            gs["new_conv"][None], gs["rec"][None], gs["new_sc"][None])
```

```python
import functools

import jax
import jax.numpy as jnp
from jax import lax
from jax.experimental import pallas as pl
from jax.experimental.pallas import tpu as pltpu

F32 = jnp.float32
BF16 = jnp.bfloat16
I32 = jnp.int32
HIGHEST = lax.Precision.HIGHEST

D_MODEL = 1024
GDN_HEADS = 8
GDN_DK = 128
GDN_DV = 128
GDN_QK = GDN_HEADS * GDN_DK
QKV_W = 3 * GDN_QK
GDN_CONV = 4
GDN_CHUNK = 64
GDN_STEP_CHUNKS = 4
GDN_STEP_SEQS = 8
GDN_WAVE_CHUNKS = 8
SC_CONV = 3
N_EXPERTS = 32
TOP_K = 4
SWIGLU_LIMIT = 7.0
SWIGLU_ALPHA = 1.702
NORM_EPS = 1e-6
N_GATE_COLS = 2 * GDN_HEADS
ROUTER_LANES = 128
EXPERT_BLOCK = 512
SUBLANES = 8
VMEM_LIMIT = 56 * 1024 * 1024

_NT = (((1,), (1,)), ((), ()))
_TN = (((0,), (0,)), ((), ()))


def _bdot(a, b):
    return jnp.dot(a.astype(BF16), b.astype(BF16), preferred_element_type=F32)


def _bdot_nt(a, b):
    return lax.dot_general(a.astype(BF16), b.astype(BF16), _NT, preferred_element_type=F32)


def _bdot_tn(a, b):
    return lax.dot_general(a.astype(BF16), b.astype(BF16), _TN, preferred_element_type=F32)


def _silu(x):
    return x * jax.nn.sigmoid(x)


def _softplus(x):
    return jnp.maximum(x, 0.0) + jnp.log1p(jnp.exp(-jnp.abs(x)))


def _seq_tile(n_seq, seq_len, target):
    if seq_len >= target:
        assert seq_len % target == 0
        return 1, target
    sb = min(n_seq, target // seq_len)
    assert n_seq % sb == 0 and seq_len % SUBLANES == 0
    return sb, seq_len


def _route_tile(n_seq, seq_len):
    return _seq_tile(n_seq, seq_len, 512)


def _params(sem):
    return pltpu.CompilerParams(dimension_semantics=sem, vmem_limit_bytes=VMEM_LIMIT)


def _ada_body(c_ref, w_ref, b_ref, o_ref):
    o_ref[...] = _bdot(_silu(c_ref[...]), w_ref[...]) + b_ref[...]


def _ada(c, w_ada, b_ada):
    rows, d = c.shape
    n = w_ada.shape[1]
    tn = 1024
    return pl.pallas_call(
        _ada_body,
        grid=(n // tn,),
        in_specs=[pl.BlockSpec((rows, d), lambda j: (0, 0)),
                  pl.BlockSpec((d, tn), lambda j: (0, j)),
                  pl.BlockSpec((1, tn), lambda j: (0, j))],
        out_specs=pl.BlockSpec((rows, tn), lambda j: (0, j)),
        out_shape=jax.ShapeDtypeStruct((rows, n), F32),
        compiler_params=_params(("arbitrary",)),
        name="ada",
    )(c, w_ada, b_ada.reshape(1, n))


def _gates(v, a_log, dt_bias, axis):
    is_beta = lax.broadcasted_iota(I32, v.shape, axis) < GDN_HEADS
    beta = jax.nn.sigmoid(v)
    g = -jnp.exp(a_log) * _softplus(v + dt_bias)
    return jnp.where(is_beta, beta, g)


def _ada_norm(x_ref, sc_ref, sh_ref, nw_ref):
    x = x_ref[...]
    y = x * lax.rsqrt(jnp.mean(x * x, axis=-1, keepdims=True) + NORM_EPS) * nw_ref[...]
    h = y * (1.0 + sc_ref[...]) + sh_ref[...]
    return h.reshape(x.shape[0] * x.shape[1], x.shape[2]).astype(BF16)


def _resident(shape):
    return pl.BlockSpec(shape, lambda *_: (0,) * len(shape), pipeline_mode=pl.Buffered(1))


def _shift_rows(x, hist, s, per_seq):
    rows, width = x.shape
    x3 = x.reshape(rows // SUBLANES, SUBLANES, width)
    xr = pltpu.roll(x3, s, 1)
    if per_seq:
        src = pltpu.roll(hist.reshape(x3.shape), s, 1)
    else:
        hr = pltpu.roll(hist.reshape(1, SUBLANES, width), s, 1)
        src = hr if rows == SUBLANES else jnp.concatenate([hr, xr[:-1]], axis=0)
    sub = lax.broadcasted_iota(I32, x3.shape, 1)
    return jnp.where(sub < s, src, xr).reshape(rows, width)


def _causal_conv(x, hist, w, per_seq):
    taps = w.shape[0]
    acc = x * w[taps - 1:taps, :]
    for s in range(1, taps):
        acc = acc + _shift_rows(x, hist, s, per_seq) * w[taps - 1 - s:taps - s, :]
    return acc


def _gdn_body(x_ref, sc_ref, sh_ref, n1_ref, wq_ref, wg_ref, wgt_ref, prow_ref, pcol_ref, conv0_ref, s0_ref,
              cw_ref, nw_ref, o_ref, s_ref, conv_ref, *rest, chunk, n_chunks):
    stacked = s_ref.shape[0] > 1
    hist_scr = rest[-1]

    @pl.when(pl.program_id(1) == 0)
    def _():
        s_ref[...] = s0_ref[...]
        if not stacked:
            hist_scr[...] = conv0_ref[0]

    h_in = _ada_norm(x_ref, sc_ref, sh_ref, n1_ref)
    n_own = QKV_W + D_MODEL
    proj = jnp.dot(h_in, wq_ref[:, :n_own], preferred_element_type=F32)
    x = proj[:, :QKV_W]
    z = proj[:, QKV_W:]
    if len(rest) == 2:
        rest[0][...] = jnp.dot(h_in, wq_ref[:, n_own:], preferred_element_type=F32)
    if stacked:
        hist = conv0_ref[...].reshape(x.shape)
        conv_ref[...] = x.reshape(conv_ref.shape)
    else:
        hist = hist_scr[...]
        hist_scr[...] = x[x.shape[0] - SUBLANES:, :]
        conv_ref[0] = x[x.shape[0] - SUBLANES:, :]
    qkvc = _silu(_causal_conv(x, hist, cw_ref[...], stacked))

    gcol = _gates(jnp.dot(h_in, wg_ref[...], preferred_element_type=F32), prow_ref[0:1, :], prow_ref[1:2, :], 1)
    ri = lax.broadcasted_iota(I32, (chunk, chunk), 0)
    ci = lax.broadcasted_iota(I32, (chunk, chunk), 1)
    causal = ri >= ci
    strict = ri > ci
    tri = causal.astype(F32)
    eye = (ri == ci).astype(F32)
    n_lvl = chunk.bit_length() - 1
    blk = [lax.shift_right_logical(ri, l) == lax.shift_right_logical(ci, l) for l in range(1, n_lvl + 1)]
    pair = [blk[l] & jnp.logical_not(blk[l - 1]) for l in range(1, n_lvl)]

    units = [(c, h) for c in range(n_chunks) for h in range(GDN_HEADS)]
    every = range(len(units))
    q, k, dcol, dlast, edec, qk, sol = {}, {}, {}, {}, {}, {}, {}
    for c0 in range(0, n_chunks, GDN_WAVE_CHUNKS):
        wave = [c * GDN_HEADS + h for c in range(c0, min(c0 + GDN_WAVE_CHUNKS, n_chunks)) for h in range(GDN_HEADS)]
        v, beta, decay = {}, {}, {}
        for c in range(c0, min(c0 + GDN_WAVE_CHUNKS, n_chunks)):
            rows = slice(c * chunk, (c + 1) * chunk)
            dec_col = jnp.dot(tri, gcol[rows, GDN_HEADS:], precision=HIGHEST, preferred_element_type=F32)
            grow = _gates(lax.dot_general(wgt_ref[...], h_in[rows, :], _NT, preferred_element_type=F32),
                          pcol_ref[:, 0:1], pcol_ref[:, 1:2], 0)
            dec_row = lax.dot_general(grow[GDN_HEADS:, :], tri, _NT, precision=HIGHEST, preferred_element_type=F32)
            for h in range(GDN_HEADS):
                i = c * GDN_HEADS + h
                qh = qkvc[rows, h * GDN_DK:(h + 1) * GDN_DK]
                kh = qkvc[rows, GDN_QK + h * GDN_DK:GDN_QK + (h + 1) * GDN_DK]
                q[i] = qh * (lax.rsqrt(jnp.sum(qh * qh, axis=-1, keepdims=True) + 1e-6) * (GDN_DK ** -0.5))
                k[i] = kh * lax.rsqrt(jnp.sum(kh * kh, axis=-1, keepdims=True) + 1e-6)
                v[i] = qkvc[rows, 2 * GDN_QK + h * GDN_DV:2 * GDN_QK + (h + 1) * GDN_DV]
                beta[i] = gcol[rows, h:h + 1]
                dcol[i] = dec_col[:, h:h + 1]
                dlast[i] = dec_col[chunk - 1:chunk, h:h + 1]
                decay[i] = jnp.where(causal, jnp.exp(dcol[i] - dec_row[h:h + 1, :]), 0.0)
                edec[i] = jnp.exp(dcol[i])
        kk = {i: _bdot_nt(k[i], k[i]) for i in wave}
        for i in wave:
            qk[i] = _bdot_nt(q[i], k[i]) * decay[i]
        a = {i: jnp.where(strict, beta[i] * kk[i] * decay[i], 0.0) for i in wave}
        inv = {i: eye - jnp.where(blk[0], a[i], 0.0) for i in wave}
        for lower_left in pair:
            right = {i: _bdot(jnp.where(lower_left, a[i], 0.0), inv[i]) for i in wave}
            inv = {i: inv[i] - _bdot(inv[i], right[i]) for i in wave}
        for i in wave:
            sol[i] = _bdot(inv[i], jnp.concatenate([beta[i] * v[i], (beta[i] * edec[i]) * k[i]], axis=1))
    seq_of = (lambda c: c) if stacked else (lambda c: 0)
    waves = [list(every)] if stacked else [[c * GDN_HEADS + h for h in range(GDN_HEADS)] for c in range(n_chunks)]
    state = {(seq_of(c), h): s_ref[seq_of(c), h] for c, h in units}
    for wave in waves:
        key = {i: (seq_of(units[i][0]), units[i][1]) for i in wave}
        ws = {i: _bdot(jnp.concatenate([sol[i][:, GDN_DV:], q[i] * edec[i]], axis=0), state[key[i]]) for i in wave}
        u = {i: sol[i][:, :GDN_DV] - ws[i][:chunk] for i in wave}
        o = {i: ws[i][chunk:] + _bdot(qk[i], u[i]) for i in wave}
        upd = {i: _bdot_tn(k[i] * jnp.exp(dlast[i] - dcol[i]), u[i]) for i in wave}
        for i in wave:
            c, h = units[i]
            state[key[i]] = state[key[i]] * jnp.exp(dlast[i]) + upd[i]
            rows = slice(c * chunk, (c + 1) * chunk)
            on = o[i] * lax.rsqrt(jnp.mean(o[i] * o[i], axis=-1, keepdims=True) + NORM_EPS) * nw_ref[...]
            o_ref[rows, h * GDN_DV:(h + 1) * GDN_DV] = on * _silu(z[rows, h * GDN_DV:(h + 1) * GDN_DV])
    for (b, h), value in state.items():
        s_ref[b, h] = value


def _gdn(x, sc, sh, norm1_w, w_proj, w_g, w_gt, p_row, p_col, conv0, state0, conv_w, norm_w):
    n_seq, seq_len, d = x.shape
    chunk = min(GDN_CHUNK, seq_len)
    assert seq_len % chunk == 0 and chunk % SUBLANES == 0
    n_tok = n_seq * seq_len
    stacked = seq_len == SUBLANES and n_seq % GDN_STEP_SEQS == 0
    if stacked:
        n_step_chunks, sb, nt = GDN_STEP_SEQS, GDN_STEP_SEQS, 1
    else:
        n_step_chunks = GDN_STEP_CHUNKS if (seq_len // chunk) % GDN_STEP_CHUNKS == 0 else 1
        sb, nt = 1, seq_len // (n_step_chunks * chunk)
    step_rows = n_step_chunks * chunk
    tb = step_rows // sb
    row = lambda s, t: s * nt + t
    ada = pl.BlockSpec((sb, 1, d), lambda s, t: (s, 0, 0))
    extra = w_proj.shape[1] - (QKV_W + d)
    extra_spec = [pl.BlockSpec((step_rows, extra), lambda s, t: (row(s, t), 0))] if extra else []
    extra_shape = [jax.ShapeDtypeStruct((n_tok, extra), F32)] if extra else []
    return pl.pallas_call(
        functools.partial(_gdn_body, chunk=chunk, n_chunks=n_step_chunks),
        grid=(n_seq // sb, nt),
        in_specs=[pl.BlockSpec((sb, tb, d), lambda s, t: (s, t, 0)),
                  ada, ada, _resident((1, d)),
                  _resident(w_proj.shape), _resident((d, N_GATE_COLS)), _resident((N_GATE_COLS, d)),
                  _resident((2, N_GATE_COLS)), _resident((N_GATE_COLS, 2)),
                  pl.BlockSpec((sb, SUBLANES, QKV_W), lambda s, t: (s, 0, 0)),
                  pl.BlockSpec((sb, GDN_HEADS, GDN_DK, GDN_DV), lambda s, t: (s, 0, 0, 0)),
                  _resident((GDN_CONV, QKV_W)), _resident((1, GDN_DV))],
        out_specs=[pl.BlockSpec((step_rows, d), lambda s, t: (row(s, t), 0)),
                   pl.BlockSpec((sb, GDN_HEADS, GDN_DK, GDN_DV), lambda s, t: (s, 0, 0, 0)),
                   pl.BlockSpec((sb, SUBLANES, QKV_W), lambda s, t: (s, 0, 0))] + extra_spec,
        out_shape=[jax.ShapeDtypeStruct((n_tok, d), F32),
                   jax.ShapeDtypeStruct((n_seq, GDN_HEADS, GDN_DK, GDN_DV), F32),
                   jax.ShapeDtypeStruct((n_seq, SUBLANES, QKV_W), F32)] + extra_shape,
        scratch_shapes=[pltpu.VMEM((SUBLANES, QKV_W), F32)],
        compiler_params=_params(("arbitrary", "arbitrary")),
        name="gdn",
    )(x, sc, sh, norm1_w, w_proj, w_g, w_gt, p_row, p_col, conv0, state0, conv_w, norm_w)


def _merge_body(x_ref, on_ref, *refs, per_seq, n_tiles, projected):
    n_head = 5 if projected else 4
    (sc0_ref, g1_ref, sc2_ref, sh2_ref, wa_ref, wb_ref, wo_ref, cw_ref, n2_ref, wr_ref, br_ref, cnt0_ref,
     x1_ref, h2_ref, pos_ref, rw_ref, cnt_ref, tail_ref, post_ref, tbl_ref, cnt_scr, hist_scr) = refs[n_head:]
    step = pl.program_id(0) * pl.num_programs(1) + pl.program_id(1)

    @pl.when(step == 0)
    def _():
        cnt_scr[...] = cnt0_ref[...]

    tm, d = on_ref.shape
    if projected:
        sc_b, sc_c, sc_h, gate_a, gate_b = (r[...] for r in refs[:n_head])
    else:
        sc1_ref, sh1_ref, n1_ref, w5_ref = refs[:n_head]
        p5 = jnp.dot(_ada_norm(x_ref, sc1_ref, sh1_ref, n1_ref), w5_ref[...], preferred_element_type=F32)
        sc_b, sc_c, sc_h, gate_a, gate_b = (p5[:, j * d:(j + 1) * d] for j in range(5))
    pre = sc_c * sc_h
    if per_seq:
        hist = sc0_ref[...].reshape(tm, d)
        tail_ref[...] = pre
    else:
        @pl.when(pl.program_id(1) == 0)
        def _():
            hist_scr[...] = sc0_ref[0]

        hist = hist_scr[...]
        hist_scr[...] = pre[tm - SUBLANES:, :]
        tail_ref[...] = pre[tm - SUBLANES:, :]
    y_b = _bdot(sc_b * _causal_conv(pre, hist, cw_ref[...], per_seq), wb_ref[...])
    y_a = _bdot(on_ref[...], wa_ref[...])
    merged = jax.nn.sigmoid(gate_a) * y_a + jax.nn.sigmoid(gate_b) * y_b
    mo = _bdot(merged, wo_ref[...]).reshape(x_ref.shape)
    x1 = x_ref[...] + g1_ref[...] * mo
    y = x1 * lax.rsqrt(jnp.mean(x1 * x1, axis=-1, keepdims=True) + NORM_EPS) * n2_ref[...]
    h2 = (y * (1.0 + sc2_ref[...]) + sh2_ref[...]).reshape(tm, d)
    x1_ref[...] = x1.reshape(tm, d)
    h2_ref[...] = h2

    h2_hi = h2.astype(BF16)
    h2_lo = (h2 - h2_hi.astype(F32)).astype(BF16)
    w_hi = wr_ref[0]
    logits = (jnp.dot(h2_hi, w_hi, preferred_element_type=F32)
              + (jnp.dot(h2_hi, wr_ref[1], preferred_element_type=F32)
                 + jnp.dot(h2_lo, w_hi, preferred_element_type=F32))) + br_ref[...]
    lane = lax.broadcasted_iota(I32, logits.shape, 1)
    lane_f = lane.astype(F32)
    work = logits
    vals, hots = [], []
    member = jnp.zeros(logits.shape, F32)
    for _ in range(TOP_K):
        m = jnp.max(work, axis=-1, keepdims=True)
        sel = jnp.min(jnp.where(work == m, lane_f, float(N_EXPERTS - 1)), axis=-1, keepdims=True)
        hot = lane_f == sel
        vals.append(m)
        hots.append(hot)
        member = member + hot.astype(F32)
        work = jnp.where(hot, -jnp.inf, work)
    exps = [jnp.exp(v - vals[0]) for v in vals]
    denom = exps[0] + exps[1] + exps[2] + exps[3]
    ti = lax.broadcasted_iota(I32, (tm, tm), 0)
    tj = lax.broadcasted_iota(I32, (tm, tm), 1)
    before = (tj < ti).astype(BF16)
    rank_loc = jnp.dot(before, member.astype(BF16), preferred_element_type=F32)
    cnt = jnp.sum(member, axis=0, keepdims=True).astype(I32)
    seg = lax.shift_left(lax.shift_right_logical(cnt + (SUBLANES - 1), 3), 3)
    seg8 = jnp.broadcast_to(seg.astype(F32), (SUBLANES, ROUTER_LANES))
    ei = lax.broadcasted_iota(I32, (ROUTER_LANES, ROUTER_LANES), 0)
    ej = lax.broadcasted_iota(I32, (ROUTER_LANES, ROUTER_LANES), 1)
    base8 = jnp.dot(seg8, (ei < ej).astype(F32), precision=HIGHEST, preferred_element_type=F32)
    row_all = base8[0:1, :] + rank_loc
    pos = jnp.zeros((tm, ROUTER_LANES), F32)
    rw = jnp.zeros((tm, ROUTER_LANES), F32)
    for kk in range(TOP_K):
        pos_k = jnp.sum(jnp.where(hots[kk], row_all, 0.0), axis=-1, keepdims=True)
        pos = jnp.where(lane == kk, pos_k, pos)
        rw = jnp.where(lane == kk, exps[kk] / denom, rw)
    pos_ref[...] = pos
    rw_ref[...] = rw
    pick = (lax.broadcasted_iota(I32, (SUBLANES, ROUTER_LANES), 0)
            == lax.broadcasted_iota(I32, (SUBLANES, ROUTER_LANES), 1)).astype(F32)
    post_ref[...] = lax.dot_general(pick, pos, _NT, precision=HIGHEST, preferred_element_type=F32)
    sub = lax.broadcasted_iota(I32, (SUBLANES, ROUTER_LANES), 0)
    tbl_ref[...] = jnp.where(sub == 0, base8.astype(I32),
                             jnp.where(sub == 1, seg8.astype(I32), cnt_scr[...].astype(I32)))
    cnt_scr[...] = cnt_scr[...] + seg8

    @pl.when(step == n_tiles - 1)
    def _():
        cnt_ref[...] = cnt_scr[...]


def _merge(x, on, p5, sc1, sh1, norm1_w, w5, sc0, g1, sc2, sh2, w_a, w_b, w_o, conv_w, norm2_w, w_r, b_r, cnt0):
    n_seq, seq_len, d = x.shape
    sb, tb = _route_tile(n_seq, seq_len)
    per_seq = sb > 1 or tb == SUBLANES
    if per_seq:
        assert tb == SUBLANES
    tm = sb * tb
    nt = seq_len // tb
    ns = n_seq // sb
    n_tok = n_seq * seq_len
    row = lambda s, t: s * nt + t
    full = lambda shape: pl.BlockSpec(shape, lambda s, t: (0,) * len(shape))
    ada = pl.BlockSpec((sb, 1, d), lambda s, t: (s, 0, 0))
    tok = pl.BlockSpec((tm, d), lambda s, t: (row(s, t), 0))
    lanes = pl.BlockSpec((tm, ROUTER_LANES), lambda s, t: (row(s, t), 0))
    tail_rows = tm if per_seq else SUBLANES
    if p5 is None:
        head_specs = [ada, ada, _resident((1, d)), _resident((d, 5 * d))]
        head_args = [sc1, sh1, norm1_w, w5]
    else:
        head_specs = [pl.BlockSpec((tm, d), lambda s, t, j=j: (row(s, t), j)) for j in range(5)]
        head_args = [p5] * 5
    return pl.pallas_call(
        functools.partial(_merge_body, per_seq=per_seq, n_tiles=ns * nt, projected=p5 is not None),
        grid=(ns, nt),
        in_specs=[pl.BlockSpec((sb, tb, d), lambda s, t: (s, t, 0)),
                  tok] + head_specs + [
                  pl.BlockSpec((sb, SUBLANES, d), lambda s, t: (s, 0, 0)),
                  ada, ada, ada,
                  _resident((d, d)), _resident((d, d)), _resident((d, d)),
                  _resident((SC_CONV, d)), _resident((1, d)), _resident((2, d, ROUTER_LANES)),
                  _resident((1, ROUTER_LANES)), _resident((SUBLANES, ROUTER_LANES))],
        out_specs=[tok, tok, lanes, lanes, full((SUBLANES, ROUTER_LANES)),
                   pl.BlockSpec((tail_rows, d), lambda s, t: (row(s, t), 0)),
                   pl.BlockSpec((SUBLANES, tm), lambda s, t: (0, row(s, t))),
                   pl.BlockSpec((SUBLANES, ROUTER_LANES), lambda s, t: (row(s, t), 0))],
        out_shape=[jax.ShapeDtypeStruct((n_tok, d), F32),
                   jax.ShapeDtypeStruct((n_tok, d), F32),
                   jax.ShapeDtypeStruct((n_tok, ROUTER_LANES), F32),
                   jax.ShapeDtypeStruct((n_tok, ROUTER_LANES), F32),
                   jax.ShapeDtypeStruct((SUBLANES, ROUTER_LANES), F32),
                   jax.ShapeDtypeStruct((ns * nt * tail_rows, d), F32),
                   jax.ShapeDtypeStruct((SUBLANES, n_tok), F32),
                   jax.ShapeDtypeStruct((ns * nt * SUBLANES, ROUTER_LANES), I32)],
        scratch_shapes=[pltpu.VMEM((SUBLANES, ROUTER_LANES), F32), pltpu.VMEM((SUBLANES, d), F32)],
        compiler_params=_params(("arbitrary", "arbitrary")),
        name="merge",
    )(x, on, *head_args, sc0, g1, sc2, sh2, w_a, w_b, w_o, conv_w, norm2_w, w_r, b_r, cnt0)


def _pow2_pieces(length, max_rows):
    out = []
    rows = max_rows
    while rows >= SUBLANES:
        shift = rows.bit_length()
        offset = lax.shift_left(lax.shift_right_logical(length, shift), shift)
        out.append(((length & rows) != 0, offset, rows))
        rows //= 2
    return out


def _segment_copies(tbl_ref, tile_id, local_ref, global_ref, sem, to_global, max_rows):
    out = []
    base = tile_id * (3 * N_EXPERTS)
    for e in range(N_EXPERTS):
        local0 = tbl_ref[base + e]
        length = tbl_ref[base + N_EXPERTS + e]
        global0 = tbl_ref[base + 2 * N_EXPERTS + e]
        for pred, offset, rows in _pow2_pieces(length, max_rows):
            loc = local_ref.at[pl.ds(pl.multiple_of(local0 + offset, SUBLANES), rows)]
            glo = global_ref.at[pl.ds(pl.multiple_of(global0 + offset, SUBLANES), rows)]
            cp = pltpu.make_async_copy(loc, glo, sem) if to_global else pltpu.make_async_copy(glo, loc, sem)
            out.append((pred, cp))
    return out


def _start_all(copies):
    for pred, cp in copies:
        pl.when(pred)(cp.start)


def _wait_all(copies):
    for pred, cp in copies:
        pl.when(pred)(cp.wait)


def _dispatch_body(tbl_ref, tails_ref, post_ref, h2_ref, *refs, tile0, zero_tails):
    buf_ref, sorted_scr, zero_scr, sems = refs[-4:]
    i = pl.program_id(0)
    slot = lax.rem(i, 2)
    tm = h2_ref.shape[0]
    r_rows = sorted_scr.shape[1]

    if zero_tails:
        @pl.when(i == 0)
        def _():
            zero_scr[...] = jnp.zeros(zero_scr.shape, F32)
            z_rows = zero_scr.shape[0]

            def fill(b, carry):
                cp = pltpu.make_async_copy(zero_scr, buf_ref.at[pl.ds(pl.multiple_of(b * SUBLANES, SUBLANES), z_rows)],
                                           sems.at[2])
                cp.start()
                cp.wait()
                return carry

            tails = []
            for e in range(N_EXPERTS):
                start, length = tails_ref[e], tails_ref[N_EXPERTS + e]
                whole = length // z_rows
                lax.fori_loop(0, whole, lambda b, c, s=start: fill(s // SUBLANES + b * (z_rows // SUBLANES), c), 0)
                rest0 = start + whole * z_rows
                for pred, offset, rows in _pow2_pieces(length - whole * z_rows, z_rows // 2):
                    at = pl.multiple_of(rest0 + offset, SUBLANES)
                    tails.append((pred, pltpu.make_async_copy(zero_scr.at[pl.ds(0, rows)],
                                                              buf_ref.at[pl.ds(at, rows)], sems.at[2])))
            _start_all(tails)
            _wait_all(tails)
            lax.fori_loop(tails_ref[2 * N_EXPERTS] // z_rows, buf_ref.shape[0] // z_rows,
                          lambda b, c: fill(b * (z_rows // SUBLANES), c), 0)

    pos = post_ref[...].astype(I32)
    j = lax.broadcasted_iota(I32, (r_rows, tm), 0)
    onehot = jnp.zeros((r_rows, tm), F32)
    for k in range(TOP_K):
        onehot = jnp.where(j == pos[k:k + 1, :], 1.0, onehot)
    sorted_scr[slot] = _bdot(onehot, h2_ref[...])

    _start_all(_segment_copies(tbl_ref, tile0 + i, sorted_scr.at[slot], buf_ref, sems.at[slot], True, tm))

    @pl.when(i > 0)
    def _():
        _wait_all(_segment_copies(tbl_ref, tile0 + i - 1, sorted_scr.at[1 - slot], buf_ref, sems.at[1 - slot],
                                  True, tm))

    @pl.when(i == pl.num_programs(0) - 1)
    def _():
        _wait_all(_segment_copies(tbl_ref, tile0 + i, sorted_scr.at[slot], buf_ref, sems.at[slot], True, tm))


def _dispatch(tbl, tails, post, h2, buf, buf_rows, tm, tile0):
    n_tok, d = h2.shape
    r_rows = TOP_K * tm + N_EXPERTS * SUBLANES
    first = buf is None
    grid_spec = pltpu.PrefetchScalarGridSpec(
        num_scalar_prefetch=2,
        grid=(n_tok // tm,),
        in_specs=[pl.BlockSpec((SUBLANES, tm), lambda i, *_: (0, i)),
                  pl.BlockSpec((tm, d), lambda i, *_: (i, 0))]
                 + ([] if first else [pl.BlockSpec(memory_space=pl.ANY)]),
        out_specs=pl.BlockSpec(memory_space=pl.ANY),
        scratch_shapes=[pltpu.VMEM((2, r_rows, d), F32),
                        pltpu.VMEM((EXPERT_BLOCK // 2, d), F32),
                        pltpu.SemaphoreType.DMA((3,))],
    )
    return pl.pallas_call(
        functools.partial(_dispatch_body, tile0=tile0, zero_tails=first),
        grid_spec=grid_spec,
        out_shape=jax.ShapeDtypeStruct((buf_rows, d), F32),
        input_output_aliases={} if first else {4: 0},
        compiler_params=pltpu.CompilerParams(dimension_semantics=("arbitrary",), has_side_effects=True,
                                             vmem_limit_bytes=VMEM_LIMIT),
        name="dispatch",
    )(tbl, tails, post, h2, *([] if first else [buf]))


def _expert_body(be_ref, nv_ref, x_ref, wg_ref, bg_ref, wu_ref, bu_ref, wd_ref, bd_ref, o_ref,
                 wg_s, wu_s, wd_s):
    i = pl.program_id(0)

    @pl.when(i < nv_ref[0])
    def _():
        @pl.when((i == 0) | (be_ref[i] != be_ref[jnp.maximum(i - 1, 0)]))
        def _():
            wg_s[...] = wg_ref[0].astype(BF16)
            wu_s[...] = wu_ref[0].astype(BF16)
            wd_s[...] = wd_ref[0].astype(BF16)

        x = x_ref[...].astype(BF16)
        gate = jnp.dot(x, wg_s[...], preferred_element_type=F32) + bg_ref[0]
        up = jnp.dot(x, wu_s[...], preferred_element_type=F32) + bu_ref[0]
        gate = jnp.minimum(gate, SWIGLU_LIMIT)
        up = jnp.clip(up, -SWIGLU_LIMIT, SWIGLU_LIMIT)
        glu = gate * jax.nn.sigmoid(SWIGLU_ALPHA * gate)
        o_ref[...] = _bdot((up + 1.0) * glu, wd_s[...]) + bd_ref[0]

    @pl.when(i >= nv_ref[0])
    def _():
        o_ref[...] = jnp.zeros(o_ref.shape, F32)


def _experts(block_e, n_valid, xs, w_gate, b_gate, w_up, b_up, w_down, b_down):
    m_pad, d = xs.shape
    n_blocks = m_pad // EXPERT_BLOCK
    f = w_gate.shape[2]
    blk = lambda i, be, nv: (jnp.maximum(jnp.minimum(i, nv[0] - 1), 0), 0)
    wspec = lambda a, b: pl.BlockSpec((1, a, b), lambda i, be, nv: (be[i], 0, 0))
    grid_spec = pltpu.PrefetchScalarGridSpec(
        num_scalar_prefetch=2,
        grid=(n_blocks,),
        in_specs=[pl.BlockSpec((EXPERT_BLOCK, d), blk),
                  wspec(d, f), wspec(1, f), wspec(d, f), wspec(1, f), wspec(f, d), wspec(1, d)],
        out_specs=pl.BlockSpec((EXPERT_BLOCK, d), lambda i, be, nv: (i, 0)),
        scratch_shapes=[pltpu.VMEM((d, f), BF16), pltpu.VMEM((d, f), BF16), pltpu.VMEM((f, d), BF16)],
    )
    return pl.pallas_call(
        _expert_body,
        grid_spec=grid_spec,
        out_shape=jax.ShapeDtypeStruct((m_pad, d), F32),
        compiler_params=_params(("arbitrary",)),
        name="experts",
    )(block_e, n_valid, xs, w_gate, b_gate[:, None, :], w_up, b_up[:, None, :], w_down, b_down[:, None, :])


def _combine_body(tbl_ref, ys_ref, pos_ref, rw_ref, x1_ref, g2_ref, fw_ref, o_ref, blk_scr, sems, *, tile0):
    step = pl.program_id(0) * pl.num_programs(1) + pl.program_id(1)
    n_steps = pl.num_programs(0) * pl.num_programs(1)
    slot = lax.rem(step, 2)
    tm = pos_ref.shape[0]
    r_rows = blk_scr.shape[1]

    def fetch(tile, into):
        return _segment_copies(tbl_ref, tile0 + tile, blk_scr.at[into], ys_ref, sems.at[into], False, tm)

    @pl.when(step == 0)
    def _():
        blk_scr[...] = jnp.zeros(blk_scr.shape, F32)
        _start_all(fetch(0, 0))

    @pl.when(step + 1 < n_steps)
    def _():
        _start_all(fetch(step + 1, 1 - slot))

    _wait_all(fetch(step, slot))

    pos = pos_ref[...].astype(I32)
    rw = rw_ref[...]
    j = lax.broadcasted_iota(I32, (tm, r_rows), 1)
    pw = jnp.zeros((tm, r_rows), F32)
    for k in range(TOP_K):
        pw = jnp.where(j == pos[:, k:k + 1], rw[:, k:k + 1], pw)
    ffn = _bdot(pw, blk_scr[slot])
    y = x1_ref[...] + g2_ref[...] * ffn.reshape(x1_ref.shape)
    o_ref[...] = y * lax.rsqrt(jnp.mean(y * y, axis=-1, keepdims=True) + NORM_EPS) * fw_ref[...]


def _combine(tbl, ys, pos, rw, x1, g2, final_w, sb, tb, tile0):
    n_seq, seq_len, d = x1.shape
    tm = sb * tb
    nt = seq_len // tb
    r_rows = TOP_K * tm + N_EXPERTS * SUBLANES
    row = lambda s, t: s * nt + t
    grid_spec = pltpu.PrefetchScalarGridSpec(
        num_scalar_prefetch=1,
        grid=(n_seq // sb, nt),
        in_specs=[pl.BlockSpec(memory_space=pl.ANY),
                  pl.BlockSpec((tm, ROUTER_LANES), lambda s, t, *_: (row(s, t), 0)),
                  pl.BlockSpec((tm, ROUTER_LANES), lambda s, t, *_: (row(s, t), 0)),
                  pl.BlockSpec((sb, tb, d), lambda s, t, *_: (s, t, 0)),
                  pl.BlockSpec((sb, 1, d), lambda s, t, *_: (s, 0, 0)),
                  pl.BlockSpec((1, d), lambda s, t, *_: (0, 0))],
        out_specs=pl.BlockSpec((sb, tb, d), lambda s, t, *_: (s, t, 0)),
        scratch_shapes=[pltpu.VMEM((2, r_rows, d), F32), pltpu.SemaphoreType.DMA((2,))],
    )
    return pl.pallas_call(
        functools.partial(_combine_body, tile0=tile0),
        grid_spec=grid_spec,
        out_shape=jax.ShapeDtypeStruct((n_seq, seq_len, d), F32),
        compiler_params=_params(("arbitrary", "arbitrary")),
        name="combine",
    )(tbl, ys, pos, rw, x1, g2, final_w)


def _pad_state(state, rows):
    return jnp.pad(state, ((0, 0), (rows - state.shape[1], 0), (0, 0)))


def kernel(x_prompt, x_sample, c_prompt, c_sample, state_gdn_conv, state_gdn_rec, state_sc_conv, w_ada, b_ada,
           norm1_w, w_in, gdn_conv_w, gdn_a_log, gdn_dt_bias, gdn_norm_w, w_branch_a, sc_conv_w, w_branch_b,
           w_out, norm2_w, w_router, b_router, w_gate, b_gate, w_up, b_up, w_down, b_down, final_norm_w):
    assert w_ada.shape[0] == 1, "single-layer trunk"
    d = D_MODEL
    bp, tp, _ = x_prompt.shape
    bs, ts, _ = x_sample.shape
    n_p, n_s = bp * tp, bs * ts
    n_tok = n_p + n_s

    w_in0 = w_in[0]
    g_lo, g_hi = QKV_W + d, QKV_W + d + N_GATE_COLS
    w_qkvz = w_in0[:, :g_lo].astype(BF16)
    w5 = w_in0[:, g_hi:].astype(BF16)
    w_all = jnp.concatenate([w_qkvz, w5], axis=1)
    w_g = w_in0[:, g_lo:g_hi].astype(BF16)
    w_gt = w_g.T
    zeros_h = jnp.zeros((GDN_HEADS,), F32)
    p_row = jnp.stack([jnp.concatenate([zeros_h, gdn_a_log[0]]), jnp.concatenate([zeros_h, gdn_dt_bias[0]])])
    p_col = p_row.T
    w_a = w_branch_a[0].astype(BF16)
    w_b = w_branch_b[0].astype(BF16)
    w_o = w_out[0].astype(BF16)
    w_r = jnp.pad(w_router[0], ((0, 0), (0, ROUTER_LANES - N_EXPERTS)))
    w_r_hi = w_r.astype(BF16)
    w_r = jnp.stack([w_r_hi, (w_r - w_r_hi.astype(F32)).astype(BF16)])
    b_r = jnp.pad(b_router[0], (0, ROUTER_LANES - N_EXPERTS), constant_values=-jnp.inf).reshape(1, ROUTER_LANES)

    n_c = bp + bs
    c_rows = -(-n_c // 16) * 16
    c_all = jnp.pad(jnp.concatenate([c_prompt, c_sample], axis=0), ((0, c_rows - n_c), (0, 0)))
    ada = _ada(c_all, w_ada[0], b_ada[0])

    def ada_parts(lo, hi):
        return [ada[lo:hi, j * d:(j + 1) * d].reshape(hi - lo, 1, d) for j in range(6)]

    groups = [
        dict(x=x_prompt, ada=ada_parts(0, bp), n_seq=bp, seq_len=tp,
             conv0=jnp.zeros((bp, SUBLANES, QKV_W), F32),
             rec0=jnp.zeros((bp, GDN_HEADS, GDN_DK, GDN_DV), F32),
             sc0=jnp.zeros((bp, SUBLANES, d), F32)),
        dict(x=x_sample, ada=ada_parts(bp, n_c), n_seq=bs, seq_len=ts,
             conv0=_pad_state(state_gdn_conv[0], SUBLANES),
             rec0=state_gdn_rec[0],
             sc0=_pad_state(state_sc_conv[0], SUBLANES)),
    ]

    counts = jnp.zeros((SUBLANES, ROUTER_LANES), F32)
    for g in groups:
        sh1, sc1, g1, sh2, sc2, g2 = g["ada"]
        n_seq, seq_len = g["n_seq"], g["seq_len"]
        wide = seq_len > SUBLANES
        on, rec, conv_tail, *p5 = _gdn(g["x"], sc1, sh1, norm1_w, w_all if wide else w_qkvz, w_g, w_gt, p_row, p_col,
                                       g["conv0"], g["rec0"], gdn_conv_w[0], gdn_norm_w)
        x1, h2, pos, rw, counts, tail, post, tbl = _merge(g["x"], on, p5[0] if wide else None, sc1, sh1, norm1_w, w5,
                                                          g["sc0"], g1, sc2, sh2, w_a, w_b, w_o, sc_conv_w[0],
                                                          norm2_w, w_r, b_r, counts)
        sb, tb = _route_tile(n_seq, seq_len)
        g.update(x1=x1, h2=h2, pos=pos, rw=rw, post=post, tbl=tbl, rec=rec, g2=g2, sb=sb, tb=tb, tm=sb * tb,
                 rows_after=counts[0, :N_EXPERTS].astype(I32),
                 new_conv=conv_tail[:, SUBLANES - (GDN_CONV - 1):, :],
                 new_sc=tail.reshape(n_seq, -1, d)[:, -(SC_CONV - 1):, :])

    rows_e = counts[0, :N_EXPERTS].astype(I32)
    padded = (rows_e + EXPERT_BLOCK - 1) // EXPERT_BLOCK * EXPERT_BLOCK
    pad_end = jnp.cumsum(padded)
    expert0 = (pad_end - padded).astype(I32)
    tbl_all = jnp.concatenate([g["tbl"] for g in groups], axis=0).reshape(-1, SUBLANES, ROUTER_LANES)
    n_tiles = tbl_all.shape[0]
    tbl_all = tbl_all[:, :3, :N_EXPERTS].at[:, 2, :].add(expert0[None, :]).reshape(-1)
    rows_first = groups[0]["rows_after"]
    tails = jnp.concatenate([expert0 + rows_first, padded - rows_first, pad_end[-1:]]).astype(I32)
    max_rows = n_tok * TOP_K + n_tiles * N_EXPERTS * (SUBLANES - 1)
    n_blocks = -(-max_rows // EXPERT_BLOCK) + N_EXPERTS
    block_start = jnp.arange(n_blocks, dtype=I32) * EXPERT_BLOCK
    block_e = jnp.minimum(jnp.sum((pad_end[None, :] <= block_start[:, None]).astype(I32), axis=1), N_EXPERTS - 1)
    n_valid = (pad_end[-1:] // EXPERT_BLOCK).astype(I32)

    buf = None
    tile0 = 0
    for g in groups:
        g["tile0"] = tile0
        buf = _dispatch(tbl_all, tails, g["post"], g["h2"], buf, n_blocks * EXPERT_BLOCK, g["tm"], tile0)
        tile0 += g["h2"].shape[0] // g["tm"]
    ys = _experts(block_e, n_valid, buf, w_gate[0], b_gate[0], w_up[0], b_up[0], w_down[0], b_down[0])
    outs = []
    for g in groups:
        n_seq, seq_len = g["n_seq"], g["seq_len"]
        x1 = g["x1"].reshape(n_seq, seq_len, d)
        outs.append(_combine(tbl_all, ys, g["pos"], g["rw"], x1, g["g2"], final_norm_w.reshape(1, d),
                             g["sb"], g["tb"], g["tile0"]))

    gp, gs = groups
    return (outs[0], outs[1], gp["new_conv"][None], gp["rec"][None], gp["new_sc"][None],
            gs["new_conv"][None], gs["rec"][None], gs["new_sc"][None])
```

```python
import functools

import jax
import jax.numpy as jnp
from jax import lax
from jax.experimental import pallas as pl
from jax.experimental.pallas import tpu as pltpu

F32 = jnp.float32
BF16 = jnp.bfloat16
I32 = jnp.int32
U32 = jnp.uint32
HIGHEST = lax.Precision.HIGHEST

D_MODEL = 1024
GDN_HEADS = 8
GDN_DK = 128
GDN_DV = 128
GDN_QK = GDN_HEADS * GDN_DK
QKV_W = 3 * GDN_QK
GDN_CONV = 4
GDN_CHUNK = 64
GDN_STEP_CHUNKS = 4
GDN_STEP_SEQS = 8
GDN_WAVE_CHUNKS = 8
SC_CONV = 3
N_EXPERTS = 32
TOP_K = 4
SWIGLU_LIMIT = 7.0
SWIGLU_ALPHA = 1.702
NORM_EPS = 1e-6
N_GATE_COLS = 2 * GDN_HEADS
ROUTER_LANES = 128
EXPERT_BLOCK = 512
SUBLANES = 8
VMEM_LIMIT = 56 * 1024 * 1024

_NT = (((1,), (1,)), ((), ()))
_TN = (((0,), (0,)), ((), ()))


def _bdot(a, b):
    return jnp.dot(a.astype(BF16), b.astype(BF16), preferred_element_type=F32)


def _bdot_nt(a, b):
    return lax.dot_general(a.astype(BF16), b.astype(BF16), _NT, preferred_element_type=F32)


def _bdot_tn(a, b):
    return lax.dot_general(a.astype(BF16), b.astype(BF16), _TN, preferred_element_type=F32)


def _pack_halves(x):
    w = x.shape[1] // 2
    lo = lax.bitcast_convert_type(x[:, :w], U32)
    hi = lax.bitcast_convert_type(x[:, w:], U32)
    return lax.shift_right_logical(lo, jnp.uint32(16)) | (hi & jnp.uint32(0xFFFF0000))


def _unpack_halves(p):
    lo = lax.bitcast_convert_type(lax.shift_left(p, jnp.uint32(16)), F32)
    hi = lax.bitcast_convert_type(p & jnp.uint32(0xFFFF0000), F32)
    return jnp.concatenate([lo, hi], axis=1).astype(BF16)


def _silu(x):
    return x * jax.nn.sigmoid(x)


def _softplus(x):
    return jnp.maximum(x, 0.0) + jnp.log1p(jnp.exp(-jnp.abs(x)))


def _seq_tile(n_seq, seq_len, target):
    if seq_len >= target:
        assert seq_len % target == 0
        return 1, target
    sb = min(n_seq, target // seq_len)
    assert n_seq % sb == 0 and seq_len % SUBLANES == 0
    return sb, seq_len


def _route_tile(n_seq, seq_len):
    return _seq_tile(n_seq, seq_len, 512)


def _params(sem):
    return pltpu.CompilerParams(dimension_semantics=sem, vmem_limit_bytes=VMEM_LIMIT)


def _ada_body(c_ref, w_ref, b_ref, o_ref):
    o_ref[...] = _bdot(_silu(c_ref[...]), w_ref[...]) + b_ref[...]


def _ada(c, w_ada, b_ada):
    rows, d = c.shape
    n = w_ada.shape[1]
    tn = 1024
    return pl.pallas_call(
        _ada_body,
        grid=(n // tn,),
        in_specs=[pl.BlockSpec((rows, d), lambda j: (0, 0)),
                  pl.BlockSpec((d, tn), lambda j: (0, j)),
                  pl.BlockSpec((1, tn), lambda j: (0, j))],
        out_specs=pl.BlockSpec((rows, tn), lambda j: (0, j)),
        out_shape=jax.ShapeDtypeStruct((rows, n), F32),
        compiler_params=_params(("arbitrary",)),
        name="ada",
    )(c, w_ada, b_ada.reshape(1, n))


def _gates(v, a_log, dt_bias, axis):
    is_beta = lax.broadcasted_iota(I32, v.shape, axis) < GDN_HEADS
    beta = jax.nn.sigmoid(v)
    g = -jnp.exp(a_log) * _softplus(v + dt_bias)
    return jnp.where(is_beta, beta, g)


def _ada_norm(x_ref, sc_ref, sh_ref, nw_ref):
    x = x_ref[...]
    y = x * lax.rsqrt(jnp.mean(x * x, axis=-1, keepdims=True) + NORM_EPS) * nw_ref[...]
    h = y * (1.0 + sc_ref[...]) + sh_ref[...]
    return h.reshape(x.shape[0] * x.shape[1], x.shape[2]).astype(BF16)


def _resident(shape):
    return pl.BlockSpec(shape, lambda *_: (0,) * len(shape), pipeline_mode=pl.Buffered(1))


def _shift_rows(x, hist, s, per_seq):
    rows, width = x.shape
    x3 = x.reshape(rows // SUBLANES, SUBLANES, width)
    xr = pltpu.roll(x3, s, 1)
    if per_seq:
        src = pltpu.roll(hist.reshape(x3.shape), s, 1)
    else:
        hr = pltpu.roll(hist.reshape(1, SUBLANES, width), s, 1)
        src = hr if rows == SUBLANES else jnp.concatenate([hr, xr[:-1]], axis=0)
    sub = lax.broadcasted_iota(I32, x3.shape, 1)
    return jnp.where(sub < s, src, xr).reshape(rows, width)


def _causal_conv(x, hist, w, per_seq):
    taps = w.shape[0]
    acc = x * w[taps - 1:taps, :]
    for s in range(1, taps):
        acc = acc + _shift_rows(x, hist, s, per_seq) * w[taps - 1 - s:taps - s, :]
    return acc


def _gdn_body(x_ref, sc_ref, sh_ref, n1_ref, wq_ref, wg_ref, wgt_ref, prow_ref, pcol_ref, conv0_ref, s0_ref,
              cw_ref, nw_ref, o_ref, s_ref, conv_ref, *rest, chunk, n_chunks):
    stacked = s_ref.shape[0] > 1
    hist_scr = rest[-1]

    @pl.when(pl.program_id(1) == 0)
    def _():
        s_ref[...] = s0_ref[...]
        if not stacked:
            hist_scr[...] = conv0_ref[0]

    h_in = _ada_norm(x_ref, sc_ref, sh_ref, n1_ref)
    n_own = QKV_W + D_MODEL
    proj = jnp.dot(h_in, wq_ref[:, :n_own], preferred_element_type=F32)
    x = proj[:, :QKV_W]
    z = proj[:, QKV_W:]
    if len(rest) == 2:
        rest[0][...] = jnp.dot(h_in, wq_ref[:, n_own:], preferred_element_type=F32)
    if stacked:
        hist = conv0_ref[...].reshape(x.shape)
        conv_ref[...] = x.reshape(conv_ref.shape)
    else:
        hist = hist_scr[...]
        hist_scr[...] = x[x.shape[0] - SUBLANES:, :]
        conv_ref[0] = x[x.shape[0] - SUBLANES:, :]
    qkvc = _silu(_causal_conv(x, hist, cw_ref[...], stacked))

    gcol = _gates(jnp.dot(h_in, wg_ref[...], preferred_element_type=F32), prow_ref[0:1, :], prow_ref[1:2, :], 1)
    ri = lax.broadcasted_iota(I32, (chunk, chunk), 0)
    ci = lax.broadcasted_iota(I32, (chunk, chunk), 1)
    causal = ri >= ci
    strict = ri > ci
    tri = causal.astype(F32)
    eye = (ri == ci).astype(F32)
    n_lvl = chunk.bit_length() - 1
    blk = [lax.shift_right_logical(ri, l) == lax.shift_right_logical(ci, l) for l in range(1, n_lvl + 1)]
    pair = [blk[l] & jnp.logical_not(blk[l - 1]) for l in range(1, n_lvl)]

    units = [(c, h) for c in range(n_chunks) for h in range(GDN_HEADS)]
    every = range(len(units))
    q, k, dcol, dlast, edec, qk, sol = {}, {}, {}, {}, {}, {}, {}
    for c0 in range(0, n_chunks, GDN_WAVE_CHUNKS):
        wave = [c * GDN_HEADS + h for c in range(c0, min(c0 + GDN_WAVE_CHUNKS, n_chunks)) for h in range(GDN_HEADS)]
        v, beta, decay = {}, {}, {}
        for c in range(c0, min(c0 + GDN_WAVE_CHUNKS, n_chunks)):
            rows = slice(c * chunk, (c + 1) * chunk)
            dec_col = jnp.dot(tri, gcol[rows, GDN_HEADS:], precision=HIGHEST, preferred_element_type=F32)
            grow = _gates(lax.dot_general(wgt_ref[...], h_in[rows, :], _NT, preferred_element_type=F32),
                          pcol_ref[:, 0:1], pcol_ref[:, 1:2], 0)
            dec_row = lax.dot_general(grow[GDN_HEADS:, :], tri, _NT, precision=HIGHEST, preferred_element_type=F32)
            for h in range(GDN_HEADS):
                i = c * GDN_HEADS + h
                qh = qkvc[rows, h * GDN_DK:(h + 1) * GDN_DK]
                kh = qkvc[rows, GDN_QK + h * GDN_DK:GDN_QK + (h + 1) * GDN_DK]
                q[i] = qh * (lax.rsqrt(jnp.sum(qh * qh, axis=-1, keepdims=True) + 1e-6) * (GDN_DK ** -0.5))
                k[i] = kh * lax.rsqrt(jnp.sum(kh * kh, axis=-1, keepdims=True) + 1e-6)
                v[i] = qkvc[rows, 2 * GDN_QK + h * GDN_DV:2 * GDN_QK + (h + 1) * GDN_DV]
                beta[i] = gcol[rows, h:h + 1]
                dcol[i] = dec_col[:, h:h + 1]
                dlast[i] = dec_col[chunk - 1:chunk, h:h + 1]
                decay[i] = jnp.where(causal, jnp.exp(dcol[i] - dec_row[h:h + 1, :]), 0.0)
                edec[i] = jnp.exp(dcol[i])
        kk = {i: _bdot_nt(k[i], k[i]) for i in wave}
        for i in wave:
            qk[i] = _bdot_nt(q[i], k[i]) * decay[i]
        a = {i: jnp.where(strict, beta[i] * kk[i] * decay[i], 0.0) for i in wave}
        inv = {i: eye - jnp.where(blk[0], a[i], 0.0) for i in wave}
        for lower_left in pair:
            right = {i: _bdot(jnp.where(lower_left, a[i], 0.0), inv[i]) for i in wave}
            inv = {i: inv[i] - _bdot(inv[i], right[i]) for i in wave}
        for i in wave:
            sol[i] = _bdot(inv[i], jnp.concatenate([beta[i] * v[i], (beta[i] * edec[i]) * k[i]], axis=1))
    seq_of = (lambda c: c) if stacked else (lambda c: 0)
    waves = [list(every)] if stacked else [[c * GDN_HEADS + h for h in range(GDN_HEADS)] for c in range(n_chunks)]
    state = {(seq_of(c), h): s_ref[seq_of(c), h] for c, h in units}
    for wave in waves:
        key = {i: (seq_of(units[i][0]), units[i][1]) for i in wave}
        ws = {i: _bdot(jnp.concatenate([sol[i][:, GDN_DV:], q[i] * edec[i]], axis=0), state[key[i]]) for i in wave}
        u = {i: sol[i][:, :GDN_DV] - ws[i][:chunk] for i in wave}
        o = {i: ws[i][chunk:] + _bdot(qk[i], u[i]) for i in wave}
        upd = {i: _bdot_tn(k[i] * jnp.exp(dlast[i] - dcol[i]), u[i]) for i in wave}
        for i in wave:
            c, h = units[i]
            state[key[i]] = state[key[i]] * jnp.exp(dlast[i]) + upd[i]
            rows = slice(c * chunk, (c + 1) * chunk)
            on = o[i] * lax.rsqrt(jnp.mean(o[i] * o[i], axis=-1, keepdims=True) + NORM_EPS) * nw_ref[...]
            o_ref[rows, h * GDN_DV:(h + 1) * GDN_DV] = on * _silu(z[rows, h * GDN_DV:(h + 1) * GDN_DV])
    for (b, h), value in state.items():
        s_ref[b, h] = value


def _gdn(x, sc, sh, norm1_w, w_proj, w_g, w_gt, p_row, p_col, conv0, state0, conv_w, norm_w):
    n_seq, seq_len, d = x.shape
    chunk = min(GDN_CHUNK, seq_len)
    assert seq_len % chunk == 0 and chunk % SUBLANES == 0
    n_tok = n_seq * seq_len
    stacked = seq_len == SUBLANES and n_seq % GDN_STEP_SEQS == 0
    if stacked:
        n_step_chunks, sb, nt = GDN_STEP_SEQS, GDN_STEP_SEQS, 1
    else:
        n_step_chunks = GDN_STEP_CHUNKS if (seq_len // chunk) % GDN_STEP_CHUNKS == 0 else 1
        sb, nt = 1, seq_len // (n_step_chunks * chunk)
    step_rows = n_step_chunks * chunk
    tb = step_rows // sb
    row = lambda s, t: s * nt + t
    ada = pl.BlockSpec((sb, 1, d), lambda s, t: (s, 0, 0))
    extra = w_proj.shape[1] - (QKV_W + d)
    extra_spec = [pl.BlockSpec((step_rows, extra), lambda s, t: (row(s, t), 0))] if extra else []
    extra_shape = [jax.ShapeDtypeStruct((n_tok, extra), F32)] if extra else []
    return pl.pallas_call(
        functools.partial(_gdn_body, chunk=chunk, n_chunks=n_step_chunks),
        grid=(n_seq // sb, nt),
        in_specs=[pl.BlockSpec((sb, tb, d), lambda s, t: (s, t, 0)),
                  ada, ada, _resident((1, d)),
                  _resident(w_proj.shape), _resident((d, N_GATE_COLS)), _resident((N_GATE_COLS, d)),
                  _resident((2, N_GATE_COLS)), _resident((N_GATE_COLS, 2)),
                  pl.BlockSpec((sb, SUBLANES, QKV_W), lambda s, t: (s, 0, 0)),
                  pl.BlockSpec((sb, GDN_HEADS, GDN_DK, GDN_DV), lambda s, t: (s, 0, 0, 0)),
                  _resident((GDN_CONV, QKV_W)), _resident((1, GDN_DV))],
        out_specs=[pl.BlockSpec((step_rows, d), lambda s, t: (row(s, t), 0)),
                   pl.BlockSpec((sb, GDN_HEADS, GDN_DK, GDN_DV), lambda s, t: (s, 0, 0, 0)),
                   pl.BlockSpec((sb, SUBLANES, QKV_W), lambda s, t: (s, 0, 0))] + extra_spec,
        out_shape=[jax.ShapeDtypeStruct((n_tok, d), F32),
                   jax.ShapeDtypeStruct((n_seq, GDN_HEADS, GDN_DK, GDN_DV), F32),
                   jax.ShapeDtypeStruct((n_seq, SUBLANES, QKV_W), F32)] + extra_shape,
        scratch_shapes=[pltpu.VMEM((SUBLANES, QKV_W), F32)],
        compiler_params=_params(("arbitrary", "arbitrary")),
        name="gdn",
    )(x, sc, sh, norm1_w, w_proj, w_g, w_gt, p_row, p_col, conv0, state0, conv_w, norm_w)


def _merge_body(x_ref, on_ref, *refs, per_seq, n_tiles, projected):
    n_head = 5 if projected else 4
    (sc0_ref, g1_ref, sc2_ref, sh2_ref, wa_ref, wb_ref, wo_ref, cw_ref, n2_ref, wr_ref, br_ref, cnt0_ref,
     x1_ref, h2_ref, pos_ref, rw_ref, cnt_ref, tail_ref, post_ref, tbl_ref, cnt_scr, hist_scr) = refs[n_head:]
    step = pl.program_id(0) * pl.num_programs(1) + pl.program_id(1)

    @pl.when(step == 0)
    def _():
        cnt_scr[...] = cnt0_ref[...]

    tm, d = on_ref.shape
    if projected:
        sc_b, sc_c, sc_h, gate_a, gate_b = (r[...] for r in refs[:n_head])
    else:
        sc1_ref, sh1_ref, n1_ref, w5_ref = refs[:n_head]
        p5 = jnp.dot(_ada_norm(x_ref, sc1_ref, sh1_ref, n1_ref), w5_ref[...], preferred_element_type=F32)
        sc_b, sc_c, sc_h, gate_a, gate_b = (p5[:, j * d:(j + 1) * d] for j in range(5))
    pre = sc_c * sc_h
    if per_seq:
        hist = sc0_ref[...].reshape(tm, d)
        tail_ref[...] = pre
    else:
        @pl.when(pl.program_id(1) == 0)
        def _():
            hist_scr[...] = sc0_ref[0]

        hist = hist_scr[...]
        hist_scr[...] = pre[tm - SUBLANES:, :]
        tail_ref[...] = pre[tm - SUBLANES:, :]
    y_b = _bdot(sc_b * _causal_conv(pre, hist, cw_ref[...], per_seq), wb_ref[...])
    y_a = _bdot(on_ref[...], wa_ref[...])
    merged = jax.nn.sigmoid(gate_a) * y_a + jax.nn.sigmoid(gate_b) * y_b
    mo = _bdot(merged, wo_ref[...]).reshape(x_ref.shape)
    x1 = x_ref[...] + g1_ref[...] * mo
    y = x1 * lax.rsqrt(jnp.mean(x1 * x1, axis=-1, keepdims=True) + NORM_EPS) * n2_ref[...]
    h2 = (y * (1.0 + sc2_ref[...]) + sh2_ref[...]).reshape(tm, d)
    x1_ref[...] = x1.reshape(tm, d)
    h2_ref[...] = h2

    h2_hi = h2.astype(BF16)
    h2_lo = (h2 - h2_hi.astype(F32)).astype(BF16)
    w_hi = wr_ref[0]
    logits = (jnp.dot(h2_hi, w_hi, preferred_element_type=F32)
              + (jnp.dot(h2_hi, wr_ref[1], preferred_element_type=F32)
                 + jnp.dot(h2_lo, w_hi, preferred_element_type=F32))) + br_ref[...]
    lane = lax.broadcasted_iota(I32, logits.shape, 1)
    lane_f = lane.astype(F32)
    work = logits
    vals, hots = [], []
    member = jnp.zeros(logits.shape, F32)
    for _ in range(TOP_K):
        m = jnp.max(work, axis=-1, keepdims=True)
        sel = jnp.min(jnp.where(work == m, lane_f, float(N_EXPERTS - 1)), axis=-1, keepdims=True)
        hot = lane_f == sel
        vals.append(m)
        hots.append(hot)
        member = member + hot.astype(F32)
        work = jnp.where(hot, -jnp.inf, work)
    exps = [jnp.exp(v - vals[0]) for v in vals]
    denom = exps[0] + exps[1] + exps[2] + exps[3]
    ti = lax.broadcasted_iota(I32, (tm, tm), 0)
    tj = lax.broadcasted_iota(I32, (tm, tm), 1)
    before = (tj < ti).astype(BF16)
    rank_loc = jnp.dot(before, member.astype(BF16), preferred_element_type=F32)
    cnt = jnp.sum(member, axis=0, keepdims=True).astype(I32)
    seg = lax.shift_left(lax.shift_right_logical(cnt + (SUBLANES - 1), 3), 3)
    seg8 = jnp.broadcast_to(seg.astype(F32), (SUBLANES, ROUTER_LANES))
    ei = lax.broadcasted_iota(I32, (ROUTER_LANES, ROUTER_LANES), 0)
    ej = lax.broadcasted_iota(I32, (ROUTER_LANES, ROUTER_LANES), 1)
    base8 = jnp.dot(seg8, (ei < ej).astype(F32), precision=HIGHEST, preferred_element_type=F32)
    row_all = base8[0:1, :] + rank_loc
    pos = jnp.zeros((tm, ROUTER_LANES), F32)
    rw = jnp.zeros((tm, ROUTER_LANES), F32)
    for kk in range(TOP_K):
        pos_k = jnp.sum(jnp.where(hots[kk], row_all, 0.0), axis=-1, keepdims=True)
        pos = jnp.where(lane == kk, pos_k, pos)
        rw = jnp.where(lane == kk, exps[kk] / denom, rw)
    pos_ref[...] = pos
    rw_ref[...] = rw
    pick = (lax.broadcasted_iota(I32, (SUBLANES, ROUTER_LANES), 0)
            == lax.broadcasted_iota(I32, (SUBLANES, ROUTER_LANES), 1)).astype(F32)
    post_ref[...] = lax.dot_general(pick, pos, _NT, precision=HIGHEST, preferred_element_type=F32)
    sub = lax.broadcasted_iota(I32, (SUBLANES, ROUTER_LANES), 0)
    tbl_ref[...] = jnp.where(sub == 0, base8.astype(I32),
                             jnp.where(sub == 1, seg8.astype(I32), cnt_scr[...].astype(I32)))
    cnt_scr[...] = cnt_scr[...] + seg8

    @pl.when(step == n_tiles - 1)
    def _():
        cnt_ref[...] = cnt_scr[...]


def _merge(x, on, p5, sc1, sh1, norm1_w, w5, sc0, g1, sc2, sh2, w_a, w_b, w_o, conv_w, norm2_w, w_r, b_r, cnt0):
    n_seq, seq_len, d = x.shape
    sb, tb = _route_tile(n_seq, seq_len)
    per_seq = sb > 1 or tb == SUBLANES
    if per_seq:
        assert tb == SUBLANES
    tm = sb * tb
    nt = seq_len // tb
    ns = n_seq // sb
    n_tok = n_seq * seq_len
    row = lambda s, t: s * nt + t
    full = lambda shape: pl.BlockSpec(shape, lambda s, t: (0,) * len(shape))
    ada = pl.BlockSpec((sb, 1, d), lambda s, t: (s, 0, 0))
    tok = pl.BlockSpec((tm, d), lambda s, t: (row(s, t), 0))
    lanes = pl.BlockSpec((tm, ROUTER_LANES), lambda s, t: (row(s, t), 0))
    tail_rows = tm if per_seq else SUBLANES
    if p5 is None:
        head_specs = [ada, ada, _resident((1, d)), _resident((d, 5 * d))]
        head_args = [sc1, sh1, norm1_w, w5]
    else:
        head_specs = [pl.BlockSpec((tm, d), lambda s, t, j=j: (row(s, t), j)) for j in range(5)]
        head_args = [p5] * 5
    return pl.pallas_call(
        functools.partial(_merge_body, per_seq=per_seq, n_tiles=ns * nt, projected=p5 is not None),
        grid=(ns, nt),
        in_specs=[pl.BlockSpec((sb, tb, d), lambda s, t: (s, t, 0)),
                  tok] + head_specs + [
                  pl.BlockSpec((sb, SUBLANES, d), lambda s, t: (s, 0, 0)),
                  ada, ada, ada,
                  _resident((d, d)), _resident((d, d)), _resident((d, d)),
                  _resident((SC_CONV, d)), _resident((1, d)), _resident((2, d, ROUTER_LANES)),
                  _resident((1, ROUTER_LANES)), _resident((SUBLANES, ROUTER_LANES))],
        out_specs=[tok, tok, lanes, lanes, full((SUBLANES, ROUTER_LANES)),
                   pl.BlockSpec((tail_rows, d), lambda s, t: (row(s, t), 0)),
                   pl.BlockSpec((SUBLANES, tm), lambda s, t: (0, row(s, t))),
                   pl.BlockSpec((SUBLANES, ROUTER_LANES), lambda s, t: (row(s, t), 0))],
        out_shape=[jax.ShapeDtypeStruct((n_tok, d), F32),
                   jax.ShapeDtypeStruct((n_tok, d), F32),
                   jax.ShapeDtypeStruct((n_tok, ROUTER_LANES), F32),
                   jax.ShapeDtypeStruct((n_tok, ROUTER_LANES), F32),
                   jax.ShapeDtypeStruct((SUBLANES, ROUTER_LANES), F32),
                   jax.ShapeDtypeStruct((ns * nt * tail_rows, d), F32),
                   jax.ShapeDtypeStruct((SUBLANES, n_tok), F32),
                   jax.ShapeDtypeStruct((ns * nt * SUBLANES, ROUTER_LANES), I32)],
        scratch_shapes=[pltpu.VMEM((SUBLANES, ROUTER_LANES), F32), pltpu.VMEM((SUBLANES, d), F32)],
        compiler_params=_params(("arbitrary", "arbitrary")),
        name="merge",
    )(x, on, *head_args, sc0, g1, sc2, sh2, w_a, w_b, w_o, conv_w, norm2_w, w_r, b_r, cnt0)


def _pow2_pieces(length, max_rows):
    out = []
    rows = max_rows
    while rows >= SUBLANES:
        shift = rows.bit_length()
        offset = lax.shift_left(lax.shift_right_logical(length, shift), shift)
        out.append(((length & rows) != 0, offset, rows))
        rows //= 2
    return out


def _segment_copies(tbl_ref, tile_id, local_ref, global_ref, sem, to_global, max_rows):
    out = []
    base = tile_id * (3 * N_EXPERTS)
    for e in range(N_EXPERTS):
        local0 = tbl_ref[base + e]
        length = tbl_ref[base + N_EXPERTS + e]
        global0 = tbl_ref[base + 2 * N_EXPERTS + e]
        for pred, offset, rows in _pow2_pieces(length, max_rows):
            loc = local_ref.at[pl.ds(pl.multiple_of(local0 + offset, SUBLANES), rows)]
            glo = global_ref.at[pl.ds(pl.multiple_of(global0 + offset, SUBLANES), rows)]
            cp = pltpu.make_async_copy(loc, glo, sem) if to_global else pltpu.make_async_copy(glo, loc, sem)
            out.append((pred, cp))
    return out


def _start_all(copies):
    for pred, cp in copies:
        pl.when(pred)(cp.start)


def _wait_all(copies):
    for pred, cp in copies:
        pl.when(pred)(cp.wait)


def _dispatch_body(tbl_ref, tails_ref, post_ref, h2_ref, *refs, tile0, zero_tails):
    buf_ref, sorted_scr, zero_scr, sems = refs[-4:]
    i = pl.program_id(0)
    slot = lax.rem(i, 2)
    tm = h2_ref.shape[0]
    r_rows = sorted_scr.shape[1]

    if zero_tails:
        @pl.when(i == 0)
        def _():
            zero_scr[...] = jnp.zeros(zero_scr.shape, U32)
            z_rows = zero_scr.shape[0]

            def fill(b, carry):
                cp = pltpu.make_async_copy(zero_scr, buf_ref.at[pl.ds(pl.multiple_of(b * SUBLANES, SUBLANES), z_rows)],
                                           sems.at[2])
                cp.start()
                cp.wait()
                return carry

            tails = []
            for e in range(N_EXPERTS):
                start, length = tails_ref[e], tails_ref[N_EXPERTS + e]
                whole = length // z_rows
                lax.fori_loop(0, whole, lambda b, c, s=start: fill(s // SUBLANES + b * (z_rows // SUBLANES), c), 0)
                rest0 = start + whole * z_rows
                for pred, offset, rows in _pow2_pieces(length - whole * z_rows, z_rows // 2):
                    at = pl.multiple_of(rest0 + offset, SUBLANES)
                    tails.append((pred, pltpu.make_async_copy(zero_scr.at[pl.ds(0, rows)],
                                                              buf_ref.at[pl.ds(at, rows)], sems.at[2])))
            _start_all(tails)
            _wait_all(tails)
            lax.fori_loop(tails_ref[2 * N_EXPERTS] // z_rows, buf_ref.shape[0] // z_rows,
                          lambda b, c: fill(b * (z_rows // SUBLANES), c), 0)

    pos = post_ref[...].astype(I32)
    j = lax.broadcasted_iota(I32, (r_rows, tm), 0)
    onehot = jnp.zeros((r_rows, tm), F32)
    for k in range(TOP_K):
        onehot = jnp.where(j == pos[k:k + 1, :], 1.0, onehot)
    sorted_scr[slot] = _pack_halves(_bdot(onehot, h2_ref[...]))

    _start_all(_segment_copies(tbl_ref, tile0 + i, sorted_scr.at[slot], buf_ref, sems.at[slot], True, tm))

    @pl.when(i > 0)
    def _():
        _wait_all(_segment_copies(tbl_ref, tile0 + i - 1, sorted_scr.at[1 - slot], buf_ref, sems.at[1 - slot],
                                  True, tm))

    @pl.when(i == pl.num_programs(0) - 1)
    def _():
        _wait_all(_segment_copies(tbl_ref, tile0 + i, sorted_scr.at[slot], buf_ref, sems.at[slot], True, tm))


def _dispatch(tbl, tails, post, h2, buf, buf_rows, tm, tile0):
    n_tok, d = h2.shape
    r_rows = TOP_K * tm + N_EXPERTS * SUBLANES
    first = buf is None
    grid_spec = pltpu.PrefetchScalarGridSpec(
        num_scalar_prefetch=2,
        grid=(n_tok // tm,),
        in_specs=[pl.BlockSpec((SUBLANES, tm), lambda i, *_: (0, i)),
                  pl.BlockSpec((tm, d), lambda i, *_: (i, 0))]
                 + ([] if first else [pl.BlockSpec(memory_space=pl.ANY)]),
        out_specs=pl.BlockSpec(memory_space=pl.ANY),
        scratch_shapes=[pltpu.VMEM((2, r_rows, d // 2), U32),
                        pltpu.VMEM((EXPERT_BLOCK // 2, d // 2), U32),
                        pltpu.SemaphoreType.DMA((3,))],
    )
    return pl.pallas_call(
        functools.partial(_dispatch_body, tile0=tile0, zero_tails=first),
        grid_spec=grid_spec,
        out_shape=jax.ShapeDtypeStruct((buf_rows, d // 2), U32),
        input_output_aliases={} if first else {4: 0},
        compiler_params=pltpu.CompilerParams(dimension_semantics=("arbitrary",), has_side_effects=True,
                                             vmem_limit_bytes=VMEM_LIMIT),
        name="dispatch",
    )(tbl, tails, post, h2, *([] if first else [buf]))


def _expert_body(be_ref, nv_ref, x_ref, wg_ref, bg_ref, wu_ref, bu_ref, wd_ref, bd_ref, o_ref,
                 wg_s, wu_s, wd_s):
    i = pl.program_id(0)

    @pl.when(i < nv_ref[0])
    def _():
        @pl.when((i == 0) | (be_ref[i] != be_ref[jnp.maximum(i - 1, 0)]))
        def _():
            wg_s[...] = wg_ref[0].astype(BF16)
            wu_s[...] = wu_ref[0].astype(BF16)
            wd_s[...] = wd_ref[0].astype(BF16)

        x = _unpack_halves(x_ref[...])
        gate = jnp.dot(x, wg_s[...], preferred_element_type=F32) + bg_ref[0]
        up = jnp.dot(x, wu_s[...], preferred_element_type=F32) + bu_ref[0]
        gate = jnp.minimum(gate, SWIGLU_LIMIT)
        up = jnp.clip(up, -SWIGLU_LIMIT, SWIGLU_LIMIT)
        glu = gate * jax.nn.sigmoid(SWIGLU_ALPHA * gate)
        out = _bdot((up + 1.0) * glu, wd_s[...]) + bd_ref[0]
        o_ref[...] = _pack_halves(out.astype(BF16).astype(F32))

    @pl.when(i >= nv_ref[0])
    def _():
        o_ref[...] = jnp.zeros(o_ref.shape, U32)


def _experts(block_e, n_valid, xs, w_gate, b_gate, w_up, b_up, w_down, b_down):
    m_pad, d_half = xs.shape
    d = 2 * d_half
    n_blocks = m_pad // EXPERT_BLOCK
    f = w_gate.shape[2]
    blk = lambda i, be, nv: (jnp.maximum(jnp.minimum(i, nv[0] - 1), 0), 0)
    wspec = lambda a, b: pl.BlockSpec((1, a, b), lambda i, be, nv: (be[i], 0, 0))
    grid_spec = pltpu.PrefetchScalarGridSpec(
        num_scalar_prefetch=2,
        grid=(n_blocks,),
        in_specs=[pl.BlockSpec((EXPERT_BLOCK, d_half), blk),
                  wspec(d, f), wspec(1, f), wspec(d, f), wspec(1, f), wspec(f, d), wspec(1, d)],
        out_specs=pl.BlockSpec((EXPERT_BLOCK, d_half), lambda i, be, nv: (i, 0)),
        scratch_shapes=[pltpu.VMEM((d, f), BF16), pltpu.VMEM((d, f), BF16), pltpu.VMEM((f, d), BF16)],
    )
    return pl.pallas_call(
        _expert_body,
        grid_spec=grid_spec,
        out_shape=jax.ShapeDtypeStruct((m_pad, d_half), U32),
        compiler_params=_params(("arbitrary",)),
        name="experts",
    )(block_e, n_valid, xs, w_gate, b_gate[:, None, :], w_up, b_up[:, None, :], w_down, b_down[:, None, :])


def _combine_body(tbl_ref, ys_ref, pos_ref, rw_ref, x1_ref, g2_ref, fw_ref, o_ref, blk_scr, sems, *, tile0):
    step = pl.program_id(0) * pl.num_programs(1) + pl.program_id(1)
    n_steps = pl.num_programs(0) * pl.num_programs(1)
    slot = lax.rem(step, 2)
    tm = pos_ref.shape[0]
    r_rows = blk_scr.shape[1]

    def fetch(tile, into):
        return _segment_copies(tbl_ref, tile0 + tile, blk_scr.at[into], ys_ref, sems.at[into], False, tm)

    @pl.when(step == 0)
    def _():
        blk_scr[...] = jnp.zeros(blk_scr.shape, U32)
        _start_all(fetch(0, 0))

    @pl.when(step + 1 < n_steps)
    def _():
        _start_all(fetch(step + 1, 1 - slot))

    _wait_all(fetch(step, slot))

    pos = pos_ref[...].astype(I32)
    rw = rw_ref[...]
    j = lax.broadcasted_iota(I32, (tm, r_rows), 1)
    pw = jnp.zeros((tm, r_rows), F32)
    for k in range(TOP_K):
        pw = jnp.where(j == pos[:, k:k + 1], rw[:, k:k + 1], pw)
    ffn = jnp.dot(pw.astype(BF16), _unpack_halves(blk_scr[slot]), preferred_element_type=F32)
    y = x1_ref[...] + g2_ref[...] * ffn.reshape(x1_ref.shape)
    o_ref[...] = y * lax.rsqrt(jnp.mean(y * y, axis=-1, keepdims=True) + NORM_EPS) * fw_ref[...]


def _combine(tbl, ys, pos, rw, x1, g2, final_w, sb, tb, tile0):
    n_seq, seq_len, d = x1.shape
    tm = sb * tb
    nt = seq_len // tb
    r_rows = TOP_K * tm + N_EXPERTS * SUBLANES
    row = lambda s, t: s * nt + t
    grid_spec = pltpu.PrefetchScalarGridSpec(
        num_scalar_prefetch=1,
        grid=(n_seq // sb, nt),
        in_specs=[pl.BlockSpec(memory_space=pl.ANY),
                  pl.BlockSpec((tm, ROUTER_LANES), lambda s, t, *_: (row(s, t), 0)),
                  pl.BlockSpec((tm, ROUTER_LANES), lambda s, t, *_: (row(s, t), 0)),
                  pl.BlockSpec((sb, tb, d), lambda s, t, *_: (s, t, 0)),
                  pl.BlockSpec((sb, 1, d), lambda s, t, *_: (s, 0, 0)),
                  pl.BlockSpec((1, d), lambda s, t, *_: (0, 0))],
        out_specs=pl.BlockSpec((sb, tb, d), lambda s, t, *_: (s, t, 0)),
        scratch_shapes=[pltpu.VMEM((2, r_rows, d // 2), U32), pltpu.SemaphoreType.DMA((2,))],
    )
    return pl.pallas_call(
        functools.partial(_combine_body, tile0=tile0),
        grid_spec=grid_spec,
        out_shape=jax.ShapeDtypeStruct((n_seq, seq_len, d), F32),
        compiler_params=_params(("arbitrary", "arbitrary")),
        name="combine",
    )(tbl, ys, pos, rw, x1, g2, final_w)


def _pad_state(state, rows):
    return jnp.pad(state, ((0, 0), (rows - state.shape[1], 0), (0, 0)))


def kernel(x_prompt, x_sample, c_prompt, c_sample, state_gdn_conv, state_gdn_rec, state_sc_conv, w_ada, b_ada,
           norm1_w, w_in, gdn_conv_w, gdn_a_log, gdn_dt_bias, gdn_norm_w, w_branch_a, sc_conv_w, w_branch_b,
           w_out, norm2_w, w_router, b_router, w_gate, b_gate, w_up, b_up, w_down, b_down, final_norm_w):
    assert w_ada.shape[0] == 1, "single-layer trunk"
    d = D_MODEL
    bp, tp, _ = x_prompt.shape
    bs, ts, _ = x_sample.shape
    n_p, n_s = bp * tp, bs * ts
    n_tok = n_p + n_s

    w_in0 = w_in[0]
    g_lo, g_hi = QKV_W + d, QKV_W + d + N_GATE_COLS
    w_qkvz = w_in0[:, :g_lo].astype(BF16)
    w5 = w_in0[:, g_hi:].astype(BF16)
    w_all = jnp.concatenate([w_qkvz, w5], axis=1)
    w_g = w_in0[:, g_lo:g_hi].astype(BF16)
    w_gt = w_g.T
    zeros_h = jnp.zeros((GDN_HEADS,), F32)
    p_row = jnp.stack([jnp.concatenate([zeros_h, gdn_a_log[0]]), jnp.concatenate([zeros_h, gdn_dt_bias[0]])])
    p_col = p_row.T
    w_a = w_branch_a[0].astype(BF16)
    w_b = w_branch_b[0].astype(BF16)
    w_o = w_out[0].astype(BF16)
    w_r = jnp.pad(w_router[0], ((0, 0), (0, ROUTER_LANES - N_EXPERTS)))
    w_r_hi = w_r.astype(BF16)
    w_r = jnp.stack([w_r_hi, (w_r - w_r_hi.astype(F32)).astype(BF16)])
    b_r = jnp.pad(b_router[0], (0, ROUTER_LANES - N_EXPERTS), constant_values=-jnp.inf).reshape(1, ROUTER_LANES)

    n_c = bp + bs
    c_rows = -(-n_c // 16) * 16
    c_all = jnp.pad(jnp.concatenate([c_prompt, c_sample], axis=0), ((0, c_rows - n_c), (0, 0)))
    ada = _ada(c_all, w_ada[0], b_ada[0])

    def ada_parts(lo, hi):
        return [ada[lo:hi, j * d:(j + 1) * d].reshape(hi - lo, 1, d) for j in range(6)]

    groups = [
        dict(x=x_prompt, ada=ada_parts(0, bp), n_seq=bp, seq_len=tp,
             conv0=jnp.zeros((bp, SUBLANES, QKV_W), F32),
             rec0=jnp.zeros((bp, GDN_HEADS, GDN_DK, GDN_DV), F32),
             sc0=jnp.zeros((bp, SUBLANES, d), F32)),
        dict(x=x_sample, ada=ada_parts(bp, n_c), n_seq=bs, seq_len=ts,
             conv0=_pad_state(state_gdn_conv[0], SUBLANES),
             rec0=state_gdn_rec[0],
             sc0=_pad_state(state_sc_conv[0], SUBLANES)),
    ]

    counts = jnp.zeros((SUBLANES, ROUTER_LANES), F32)
    for g in groups:
        sh1, sc1, g1, sh2, sc2, g2 = g["ada"]
        n_seq, seq_len = g["n_seq"], g["seq_len"]
        wide = seq_len > SUBLANES
        on, rec, conv_tail, *p5 = _gdn(g["x"], sc1, sh1, norm1_w, w_all if wide else w_qkvz, w_g, w_gt, p_row, p_col,
                                       g["conv0"], g["rec0"], gdn_conv_w[0], gdn_norm_w)
        x1, h2, pos, rw, counts, tail, post, tbl = _merge(g["x"], on, p5[0] if wide else None, sc1, sh1, norm1_w, w5,
                                                          g["sc0"], g1, sc2, sh2, w_a, w_b, w_o, sc_conv_w[0],
                                                          norm2_w, w_r, b_r, counts)
        sb, tb = _route_tile(n_seq, seq_len)
        g.update(x1=x1, h2=h2, pos=pos, rw=rw, post=post, tbl=tbl, rec=rec, g2=g2, sb=sb, tb=tb, tm=sb * tb,
                 rows_after=counts[0, :N_EXPERTS].astype(I32),
                 new_conv=conv_tail[:, SUBLANES - (GDN_CONV - 1):, :],
                 new_sc=tail.reshape(n_seq, -1, d)[:, -(SC_CONV - 1):, :])

    rows_e = counts[0, :N_EXPERTS].astype(I32)
    padded = (rows_e + EXPERT_BLOCK - 1) // EXPERT_BLOCK * EXPERT_BLOCK
    pad_end = jnp.cumsum(padded)
    expert0 = (pad_end - padded).astype(I32)
    tbl_all = jnp.concatenate([g["tbl"] for g in groups], axis=0).reshape(-1, SUBLANES, ROUTER_LANES)
    n_tiles = tbl_all.shape[0]
    tbl_all = tbl_all[:, :3, :N_EXPERTS].at[:, 2, :].add(expert0[None, :]).reshape(-1)
    rows_first = groups[0]["rows_after"]
    tails = jnp.concatenate([expert0 + rows_first, padded - rows_first, pad_end[-1:]]).astype(I32)
    max_rows = n_tok * TOP_K + n_tiles * N_EXPERTS * (SUBLANES - 1)
    n_blocks = -(-max_rows // EXPERT_BLOCK) + N_EXPERTS
    block_start = jnp.arange(n_blocks, dtype=I32) * EXPERT_BLOCK
    block_e = jnp.minimum(jnp.sum((pad_end[None, :] <= block_start[:, None]).astype(I32), axis=1), N_EXPERTS - 1)
    n_valid = (pad_end[-1:] // EXPERT_BLOCK).astype(I32)

    buf = None
    tile0 = 0
    for g in groups:
        g["tile0"] = tile0
        buf = _dispatch(tbl_all, tails, g["post"], g["h2"], buf, n_blocks * EXPERT_BLOCK, g["tm"], tile0)
        tile0 += g["h2"].shape[0] // g["tm"]
    ys = _experts(block_e, n_valid, buf, w_gate[0], b_gate[0], w_up[0], b_up[0], w_down[0], b_down[0])
    outs = []
    for g in groups:
        n_seq, seq_len = g["n_seq"], g["seq_len"]
        x1 = g["x1"].reshape(n_seq, seq_len, d)
        outs.append(_combine(tbl_all, ys, g["pos"], g["rw"], x1, g["g2"], final_norm_w.reshape(1, d),
                             g["sb"], g["tb"], g["tile0"]))

    gp, gs = groups
    return (outs[0], outs[1], gp["new_conv"][None], gp["rec"][None], gp["new_sc"][None],
            gs["new_conv"][None], gs["rec"][None], gs["new_sc"][None])
```

```python
import functools

import jax
import jax.numpy as jnp
from jax import lax
from jax.experimental import pallas as pl
from jax.experimental.pallas import tpu as pltpu

F32 = jnp.float32
BF16 = jnp.bfloat16
I32 = jnp.int32
U32 = jnp.uint32
HIGHEST = lax.Precision.HIGHEST

D_MODEL = 1024
GDN_HEADS = 8
GDN_DK = 128
GDN_DV = 128
GDN_QK = GDN_HEADS * GDN_DK
QKV_W = 3 * GDN_QK
GDN_CONV = 4
GDN_CHUNK = 64
GDN_STEP_CHUNKS = 4
GDN_STEP_SEQS = 8
SC_CONV = 3
N_EXPERTS = 32
TOP_K = 4
SWIGLU_LIMIT = 7.0
SWIGLU_ALPHA = 1.702
NORM_EPS = 1e-6
N_GATE_COLS = 2 * GDN_HEADS
ROUTER_LANES = 128
EXPERT_BLOCK = 512
ROUTE_TILE = 512
ADA_TILE = 1024
SUBLANES = 8
VMEM_LIMIT = 56 * 1024 * 1024

_NT = (((1,), (1,)), ((), ()))
_TN = (((0,), (0,)), ((), ()))


def _bdot(a, b):
    return jnp.dot(a.astype(BF16), b.astype(BF16), preferred_element_type=F32)


def _bdot_nt(a, b):
    return lax.dot_general(a.astype(BF16), b.astype(BF16), _NT, preferred_element_type=F32)


def _bdot_tn(a, b):
    return lax.dot_general(a.astype(BF16), b.astype(BF16), _TN, preferred_element_type=F32)


def _pack_halves(x):
    w = x.shape[1] // 2
    lo = lax.bitcast_convert_type(x[:, :w], U32)
    hi = lax.bitcast_convert_type(x[:, w:], U32)
    return lax.shift_right_logical(lo, jnp.uint32(16)) | (hi & jnp.uint32(0xFFFF0000))


def _unpack_halves(p):
    lo = lax.bitcast_convert_type(lax.shift_left(p, jnp.uint32(16)), F32)
    hi = lax.bitcast_convert_type(p & jnp.uint32(0xFFFF0000), F32)
    return jnp.concatenate([lo, hi], axis=1).astype(BF16)


def _silu(x):
    return x * jax.nn.sigmoid(x)


def _softplus(x):
    return jnp.maximum(x, 0.0) + jnp.log1p(jnp.exp(-jnp.abs(x)))


def _seq_tile(n_seq, seq_len, target):
    if seq_len >= target:
        assert seq_len % target == 0
        return 1, target
    sb = min(n_seq, target // seq_len)
    assert n_seq % sb == 0 and seq_len % SUBLANES == 0
    return sb, seq_len


def _route_tile(n_seq, seq_len):
    return _seq_tile(n_seq, seq_len, ROUTE_TILE)


def _params(sem):
    return pltpu.CompilerParams(dimension_semantics=sem, vmem_limit_bytes=VMEM_LIMIT)


def _ada_body(c_ref, w_ref, b_ref, o_ref):
    o_ref[...] = _bdot(_silu(c_ref[...]), w_ref[...]) + b_ref[...]


def _ada(c, w_ada, b_ada):
    rows, d = c.shape
    n = w_ada.shape[1]
    tn = ADA_TILE
    return pl.pallas_call(
        _ada_body,
        grid=(n // tn,),
        in_specs=[pl.BlockSpec((rows, d), lambda j: (0, 0)),
                  pl.BlockSpec((d, tn), lambda j: (0, j)),
                  pl.BlockSpec((1, tn), lambda j: (0, j))],
        out_specs=pl.BlockSpec((rows, tn), lambda j: (0, j)),
        out_shape=jax.ShapeDtypeStruct((rows, n), F32),
        compiler_params=_params(("arbitrary",)),
        name="ada",
    )(c, w_ada, b_ada.reshape(1, n))


def _gates(v, a_log, dt_bias, axis):
    is_beta = lax.broadcasted_iota(I32, v.shape, axis) < GDN_HEADS
    beta = jax.nn.sigmoid(v)
    g = -jnp.exp(a_log) * _softplus(v + dt_bias)
    return jnp.where(is_beta, beta, g)


def _ada_norm(x_ref, sc_ref, sh_ref, nw_ref):
    x = x_ref[...]
    y = x * lax.rsqrt(jnp.mean(x * x, axis=-1, keepdims=True) + NORM_EPS) * nw_ref[...]
    h = y * (1.0 + sc_ref[...]) + sh_ref[...]
    return h.reshape(x.shape[0] * x.shape[1], x.shape[2]).astype(BF16)


def _resident(shape):
    return pl.BlockSpec(shape, lambda *_: (0,) * len(shape), pipeline_mode=pl.Buffered(1))


def _shift_rows(x, hist, s, per_seq):
    rows, width = x.shape
    x3 = x.reshape(rows // SUBLANES, SUBLANES, width)
    xr = pltpu.roll(x3, s, 1)
    if per_seq:
        src = pltpu.roll(hist.reshape(x3.shape), s, 1)
    else:
        hr = pltpu.roll(hist.reshape(1, SUBLANES, width), s, 1)
        src = hr if rows == SUBLANES else jnp.concatenate([hr, xr[:-1]], axis=0)
    sub = lax.broadcasted_iota(I32, x3.shape, 1)
    return jnp.where(sub < s, src, xr).reshape(rows, width)


def _causal_conv(x, hist, w, per_seq):
    taps = w.shape[0]
    acc = x * w[taps - 1:taps, :]
    for s in range(1, taps):
        acc = acc + _shift_rows(x, hist, s, per_seq) * w[taps - 1 - s:taps - s, :]
    return acc


def _gdn_body(x_ref, sc_ref, sh_ref, n1_ref, wq_ref, wg_ref, wgt_ref, prow_ref, pcol_ref, conv0_ref, s0_ref,
              cw_ref, nw_ref, o_ref, s_ref, conv_ref, *rest, chunk, n_chunks):
    stacked = s_ref.shape[0] > 1
    hist_scr = rest[-1]

    @pl.when(pl.program_id(1) == 0)
    def _():
        s_ref[...] = s0_ref[...]
        if not stacked:
            hist_scr[...] = conv0_ref[0]

    h_in = _ada_norm(x_ref, sc_ref, sh_ref, n1_ref)
    n_own = QKV_W + D_MODEL
    proj = jnp.dot(h_in, wq_ref[:, :n_own], preferred_element_type=F32)
    x = proj[:, :QKV_W]
    z = proj[:, QKV_W:]
    if len(rest) == 2:
        rest[0][...] = jnp.dot(h_in, wq_ref[:, n_own:], preferred_element_type=F32)
    if stacked:
        hist = conv0_ref[...].reshape(x.shape)
        conv_ref[...] = x.reshape(conv_ref.shape)
    else:
        hist = hist_scr[...]
        hist_scr[...] = x[x.shape[0] - SUBLANES:, :]
        conv_ref[0] = x[x.shape[0] - SUBLANES:, :]
    qkvc = _silu(_causal_conv(x, hist, cw_ref[...], stacked))

    gcol = _gates(jnp.dot(h_in, wg_ref[...], preferred_element_type=F32), prow_ref[0:1, :], prow_ref[1:2, :], 1)
    ri = lax.broadcasted_iota(I32, (chunk, chunk), 0)
    ci = lax.broadcasted_iota(I32, (chunk, chunk), 1)
    causal = ri >= ci
    strict = ri > ci
    tri = causal.astype(F32)
    eye = (ri == ci).astype(F32)
    n_lvl = chunk.bit_length() - 1
    blk = [lax.shift_right_logical(ri, l) == lax.shift_right_logical(ci, l) for l in range(1, n_lvl + 1)]
    pair = [blk[l] & jnp.logical_not(blk[l - 1]) for l in range(1, n_lvl)]

    units = [(c, h) for c in range(n_chunks) for h in range(GDN_HEADS)]
    every = range(len(units))
    q, k, v, beta, dcol, dlast, decay, edec = [], [], [], [], [], [], [], []
    for c in range(n_chunks):
        rows = slice(c * chunk, (c + 1) * chunk)
        dec_col = jnp.dot(tri, gcol[rows, GDN_HEADS:], precision=HIGHEST, preferred_element_type=F32)
        grow = _gates(lax.dot_general(wgt_ref[...], h_in[rows, :], _NT, preferred_element_type=F32),
                      pcol_ref[:, 0:1], pcol_ref[:, 1:2], 0)
        dec_row = lax.dot_general(grow[GDN_HEADS:, :], tri, _NT, precision=HIGHEST, preferred_element_type=F32)
        for h in range(GDN_HEADS):
            qh = qkvc[rows, h * GDN_DK:(h + 1) * GDN_DK]
            kh = qkvc[rows, GDN_QK + h * GDN_DK:GDN_QK + (h + 1) * GDN_DK]
            q.append(qh * (lax.rsqrt(jnp.sum(qh * qh, axis=-1, keepdims=True) + 1e-6) * (GDN_DK ** -0.5)))
            k.append(kh * lax.rsqrt(jnp.sum(kh * kh, axis=-1, keepdims=True) + 1e-6))
            v.append(qkvc[rows, 2 * GDN_QK + h * GDN_DV:2 * GDN_QK + (h + 1) * GDN_DV])
            beta.append(gcol[rows, h:h + 1])
            dcol.append(dec_col[:, h:h + 1])
            dlast.append(dec_col[chunk - 1:chunk, h:h + 1])
            decay.append(jnp.where(causal, jnp.exp(dcol[-1] - dec_row[h:h + 1, :]), 0.0))
            edec.append(jnp.exp(dcol[-1]))
    kk = [_bdot_nt(k[i], k[i]) for i in every]
    qk = [_bdot_nt(q[i], k[i]) * decay[i] for i in every]
    a = [jnp.where(strict, beta[i] * kk[i] * decay[i], 0.0) for i in every]
    inv = [eye - jnp.where(blk[0], a[i], 0.0) for i in every]
    for lower_left in pair:
        right = [_bdot(jnp.where(lower_left, a[i], 0.0), inv[i]) for i in every]
        inv = [inv[i] - _bdot(inv[i], right[i]) for i in every]
    sol = [_bdot(inv[i], jnp.concatenate([beta[i] * v[i], (beta[i] * edec[i]) * k[i]], axis=1)) for i in every]
    seq_of = (lambda c: c) if stacked else (lambda c: 0)
    waves = [list(every)] if stacked else [[c * GDN_HEADS + h for h in range(GDN_HEADS)] for c in range(n_chunks)]
    state = {(seq_of(c), h): s_ref[seq_of(c), h] for c, h in units}
    for wave in waves:
        key = {i: (seq_of(units[i][0]), units[i][1]) for i in wave}
        ws = {i: _bdot(jnp.concatenate([sol[i][:, GDN_DV:], q[i] * edec[i]], axis=0), state[key[i]]) for i in wave}
        u = {i: sol[i][:, :GDN_DV] - ws[i][:chunk] for i in wave}
        o = {i: ws[i][chunk:] + _bdot(qk[i], u[i]) for i in wave}
        upd = {i: _bdot_tn(k[i] * jnp.exp(dlast[i] - dcol[i]), u[i]) for i in wave}
        for i in wave:
            c, h = units[i]
            state[key[i]] = state[key[i]] * jnp.exp(dlast[i]) + upd[i]
            rows = slice(c * chunk, (c + 1) * chunk)
            on = o[i] * lax.rsqrt(jnp.mean(o[i] * o[i], axis=-1, keepdims=True) + NORM_EPS) * nw_ref[...]
            o_ref[rows, h * GDN_DV:(h + 1) * GDN_DV] = on * _silu(z[rows, h * GDN_DV:(h + 1) * GDN_DV])
    for (b, h), value in state.items():
        s_ref[b, h] = value


def _gdn(x, sc, sh, norm1_w, w_proj, w_g, w_gt, p_row, p_col, conv0, state0, conv_w, norm_w):
    n_seq, seq_len, d = x.shape
    chunk = min(GDN_CHUNK, seq_len)
    assert seq_len % chunk == 0 and chunk % SUBLANES == 0
    n_tok = n_seq * seq_len
    stacked = seq_len == SUBLANES and n_seq % GDN_STEP_SEQS == 0
    if stacked:
        n_step_chunks, sb, nt = GDN_STEP_SEQS, GDN_STEP_SEQS, 1
    else:
        n_step_chunks = GDN_STEP_CHUNKS if (seq_len // chunk) % GDN_STEP_CHUNKS == 0 else 1
        sb, nt = 1, seq_len // (n_step_chunks * chunk)
    step_rows = n_step_chunks * chunk
    tb = step_rows // sb
    row = lambda s, t: s * nt + t
    ada = pl.BlockSpec((sb, 1, d), lambda s, t: (s, 0, 0))
    extra = w_proj.shape[1] - (QKV_W + d)
    extra_spec = [pl.BlockSpec((step_rows, extra), lambda s, t: (row(s, t), 0))] if extra else []
    extra_shape = [jax.ShapeDtypeStruct((n_tok, extra), F32)] if extra else []
    return pl.pallas_call(
        functools.partial(_gdn_body, chunk=chunk, n_chunks=n_step_chunks),
        grid=(n_seq // sb, nt),
        in_specs=[pl.BlockSpec((sb, tb, d), lambda s, t: (s, t, 0)),
                  ada, ada, _resident((1, d)),
                  _resident(w_proj.shape), _resident((d, N_GATE_COLS)), _resident((N_GATE_COLS, d)),
                  _resident((2, N_GATE_COLS)), _resident((N_GATE_COLS, 2)),
                  pl.BlockSpec((sb, SUBLANES, QKV_W), lambda s, t: (s, 0, 0)),
                  pl.BlockSpec((sb, GDN_HEADS, GDN_DK, GDN_DV), lambda s, t: (s, 0, 0, 0)),
                  _resident((GDN_CONV, QKV_W)), _resident((1, GDN_DV))],
        out_specs=[pl.BlockSpec((step_rows, d), lambda s, t: (row(s, t), 0)),
                   pl.BlockSpec((sb, GDN_HEADS, GDN_DK, GDN_DV), lambda s, t: (s, 0, 0, 0)),
                   pl.BlockSpec((sb, SUBLANES, QKV_W), lambda s, t: (s, 0, 0))] + extra_spec,
        out_shape=[jax.ShapeDtypeStruct((n_tok, d), F32),
                   jax.ShapeDtypeStruct((n_seq, GDN_HEADS, GDN_DK, GDN_DV), F32),
                   jax.ShapeDtypeStruct((n_seq, SUBLANES, QKV_W), F32)] + extra_shape,
        scratch_shapes=[pltpu.VMEM((SUBLANES, QKV_W), F32)],
        compiler_params=_params(("arbitrary", "arbitrary")),
        name="gdn",
    )(x, sc, sh, norm1_w, w_proj, w_g, w_gt, p_row, p_col, conv0, state0, conv_w, norm_w)


def _merge_body(x_ref, on_ref, *refs, per_seq, n_tiles, projected):
    n_head = 5 if projected else 4
    (sc0_ref, g1_ref, sc2_ref, sh2_ref, wa_ref, wb_ref, wo_ref, cw_ref, n2_ref, wr_ref, br_ref, cnt0_ref,
     x1_ref, h2_ref, pos_ref, rw_ref, cnt_ref, tail_ref, post_ref, tbl_ref, cnt_scr, hist_scr) = refs[n_head:]
    step = pl.program_id(0) * pl.num_programs(1) + pl.program_id(1)

    @pl.when(step == 0)
    def _():
        cnt_scr[...] = cnt0_ref[...]

    tm, d = on_ref.shape
    if projected:
        sc_b, sc_c, sc_h, gate_a, gate_b = (r[...] for r in refs[:n_head])
    else:
        sc1_ref, sh1_ref, n1_ref, w5_ref = refs[:n_head]
        p5 = jnp.dot(_ada_norm(x_ref, sc1_ref, sh1_ref, n1_ref), w5_ref[...], preferred_element_type=F32)
        sc_b, sc_c, sc_h, gate_a, gate_b = (p5[:, j * d:(j + 1) * d] for j in range(5))
    pre = sc_c * sc_h
    if per_seq:
        hist = sc0_ref[...].reshape(tm, d)
        tail_ref[...] = pre
    else:
        @pl.when(pl.program_id(1) == 0)
        def _():
            hist_scr[...] = sc0_ref[0]

        hist = hist_scr[...]
        hist_scr[...] = pre[tm - SUBLANES:, :]
        tail_ref[...] = pre[tm - SUBLANES:, :]
    y_b = _bdot(sc_b * _causal_conv(pre, hist, cw_ref[...], per_seq), wb_ref[...])
    y_a = _bdot(on_ref[...], wa_ref[...])
    merged = jax.nn.sigmoid(gate_a) * y_a + jax.nn.sigmoid(gate_b) * y_b
    mo = _bdot(merged, wo_ref[...]).reshape(x_ref.shape)
    x1 = x_ref[...] + g1_ref[...] * mo
    y = x1 * lax.rsqrt(jnp.mean(x1 * x1, axis=-1, keepdims=True) + NORM_EPS) * n2_ref[...]
    h2 = (y * (1.0 + sc2_ref[...]) + sh2_ref[...]).reshape(tm, d)
    x1_ref[...] = x1.reshape(tm, d)
    h2_ref[...] = h2

    h2_hi = h2.astype(BF16)
    h2_lo = (h2 - h2_hi.astype(F32)).astype(BF16)
    w_hi = wr_ref[0]
    logits = (jnp.dot(h2_hi, w_hi, preferred_element_type=F32)
              + (jnp.dot(h2_hi, wr_ref[1], preferred_element_type=F32)
                 + jnp.dot(h2_lo, w_hi, preferred_element_type=F32))) + br_ref[...]
    lane = lax.broadcasted_iota(I32, logits.shape, 1)
    lane_f = lane.astype(F32)
    work = logits
    vals, hots = [], []
    member = jnp.zeros(logits.shape, F32)
    for _ in range(TOP_K):
        m = jnp.max(work, axis=-1, keepdims=True)
        sel = jnp.min(jnp.where(work == m, lane_f, float(N_EXPERTS - 1)), axis=-1, keepdims=True)
        hot = lane_f == sel
        vals.append(m)
        hots.append(hot)
        member = member + hot.astype(F32)
        work = jnp.where(hot, -jnp.inf, work)
    exps = [jnp.exp(v - vals[0]) for v in vals]
    denom = exps[0] + exps[1] + exps[2] + exps[3]
    ti = lax.broadcasted_iota(I32, (tm, tm), 0)
    tj = lax.broadcasted_iota(I32, (tm, tm), 1)
    before = (tj < ti).astype(BF16)
    rank_loc = jnp.dot(before, member.astype(BF16), preferred_element_type=F32)
    cnt = jnp.sum(member, axis=0, keepdims=True).astype(I32)
    seg = lax.shift_left(lax.shift_right_logical(cnt + (SUBLANES - 1), 3), 3)
    seg8 = jnp.broadcast_to(seg.astype(F32), (SUBLANES, ROUTER_LANES))
    ei = lax.broadcasted_iota(I32, (ROUTER_LANES, ROUTER_LANES), 0)
    ej = lax.broadcasted_iota(I32, (ROUTER_LANES, ROUTER_LANES), 1)
    base8 = jnp.dot(seg8, (ei < ej).astype(F32), precision=HIGHEST, preferred_element_type=F32)
    row_all = base8[0:1, :] + rank_loc
    pos = jnp.zeros((tm, ROUTER_LANES), F32)
    rw = jnp.zeros((tm, ROUTER_LANES), F32)
    for kk in range(TOP_K):
        pos_k = jnp.sum(jnp.where(hots[kk], row_all, 0.0), axis=-1, keepdims=True)
        pos = jnp.where(lane == kk, pos_k, pos)
        rw = jnp.where(lane == kk, exps[kk] / denom, rw)
    pos_ref[...] = pos
    rw_ref[...] = rw
    pick = (lax.broadcasted_iota(I32, (SUBLANES, ROUTER_LANES), 0)
            == lax.broadcasted_iota(I32, (SUBLANES, ROUTER_LANES), 1)).astype(F32)
    post_ref[...] = lax.dot_general(pick, pos, _NT, precision=HIGHEST, preferred_element_type=F32)
    sub = lax.broadcasted_iota(I32, (SUBLANES, ROUTER_LANES), 0)
    tbl_ref[...] = jnp.where(sub == 0, base8.astype(I32),
                             jnp.where(sub == 1, seg8.astype(I32), cnt_scr[...].astype(I32)))
    cnt_scr[...] = cnt_scr[...] + seg8

    @pl.when(step == n_tiles - 1)
    def _():
        cnt_ref[...] = cnt_scr[...]


def _merge(x, on, p5, sc1, sh1, norm1_w, w5, sc0, g1, sc2, sh2, w_a, w_b, w_o, conv_w, norm2_w, w_r, b_r, cnt0):
    n_seq, seq_len, d = x.shape
    sb, tb = _route_tile(n_seq, seq_len)
    per_seq = sb > 1 or tb == SUBLANES
    if per_seq:
        assert tb == SUBLANES
    tm = sb * tb
    nt = seq_len // tb
    ns = n_seq // sb
    n_tok = n_seq * seq_len
    row = lambda s, t: s * nt + t
    full = lambda shape: pl.BlockSpec(shape, lambda s, t: (0,) * len(shape))
    ada = pl.BlockSpec((sb, 1, d), lambda s, t: (s, 0, 0))
    tok = pl.BlockSpec((tm, d), lambda s, t: (row(s, t), 0))
    lanes = pl.BlockSpec((tm, ROUTER_LANES), lambda s, t: (row(s, t), 0))
    tail_rows = tm if per_seq else SUBLANES
    if p5 is None:
        head_specs = [ada, ada, _resident((1, d)), _resident((d, 5 * d))]
        head_args = [sc1, sh1, norm1_w, w5]
    else:
        head_specs = [pl.BlockSpec((tm, d), lambda s, t, j=j: (row(s, t), j)) for j in range(5)]
        head_args = [p5] * 5
    return pl.pallas_call(
        functools.partial(_merge_body, per_seq=per_seq, n_tiles=ns * nt, projected=p5 is not None),
        grid=(ns, nt),
        in_specs=[pl.BlockSpec((sb, tb, d), lambda s, t: (s, t, 0)),
                  tok] + head_specs + [
                  pl.BlockSpec((sb, SUBLANES, d), lambda s, t: (s, 0, 0)),
                  ada, ada, ada,
                  _resident((d, d)), _resident((d, d)), _resident((d, d)),
                  _resident((SC_CONV, d)), _resident((1, d)), _resident((2, d, ROUTER_LANES)),
                  _resident((1, ROUTER_LANES)), _resident((SUBLANES, ROUTER_LANES))],
        out_specs=[tok, tok, lanes, lanes, full((SUBLANES, ROUTER_LANES)),
                   pl.BlockSpec((tail_rows, d), lambda s, t: (row(s, t), 0)),
                   pl.BlockSpec((SUBLANES, tm), lambda s, t: (0, row(s, t))),
                   pl.BlockSpec((SUBLANES, ROUTER_LANES), lambda s, t: (row(s, t), 0))],
        out_shape=[jax.ShapeDtypeStruct((n_tok, d), F32),
                   jax.ShapeDtypeStruct((n_tok, d), F32),
                   jax.ShapeDtypeStruct((n_tok, ROUTER_LANES), F32),
                   jax.ShapeDtypeStruct((n_tok, ROUTER_LANES), F32),
                   jax.ShapeDtypeStruct((SUBLANES, ROUTER_LANES), F32),
                   jax.ShapeDtypeStruct((ns * nt * tail_rows, d), F32),
                   jax.ShapeDtypeStruct((SUBLANES, n_tok), F32),
                   jax.ShapeDtypeStruct((ns * nt * SUBLANES, ROUTER_LANES), I32)],
        scratch_shapes=[pltpu.VMEM((SUBLANES, ROUTER_LANES), F32), pltpu.VMEM((SUBLANES, d), F32)],
        compiler_params=_params(("arbitrary", "arbitrary")),
        name="merge",
    )(x, on, *head_args, sc0, g1, sc2, sh2, w_a, w_b, w_o, conv_w, norm2_w, w_r, b_r, cnt0)


def _pow2_pieces(length, max_rows):
    out = []
    rows = max_rows
    while rows >= SUBLANES:
        shift = rows.bit_length()
        offset = lax.shift_left(lax.shift_right_logical(length, shift), shift)
        out.append(((length & rows) != 0, offset, rows))
        rows //= 2
    return out


def _segment_copies(tbl_ref, tile_id, local_ref, global_ref, sem, to_global, max_rows):
    out = []
    base = tile_id * (3 * N_EXPERTS)
    for e in range(N_EXPERTS):
        local0 = tbl_ref[base + e]
        length = tbl_ref[base + N_EXPERTS + e]
        global0 = tbl_ref[base + 2 * N_EXPERTS + e]
        for pred, offset, rows in _pow2_pieces(length, max_rows):
            loc = local_ref.at[pl.ds(pl.multiple_of(local0 + offset, SUBLANES), rows)]
            glo = global_ref.at[pl.ds(pl.multiple_of(global0 + offset, SUBLANES), rows)]
            cp = pltpu.make_async_copy(loc, glo, sem) if to_global else pltpu.make_async_copy(glo, loc, sem)
            out.append((pred, cp))
    return out


def _start_all(copies):
    for pred, cp in copies:
        pl.when(pred)(cp.start)


def _wait_all(copies):
    for pred, cp in copies:
        pl.when(pred)(cp.wait)


def _dispatch_body(tbl_ref, tails_ref, post_ref, h2_ref, *refs, tile0, zero_tails):
    buf_ref, sorted_scr, zero_scr, sems = refs[-4:]
    i = pl.program_id(0)
    slot = lax.rem(i, 2)
    tm = h2_ref.shape[0]
    r_rows = sorted_scr.shape[1]

    if zero_tails:
        @pl.when(i == 0)
        def _():
            zero_scr[...] = jnp.zeros(zero_scr.shape, U32)
            z_rows = zero_scr.shape[0]

            def fill(b, carry):
                cp = pltpu.make_async_copy(zero_scr, buf_ref.at[pl.ds(pl.multiple_of(b * SUBLANES, SUBLANES), z_rows)],
                                           sems.at[2])
                cp.start()
                cp.wait()
                return carry

            tails = []
            for e in range(N_EXPERTS):
                start, length = tails_ref[e], tails_ref[N_EXPERTS + e]
                whole = length // z_rows
                lax.fori_loop(0, whole, lambda b, c, s=start: fill(s // SUBLANES + b * (z_rows // SUBLANES), c), 0)
                rest0 = start + whole * z_rows
                for pred, offset, rows in _pow2_pieces(length - whole * z_rows, z_rows // 2):
                    at = pl.multiple_of(rest0 + offset, SUBLANES)
                    tails.append((pred, pltpu.make_async_copy(zero_scr.at[pl.ds(0, rows)],
                                                              buf_ref.at[pl.ds(at, rows)], sems.at[2])))
            _start_all(tails)
            _wait_all(tails)
            lax.fori_loop(tails_ref[2 * N_EXPERTS] // z_rows, buf_ref.shape[0] // z_rows,
                          lambda b, c: fill(b * (z_rows // SUBLANES), c), 0)

    pos = post_ref[...].astype(I32)
    j = lax.broadcasted_iota(I32, (r_rows, tm), 0)
    onehot = jnp.zeros((r_rows, tm), F32)
    for k in range(TOP_K):
        onehot = jnp.where(j == pos[k:k + 1, :], 1.0, onehot)
    sorted_scr[slot] = _pack_halves(_bdot(onehot, h2_ref[...]))

    _start_all(_segment_copies(tbl_ref, tile0 + i, sorted_scr.at[slot], buf_ref, sems.at[slot], True, tm))

    @pl.when(i > 0)
    def _():
        _wait_all(_segment_copies(tbl_ref, tile0 + i - 1, sorted_scr.at[1 - slot], buf_ref, sems.at[1 - slot],
                                  True, tm))

    @pl.when(i == pl.num_programs(0) - 1)
    def _():
        _wait_all(_segment_copies(tbl_ref, tile0 + i, sorted_scr.at[slot], buf_ref, sems.at[slot], True, tm))


def _dispatch(tbl, tails, post, h2, buf, buf_rows, tm, tile0):
    n_tok, d = h2.shape
    r_rows = TOP_K * tm + N_EXPERTS * SUBLANES
    first = buf is None
    grid_spec = pltpu.PrefetchScalarGridSpec(
        num_scalar_prefetch=2,
        grid=(n_tok // tm,),
        in_specs=[pl.BlockSpec((SUBLANES, tm), lambda i, *_: (0, i)),
                  pl.BlockSpec((tm, d), lambda i, *_: (i, 0))]
                 + ([] if first else [pl.BlockSpec(memory_space=pl.ANY)]),
        out_specs=pl.BlockSpec(memory_space=pl.ANY),
        scratch_shapes=[pltpu.VMEM((2, r_rows, d // 2), U32),
                        pltpu.VMEM((EXPERT_BLOCK // 2, d // 2), U32),
                        pltpu.SemaphoreType.DMA((3,))],
    )
    return pl.pallas_call(
        functools.partial(_dispatch_body, tile0=tile0, zero_tails=first),
        grid_spec=grid_spec,
        out_shape=jax.ShapeDtypeStruct((buf_rows, d // 2), U32),
        input_output_aliases={} if first else {4: 0},
        compiler_params=pltpu.CompilerParams(dimension_semantics=("arbitrary",), has_side_effects=True,
                                             vmem_limit_bytes=VMEM_LIMIT),
        name="dispatch",
    )(tbl, tails, post, h2, *([] if first else [buf]))


def _expert_body(be_ref, nv_ref, x_ref, wg_ref, bg_ref, wu_ref, bu_ref, wd_ref, bd_ref, o_ref,
                 wg_s, wu_s, wd_s):
    i = pl.program_id(0)

    @pl.when(i < nv_ref[0])
    def _():
        @pl.when((i == 0) | (be_ref[i] != be_ref[jnp.maximum(i - 1, 0)]))
        def _():
            wg_s[...] = wg_ref[0].astype(BF16)
            wu_s[...] = wu_ref[0].astype(BF16)
            wd_s[...] = wd_ref[0].astype(BF16)

        x = _unpack_halves(x_ref[...])
        gate = jnp.dot(x, wg_s[...], preferred_element_type=F32) + bg_ref[0]
        up = jnp.dot(x, wu_s[...], preferred_element_type=F32) + bu_ref[0]
        gate = jnp.minimum(gate, SWIGLU_LIMIT)
        up = jnp.clip(up, -SWIGLU_LIMIT, SWIGLU_LIMIT)
        glu = gate * jax.nn.sigmoid(SWIGLU_ALPHA * gate)
        out = _bdot((up + 1.0) * glu, wd_s[...]) + bd_ref[0]
        o_ref[...] = _pack_halves(out.astype(BF16).astype(F32))

    @pl.when(i >= nv_ref[0])
    def _():
        o_ref[...] = jnp.zeros(o_ref.shape, U32)


def _experts(block_e, n_valid, xs, w_gate, b_gate, w_up, b_up, w_down, b_down):
    m_pad, d_half = xs.shape
    d = 2 * d_half
    n_blocks = m_pad // EXPERT_BLOCK
    f = w_gate.shape[2]
    blk = lambda i, be, nv: (jnp.maximum(jnp.minimum(i, nv[0] - 1), 0), 0)
    wspec = lambda a, b: pl.BlockSpec((1, a, b), lambda i, be, nv: (be[i], 0, 0))
    grid_spec = pltpu.PrefetchScalarGridSpec(
        num_scalar_prefetch=2,
        grid=(n_blocks,),
        in_specs=[pl.BlockSpec((EXPERT_BLOCK, d_half), blk),
                  wspec(d, f), wspec(1, f), wspec(d, f), wspec(1, f), wspec(f, d), wspec(1, d)],
        out_specs=pl.BlockSpec((EXPERT_BLOCK, d_half), lambda i, be, nv: (i, 0)),
        scratch_shapes=[pltpu.VMEM((d, f), BF16), pltpu.VMEM((d, f), BF16), pltpu.VMEM((f, d), BF16)],
    )
    return pl.pallas_call(
        _expert_body,
        grid_spec=grid_spec,
        out_shape=jax.ShapeDtypeStruct((m_pad, d_half), U32),
        compiler_params=_params(("arbitrary",)),
        name="experts",
    )(block_e, n_valid, xs, w_gate, b_gate[:, None, :], w_up, b_up[:, None, :], w_down, b_down[:, None, :])


def _combine_body(tbl_ref, ys_ref, pos_ref, rw_ref, x1_ref, g2_ref, fw_ref, o_ref, blk_scr, sems, *, tile0):
    step = pl.program_id(0) * pl.num_programs(1) + pl.program_id(1)
    n_steps = pl.num_programs(0) * pl.num_programs(1)
    slot = lax.rem(step, 2)
    tm = pos_ref.shape[0]
    r_rows = blk_scr.shape[1]

    def fetch(tile, into):
        return _segment_copies(tbl_ref, tile0 + tile, blk_scr.at[into], ys_ref, sems.at[into], False, tm)

    @pl.when(step == 0)
    def _():
        blk_scr[...] = jnp.zeros(blk_scr.shape, U32)
        _start_all(fetch(0, 0))

    @pl.when(step + 1 < n_steps)
    def _():
        _start_all(fetch(step + 1, 1 - slot))

    _wait_all(fetch(step, slot))

    pos = pos_ref[...].astype(I32)
    rw = rw_ref[...]
    j = lax.broadcasted_iota(I32, (tm, r_rows), 1)
    pw = jnp.zeros((tm, r_rows), F32)
    for k in range(TOP_K):
        pw = jnp.where(j == pos[:, k:k + 1], rw[:, k:k + 1], pw)
    ffn = jnp.dot(pw.astype(BF16), _unpack_halves(blk_scr[slot]), preferred_element_type=F32)
    y = x1_ref[...] + g2_ref[...] * ffn.reshape(x1_ref.shape)
    o_ref[...] = y * lax.rsqrt(jnp.mean(y * y, axis=-1, keepdims=True) + NORM_EPS) * fw_ref[...]


def _combine(tbl, ys, pos, rw, x1, g2, final_w, sb, tb, tile0):
    n_seq, seq_len, d = x1.shape
    tm = sb * tb
    nt = seq_len // tb
    r_rows = TOP_K * tm + N_EXPERTS * SUBLANES
    row = lambda s, t: s * nt + t
    grid_spec = pltpu.PrefetchScalarGridSpec(
        num_scalar_prefetch=1,
        grid=(n_seq // sb, nt),
        in_specs=[pl.BlockSpec(memory_space=pl.ANY),
                  pl.BlockSpec((tm, ROUTER_LANES), lambda s, t, *_: (row(s, t), 0)),
                  pl.BlockSpec((tm, ROUTER_LANES), lambda s, t, *_: (row(s, t), 0)),
                  pl.BlockSpec((sb, tb, d), lambda s, t, *_: (s, t, 0)),
                  pl.BlockSpec((sb, 1, d), lambda s, t, *_: (s, 0, 0)),
                  pl.BlockSpec((1, d), lambda s, t, *_: (0, 0))],
        out_specs=pl.BlockSpec((sb, tb, d), lambda s, t, *_: (s, t, 0)),
        scratch_shapes=[pltpu.VMEM((2, r_rows, d // 2), U32), pltpu.SemaphoreType.DMA((2,))],
    )
    return pl.pallas_call(
        functools.partial(_combine_body, tile0=tile0),
        grid_spec=grid_spec,
        out_shape=jax.ShapeDtypeStruct((n_seq, seq_len, d), F32),
        compiler_params=_params(("arbitrary", "arbitrary")),
        name="combine",
    )(tbl, ys, pos, rw, x1, g2, final_w)


def _pad_state(state, rows):
    return jnp.pad(state, ((0, 0), (rows - state.shape[1], 0), (0, 0)))


def kernel(x_prompt, x_sample, c_prompt, c_sample, state_gdn_conv, state_gdn_rec, state_sc_conv, w_ada, b_ada,
           norm1_w, w_in, gdn_conv_w, gdn_a_log, gdn_dt_bias, gdn_norm_w, w_branch_a, sc_conv_w, w_branch_b,
           w_out, norm2_w, w_router, b_router, w_gate, b_gate, w_up, b_up, w_down, b_down, final_norm_w):
    assert w_ada.shape[0] == 1, "single-layer trunk"
    d = D_MODEL
    bp, tp, _ = x_prompt.shape
    bs, ts, _ = x_sample.shape
    n_p, n_s = bp * tp, bs * ts
    n_tok = n_p + n_s

    w_in0 = w_in[0]
    g_lo, g_hi = QKV_W + d, QKV_W + d + N_GATE_COLS
    w_qkvz = w_in0[:, :g_lo].astype(BF16)
    w5 = w_in0[:, g_hi:].astype(BF16)
    w_all = jnp.concatenate([w_qkvz, w5], axis=1)
    w_g = w_in0[:, g_lo:g_hi].astype(BF16)
    w_gt = w_g.T
    zeros_h = jnp.zeros((GDN_HEADS,), F32)
    p_row = jnp.stack([jnp.concatenate([zeros_h, gdn_a_log[0]]), jnp.concatenate([zeros_h, gdn_dt_bias[0]])])
    p_col = p_row.T
    w_a = w_branch_a[0].astype(BF16)
    w_b = w_branch_b[0].astype(BF16)
    w_o = w_out[0].astype(BF16)
    w_r = jnp.pad(w_router[0], ((0, 0), (0, ROUTER_LANES - N_EXPERTS)))
    w_r_hi = w_r.astype(BF16)
    w_r = jnp.stack([w_r_hi, (w_r - w_r_hi.astype(F32)).astype(BF16)])
    b_r = jnp.pad(b_router[0], (0, ROUTER_LANES - N_EXPERTS), constant_values=-jnp.inf).reshape(1, ROUTER_LANES)

    n_c = bp + bs
    c_rows = -(-n_c // 16) * 16
    c_all = jnp.pad(jnp.concatenate([c_prompt, c_sample], axis=0), ((0, c_rows - n_c), (0, 0)))
    ada = _ada(c_all, w_ada[0], b_ada[0])

    def ada_parts(lo, hi):
        return [ada[lo:hi, j * d:(j + 1) * d].reshape(hi - lo, 1, d) for j in range(6)]

    groups = [
        dict(x=x_prompt, ada=ada_parts(0, bp), n_seq=bp, seq_len=tp,
             conv0=jnp.zeros((bp, SUBLANES, QKV_W), F32),
             rec0=jnp.zeros((bp, GDN_HEADS, GDN_DK, GDN_DV), F32),
             sc0=jnp.zeros((bp, SUBLANES, d), F32)),
        dict(x=x_sample, ada=ada_parts(bp, n_c), n_seq=bs, seq_len=ts,
             conv0=_pad_state(state_gdn_conv[0], SUBLANES),
             rec0=state_gdn_rec[0],
             sc0=_pad_state(state_sc_conv[0], SUBLANES)),
    ]

    counts = jnp.zeros((SUBLANES, ROUTER_LANES), F32)
    for g in groups:
        sh1, sc1, g1, sh2, sc2, g2 = g["ada"]
        n_seq, seq_len = g["n_seq"], g["seq_len"]
        wide = seq_len > SUBLANES
        on, rec, conv_tail, *p5 = _gdn(g["x"], sc1, sh1, norm1_w, w_all if wide else w_qkvz, w_g, w_gt, p_row, p_col,
                                       g["conv0"], g["rec0"], gdn_conv_w[0], gdn_norm_w)
        x1, h2, pos, rw, counts, tail, post, tbl = _merge(g["x"], on, p5[0] if wide else None, sc1, sh1, norm1_w, w5,
                                                          g["sc0"], g1, sc2, sh2, w_a, w_b, w_o, sc_conv_w[0],
                                                          norm2_w, w_r, b_r, counts)
        sb, tb = _route_tile(n_seq, seq_len)
        g.update(x1=x1, h2=h2, pos=pos, rw=rw, post=post, tbl=tbl, rec=rec, g2=g2, sb=sb, tb=tb, tm=sb * tb,
                 rows_after=counts[0, :N_EXPERTS].astype(I32),
                 new_conv=conv_tail[:, SUBLANES - (GDN_CONV - 1):, :],
                 new_sc=tail.reshape(n_seq, -1, d)[:, -(SC_CONV - 1):, :])

    rows_e = counts[0, :N_EXPERTS].astype(I32)
    padded = (rows_e + EXPERT_BLOCK - 1) // EXPERT_BLOCK * EXPERT_BLOCK
    pad_end = jnp.cumsum(padded)
    expert0 = (pad_end - padded).astype(I32)
    tbl_all = jnp.concatenate([g["tbl"] for g in groups], axis=0).reshape(-1, SUBLANES, ROUTER_LANES)
    n_tiles = tbl_all.shape[0]
    tbl_all = tbl_all[:, :3, :N_EXPERTS].at[:, 2, :].add(expert0[None, :]).reshape(-1)
    rows_first = groups[0]["rows_after"]
    tails = jnp.concatenate([expert0 + rows_first, padded - rows_first, pad_end[-1:]]).astype(I32)
    max_rows = n_tok * TOP_K + n_tiles * N_EXPERTS * (SUBLANES - 1)
    n_blocks = -(-max_rows // EXPERT_BLOCK) + N_EXPERTS
    block_start = jnp.arange(n_blocks, dtype=I32) * EXPERT_BLOCK
    block_e = jnp.minimum(jnp.sum((pad_end[None, :] <= block_start[:, None]).astype(I32), axis=1), N_EXPERTS - 1)
    n_valid = (pad_end[-1:] // EXPERT_BLOCK).astype(I32)

    buf = None
    tile0 = 0
    for g in groups:
        g["tile0"] = tile0
        buf = _dispatch(tbl_all, tails, g["post"], g["h2"], buf, n_blocks * EXPERT_BLOCK, g["tm"], tile0)
        tile0 += g["h2"].shape[0] // g["tm"]
    ys = _experts(block_e, n_valid, buf, w_gate[0], b_gate[0], w_up[0], b_up[0], w_down[0], b_down[0])
    outs = []
    for g in groups:
        n_seq, seq_len = g["n_seq"], g["seq_len"]
        x1 = g["x1"].reshape(n_seq, seq_len, d)
        outs.append(_combine(tbl_all, ys, g["pos"], g["rw"], x1, g["g2"], final_norm_w.reshape(1, d),
                             g["sb"], g["tb"], g["tile0"]))

    gp, gs = groups
    return (outs[0], outs[1], gp["new_conv"][None], gp["rec"][None], gp["new_sc"][None],
            gs["new_conv"][None], gs["rec"][None], gs["new_sc"][None])
```

```python
import functools

import jax
import jax.numpy as jnp
from jax import lax
from jax.experimental import pallas as pl
from jax.experimental.pallas import tpu as pltpu

F32 = jnp.float32
BF16 = jnp.bfloat16
I32 = jnp.int32
U32 = jnp.uint32
HIGHEST = lax.Precision.HIGHEST

D_MODEL = 1024
GDN_HEADS = 8
GDN_DK = 128
GDN_DV = 128
GDN_QK = GDN_HEADS * GDN_DK
QKV_W = 3 * GDN_QK
GDN_CONV = 4
GDN_CHUNK = 64
GDN_STEP_CHUNKS = 4
GDN_STEP_SEQS = 8
SC_CONV = 3
N_EXPERTS = 32
TOP_K = 4
SWIGLU_LIMIT = 7.0
SWIGLU_ALPHA = 1.702
NORM_EPS = 1e-6
N_GATE_COLS = 2 * GDN_HEADS
ROUTER_LANES = 128
EXPERT_BLOCK = 512
ROUTE_TILE = 512
ADA_TILE = 1024
SUBLANES = 8
VMEM_LIMIT = 56 * 1024 * 1024

_NT = (((1,), (1,)), ((), ()))
_TN = (((0,), (0,)), ((), ()))


def _bdot(a, b):
    return jnp.dot(a.astype(BF16), b.astype(BF16), preferred_element_type=F32)


def _bdot_nt(a, b):
    return lax.dot_general(a.astype(BF16), b.astype(BF16), _NT, preferred_element_type=F32)


def _bdot_tn(a, b):
    return lax.dot_general(a.astype(BF16), b.astype(BF16), _TN, preferred_element_type=F32)


def _pack_halves(x):
    w = x.shape[1] // 2
    lo = lax.bitcast_convert_type(x[:, :w], U32)
    hi = lax.bitcast_convert_type(x[:, w:], U32)
    return lax.shift_right_logical(lo, jnp.uint32(16)) | (hi & jnp.uint32(0xFFFF0000))


def _unpack_halves(p):
    lo = lax.bitcast_convert_type(lax.shift_left(p, jnp.uint32(16)), F32)
    hi = lax.bitcast_convert_type(p & jnp.uint32(0xFFFF0000), F32)
    return jnp.concatenate([lo, hi], axis=1).astype(BF16)


def _silu(x):
    return x * jax.nn.sigmoid(x)


def _softplus(x):
    return jnp.maximum(x, 0.0) + jnp.log1p(jnp.exp(-jnp.abs(x)))


def _seq_tile(n_seq, seq_len, target):
    if seq_len >= target:
        assert seq_len % target == 0
        return 1, target
    sb = min(n_seq, target // seq_len)
    assert n_seq % sb == 0 and seq_len % SUBLANES == 0
    return sb, seq_len


def _route_tile(n_seq, seq_len):
    return _seq_tile(n_seq, seq_len, ROUTE_TILE)


def _params(sem):
    return pltpu.CompilerParams(dimension_semantics=sem, vmem_limit_bytes=VMEM_LIMIT)


def _ada_body(c_ref, w_ref, b_ref, o_ref):
    o_ref[...] = _bdot(_silu(c_ref[...]), w_ref[...]) + b_ref[...]


def _ada(c, w_ada, b_ada):
    rows, d = c.shape
    n = w_ada.shape[1]
    tn = ADA_TILE
    return pl.pallas_call(
        _ada_body,
        grid=(n // tn,),
        in_specs=[pl.BlockSpec((rows, d), lambda j: (0, 0)),
                  pl.BlockSpec((d, tn), lambda j: (0, j)),
                  pl.BlockSpec((1, tn), lambda j: (0, j))],
        out_specs=pl.BlockSpec((rows, tn), lambda j: (0, j)),
        out_shape=jax.ShapeDtypeStruct((rows, n), F32),
        compiler_params=_params(("arbitrary",)),
        name="ada",
    )(c, w_ada, b_ada.reshape(1, n))


def _gates(v, a_log, dt_bias, axis):
    is_beta = lax.broadcasted_iota(I32, v.shape, axis) < GDN_HEADS
    beta = jax.nn.sigmoid(v)
    g = -jnp.exp(a_log) * _softplus(v + dt_bias)
    return jnp.where(is_beta, beta, g)


def _ada_norm(x_ref, sc_ref, sh_ref, nw_ref):
    x = x_ref[...]
    y = x * lax.rsqrt(jnp.mean(x * x, axis=-1, keepdims=True) + NORM_EPS) * nw_ref[...]
    h = y * (1.0 + sc_ref[...]) + sh_ref[...]
    return h.reshape(x.shape[0] * x.shape[1], x.shape[2]).astype(BF16)


def _resident(shape):
    return pl.BlockSpec(shape, lambda *_: (0,) * len(shape), pipeline_mode=pl.Buffered(1))


def _shift_rows(x, hist, s, per_seq):
    rows, width = x.shape
    x3 = x.reshape(rows // SUBLANES, SUBLANES, width)
    xr = pltpu.roll(x3, s, 1)
    if per_seq:
        src = pltpu.roll(hist.reshape(x3.shape), s, 1)
    else:
        hr = pltpu.roll(hist.reshape(1, SUBLANES, width), s, 1)
        src = hr if rows == SUBLANES else jnp.concatenate([hr, xr[:-1]], axis=0)
    sub = lax.broadcasted_iota(I32, x3.shape, 1)
    return jnp.where(sub < s, src, xr).reshape(rows, width)


def _causal_conv(x, hist, w, per_seq):
    taps = w.shape[0]
    acc = x * w[taps - 1:taps, :]
    for s in range(1, taps):
        acc = acc + _shift_rows(x, hist, s, per_seq) * w[taps - 1 - s:taps - s, :]
    return acc


def _gdn_body(x_ref, sc_ref, sh_ref, n1_ref, wq_ref, wg_ref, wgt_ref, prow_ref, pcol_ref, conv0_ref, s0_ref,
              cw_ref, nw_ref, o_ref, s_ref, conv_ref, *rest, chunk, n_chunks):
    stacked = s_ref.shape[0] > 1
    hist_scr = rest[-1]

    @pl.when(pl.program_id(1) == 0)
    def _():
        s_ref[...] = s0_ref[...]
        if not stacked:
            hist_scr[...] = conv0_ref[0]

    h_in = _ada_norm(x_ref, sc_ref, sh_ref, n1_ref)
    n_own = QKV_W + D_MODEL
    proj = jnp.dot(h_in, wq_ref[:, :n_own], preferred_element_type=F32)
    x = proj[:, :QKV_W]
    z = proj[:, QKV_W:]
    if len(rest) == 2:
        rest[0][...] = jnp.dot(h_in, wq_ref[:, n_own:], preferred_element_type=F32)
    if stacked:
        hist = conv0_ref[...].reshape(x.shape)
        conv_ref[...] = x.reshape(conv_ref.shape)
    else:
        hist = hist_scr[...]
        hist_scr[...] = x[x.shape[0] - SUBLANES:, :]
        conv_ref[0] = x[x.shape[0] - SUBLANES:, :]
    qkvc = _silu(_causal_conv(x, hist, cw_ref[...], stacked))

    gcol = _gates(jnp.dot(h_in, wg_ref[...], preferred_element_type=F32), prow_ref[0:1, :], prow_ref[1:2, :], 1)
    ri = lax.broadcasted_iota(I32, (chunk, chunk), 0)
    ci = lax.broadcasted_iota(I32, (chunk, chunk), 1)
    causal = ri >= ci
    strict = ri > ci
    tri = causal.astype(F32)
    eye = (ri == ci).astype(F32)
    n_lvl = chunk.bit_length() - 1
    blk = [lax.shift_right_logical(ri, l) == lax.shift_right_logical(ci, l) for l in range(1, n_lvl + 1)]
    pair = [blk[l] & jnp.logical_not(blk[l - 1]) for l in range(1, n_lvl)]

    units = [(c, h) for c in range(n_chunks) for h in range(GDN_HEADS)]
    every = range(len(units))
    q, k, v, beta, dcol, dlast, decay, edec = [], [], [], [], [], [], [], []
    for c in range(n_chunks):
        rows = slice(c * chunk, (c + 1) * chunk)
        dec_col = jnp.dot(tri, gcol[rows, GDN_HEADS:], precision=HIGHEST, preferred_element_type=F32)
        grow = _gates(lax.dot_general(wgt_ref[...], h_in[rows, :], _NT, preferred_element_type=F32),
                      pcol_ref[:, 0:1], pcol_ref[:, 1:2], 0)
        dec_row = lax.dot_general(grow[GDN_HEADS:, :], tri, _NT, precision=HIGHEST, preferred_element_type=F32)
        for h in range(GDN_HEADS):
            qh = qkvc[rows, h * GDN_DK:(h + 1) * GDN_DK]
            kh = qkvc[rows, GDN_QK + h * GDN_DK:GDN_QK + (h + 1) * GDN_DK]
            q.append(qh * (lax.rsqrt(jnp.sum(qh * qh, axis=-1, keepdims=True) + 1e-6) * (GDN_DK ** -0.5)))
            k.append(kh * lax.rsqrt(jnp.sum(kh * kh, axis=-1, keepdims=True) + 1e-6))
            v.append(qkvc[rows, 2 * GDN_QK + h * GDN_DV:2 * GDN_QK + (h + 1) * GDN_DV])
            beta.append(gcol[rows, h:h + 1])
            dcol.append(dec_col[:, h:h + 1])
            dlast.append(dec_col[chunk - 1:chunk, h:h + 1])
            decay.append(jnp.where(causal, jnp.exp(dcol[-1] - dec_row[h:h + 1, :]), 0.0))
            edec.append(jnp.exp(dcol[-1]))
    kk = [_bdot_nt(k[i], k[i]) for i in every]
    qk = [_bdot_nt(q[i], k[i]) * decay[i] for i in every]
    a = [jnp.where(strict, beta[i] * kk[i] * decay[i], 0.0) for i in every]
    inv = [eye - jnp.where(blk[0], a[i], 0.0) for i in every]
    for lower_left in pair:
        right = [_bdot(jnp.where(lower_left, a[i], 0.0), inv[i]) for i in every]
        inv = [inv[i] - _bdot(inv[i], right[i]) for i in every]
    sol = [_bdot(inv[i], jnp.concatenate([beta[i] * v[i], (beta[i] * edec[i]) * k[i]], axis=1)) for i in every]
    seq_of = (lambda c: c) if stacked else (lambda c: 0)
    waves = [list(every)] if stacked else [[c * GDN_HEADS + h for h in range(GDN_HEADS)] for c in range(n_chunks)]
    state = {(seq_of(c), h): s_ref[seq_of(c), h] for c, h in units}
    for wave in waves:
        key = {i: (seq_of(units[i][0]), units[i][1]) for i in wave}
        ws = {i: _bdot(jnp.concatenate([sol[i][:, GDN_DV:], q[i] * edec[i]], axis=0), state[key[i]]) for i in wave}
        u = {i: sol[i][:, :GDN_DV] - ws[i][:chunk] for i in wave}
        o = {i: ws[i][chunk:] + _bdot(qk[i], u[i]) for i in wave}
        upd = {i: _bdot_tn(k[i] * jnp.exp(dlast[i] - dcol[i]), u[i]) for i in wave}
        for i in wave:
            c, h = units[i]
            state[key[i]] = state[key[i]] * jnp.exp(dlast[i]) + upd[i]
            rows = slice(c * chunk, (c + 1) * chunk)
            on = o[i] * lax.rsqrt(jnp.mean(o[i] * o[i], axis=-1, keepdims=True) + NORM_EPS) * nw_ref[...]
            o_ref[rows, h * GDN_DV:(h + 1) * GDN_DV] = on * _silu(z[rows, h * GDN_DV:(h + 1) * GDN_DV])
    for (b, h), value in state.items():
        s_ref[b, h] = value


def _gdn(x, sc, sh, norm1_w, w_proj, w_g, w_gt, p_row, p_col, conv0, state0, conv_w, norm_w):
    n_seq, seq_len, d = x.shape
    chunk = min(GDN_CHUNK, seq_len)
    assert seq_len % chunk == 0 and chunk % SUBLANES == 0
    n_tok = n_seq * seq_len
    stacked = seq_len == SUBLANES and n_seq % GDN_STEP_SEQS == 0
    if stacked:
        n_step_chunks, sb, nt = GDN_STEP_SEQS, GDN_STEP_SEQS, 1
    else:
        n_step_chunks = GDN_STEP_CHUNKS if (seq_len // chunk) % GDN_STEP_CHUNKS == 0 else 1
        sb, nt = 1, seq_len // (n_step_chunks * chunk)
    step_rows = n_step_chunks * chunk
    tb = step_rows // sb
    row = lambda s, t: s * nt + t
    ada = pl.BlockSpec((sb, 1, d), lambda s, t: (s, 0, 0))
    extra = w_proj.shape[1] - (QKV_W + d)
    extra_spec = [pl.BlockSpec((step_rows, extra), lambda s, t: (row(s, t), 0))] if extra else []
    extra_shape = [jax.ShapeDtypeStruct((n_tok, extra), F32)] if extra else []
    return pl.pallas_call(
        functools.partial(_gdn_body, chunk=chunk, n_chunks=n_step_chunks),
        grid=(n_seq // sb, nt),
        in_specs=[pl.BlockSpec((sb, tb, d), lambda s, t: (s, t, 0)),
                  ada, ada, _resident((1, d)),
                  _resident(w_proj.shape), _resident((d, N_GATE_COLS)), _resident((N_GATE_COLS, d)),
                  _resident((2, N_GATE_COLS)), _resident((N_GATE_COLS, 2)),
                  pl.BlockSpec((sb, SUBLANES, QKV_W), lambda s, t: (s, 0, 0)),
                  pl.BlockSpec((sb, GDN_HEADS, GDN_DK, GDN_DV), lambda s, t: (s, 0, 0, 0)),
                  _resident((GDN_CONV, QKV_W)), _resident((1, GDN_DV))],
        out_specs=[pl.BlockSpec((step_rows, d), lambda s, t: (row(s, t), 0)),
                   pl.BlockSpec((sb, GDN_HEADS, GDN_DK, GDN_DV), lambda s, t: (s, 0, 0, 0)),
                   pl.BlockSpec((sb, SUBLANES, QKV_W), lambda s, t: (s, 0, 0))] + extra_spec,
        out_shape=[jax.ShapeDtypeStruct((n_tok, d), F32),
                   jax.ShapeDtypeStruct((n_seq, GDN_HEADS, GDN_DK, GDN_DV), F32),
                   jax.ShapeDtypeStruct((n_seq, SUBLANES, QKV_W), F32)] + extra_shape,
        scratch_shapes=[pltpu.VMEM((SUBLANES, QKV_W), F32)],
        compiler_params=_params(("arbitrary", "arbitrary")),
        name="gdn",
    )(x, sc, sh, norm1_w, w_proj, w_g, w_gt, p_row, p_col, conv0, state0, conv_w, norm_w)


def _merge_body(x_ref, on_ref, *refs, per_seq, n_tiles, projected):
    n_head = 5 if projected else 4
    (sc0_ref, g1_ref, sc2_ref, sh2_ref, wa_ref, wb_ref, wo_ref, cw_ref, n2_ref, wr_ref, br_ref, cnt0_ref,
     x1_ref, h2_ref, pos_ref, rw_ref, cnt_ref, tail_ref, post_ref, tbl_ref, cnt_scr, hist_scr) = refs[n_head:]
    step = pl.program_id(0) * pl.num_programs(1) + pl.program_id(1)

    @pl.when(step == 0)
    def _():
        cnt_scr[...] = cnt0_ref[...]

    tm, d = on_ref.shape
    if projected:
        sc_b, sc_c, sc_h, gate_a, gate_b = (r[...] for r in refs[:n_head])
    else:
        sc1_ref, sh1_ref, n1_ref, w5_ref = refs[:n_head]
        p5 = jnp.dot(_ada_norm(x_ref, sc1_ref, sh1_ref, n1_ref), w5_ref[...], preferred_element_type=F32)
        sc_b, sc_c, sc_h, gate_a, gate_b = (p5[:, j * d:(j + 1) * d] for j in range(5))
    pre = sc_c * sc_h
    if per_seq:
        hist = sc0_ref[...].reshape(tm, d)
        tail_ref[...] = pre
    else:
        @pl.when(pl.program_id(1) == 0)
        def _():
            hist_scr[...] = sc0_ref[0]

        hist = hist_scr[...]
        hist_scr[...] = pre[tm - SUBLANES:, :]
        tail_ref[...] = pre[tm - SUBLANES:, :]
    y_b = _bdot(sc_b * _causal_conv(pre, hist, cw_ref[...], per_seq), wb_ref[...])
    y_a = _bdot(on_ref[...], wa_ref[...])
    merged = jax.nn.sigmoid(gate_a) * y_a + jax.nn.sigmoid(gate_b) * y_b
    mo = _bdot(merged, wo_ref[...]).reshape(x_ref.shape)
    x1 = x_ref[...] + g1_ref[...] * mo
    y = x1 * lax.rsqrt(jnp.mean(x1 * x1, axis=-1, keepdims=True) + NORM_EPS) * n2_ref[...]
    h2 = (y * (1.0 + sc2_ref[...]) + sh2_ref[...]).reshape(tm, d)
    x1_ref[...] = x1.reshape(tm, d)
    h2_ref[...] = h2

    h2_hi = h2.astype(BF16)
    h2_lo = (h2 - h2_hi.astype(F32)).astype(BF16)
    w_hi = wr_ref[0]
    logits = (jnp.dot(h2_hi, w_hi, preferred_element_type=F32)
              + (jnp.dot(h2_hi, wr_ref[1], preferred_element_type=F32)
                 + jnp.dot(h2_lo, w_hi, preferred_element_type=F32))) + br_ref[...]
    lane = lax.broadcasted_iota(I32, logits.shape, 1)
    lane_f = lane.astype(F32)
    work = logits
    vals, hots = [], []
    member = jnp.zeros(logits.shape, F32)
    for _ in range(TOP_K):
        m = jnp.max(work, axis=-1, keepdims=True)
        sel = jnp.min(jnp.where(work == m, lane_f, float(N_EXPERTS - 1)), axis=-1, keepdims=True)
        hot = lane_f == sel
        vals.append(m)
        hots.append(hot)
        member = member + hot.astype(F32)
        work = jnp.where(hot, -jnp.inf, work)
    exps = [jnp.exp(v - vals[0]) for v in vals]
    denom = exps[0] + exps[1] + exps[2] + exps[3]
    ti = lax.broadcasted_iota(I32, (tm, tm), 0)
    tj = lax.broadcasted_iota(I32, (tm, tm), 1)
    before = (tj < ti).astype(BF16)
    rank_loc = jnp.dot(before, member.astype(BF16), preferred_element_type=F32)
    cnt = jnp.sum(member, axis=0, keepdims=True).astype(I32)
    seg = lax.shift_left(lax.shift_right_logical(cnt + (SUBLANES - 1), 3), 3)
    seg8 = jnp.broadcast_to(seg.astype(F32), (SUBLANES, ROUTER_LANES))
    ei = lax.broadcasted_iota(I32, (ROUTER_LANES, ROUTER_LANES), 0)
    ej = lax.broadcasted_iota(I32, (ROUTER_LANES, ROUTER_LANES), 1)
    base8 = jnp.dot(seg8, (ei < ej).astype(F32), precision=HIGHEST, preferred_element_type=F32)
    row_all = base8[0:1, :] + rank_loc
    pos = jnp.zeros((tm, ROUTER_LANES), F32)
    rw = jnp.zeros((tm, ROUTER_LANES), F32)
    for kk in range(TOP_K):
        pos_k = jnp.sum(jnp.where(hots[kk], row_all, 0.0), axis=-1, keepdims=True)
        pos = jnp.where(lane == kk, pos_k, pos)
        rw = jnp.where(lane == kk, exps[kk] / denom, rw)
    pos_ref[...] = pos
    rw_ref[...] = rw
    pick = (lax.broadcasted_iota(I32, (SUBLANES, ROUTER_LANES), 0)
            == lax.broadcasted_iota(I32, (SUBLANES, ROUTER_LANES), 1)).astype(F32)
    post_ref[...] = lax.dot_general(pick, pos, _NT, precision=HIGHEST, preferred_element_type=F32)
    sub = lax.broadcasted_iota(I32, (SUBLANES, ROUTER_LANES), 0)
    tbl_ref[...] = jnp.where(sub == 0, base8.astype(I32),
                             jnp.where(sub == 1, seg8.astype(I32), cnt_scr[...].astype(I32)))
    cnt_scr[...] = cnt_scr[...] + seg8

    @pl.when(step == n_tiles - 1)
    def _():
        cnt_ref[...] = cnt_scr[...]


def _merge(x, on, p5, sc1, sh1, norm1_w, w5, sc0, g1, sc2, sh2, w_a, w_b, w_o, conv_w, norm2_w, w_r, b_r, cnt0):
    n_seq, seq_len, d = x.shape
    sb, tb = _route_tile(n_seq, seq_len)
    per_seq = sb > 1 or tb == SUBLANES
    if per_seq:
        assert tb == SUBLANES
    tm = sb * tb
    nt = seq_len // tb
    ns = n_seq // sb
    n_tok = n_seq * seq_len
    row = lambda s, t: s * nt + t
    full = lambda shape: pl.BlockSpec(shape, lambda s, t: (0,) * len(shape))
    ada = pl.BlockSpec((sb, 1, d), lambda s, t: (s, 0, 0))
    tok = pl.BlockSpec((tm, d), lambda s, t: (row(s, t), 0))
    lanes = pl.BlockSpec((tm, ROUTER_LANES), lambda s, t: (row(s, t), 0))
    tail_rows = tm if per_seq else SUBLANES
    if p5 is None:
        head_specs = [ada, ada, _resident((1, d)), _resident((d, 5 * d))]
        head_args = [sc1, sh1, norm1_w, w5]
    else:
        head_specs = [pl.BlockSpec((tm, d), lambda s, t, j=j: (row(s, t), j)) for j in range(5)]
        head_args = [p5] * 5
    return pl.pallas_call(
        functools.partial(_merge_body, per_seq=per_seq, n_tiles=ns * nt, projected=p5 is not None),
        grid=(ns, nt),
        in_specs=[pl.BlockSpec((sb, tb, d), lambda s, t: (s, t, 0)),
                  tok] + head_specs + [
                  pl.BlockSpec((sb, SUBLANES, d), lambda s, t: (s, 0, 0)),
                  ada, ada, ada,
                  _resident((d, d)), _resident((d, d)), _resident((d, d)),
                  _resident((SC_CONV, d)), _resident((1, d)), _resident((2, d, ROUTER_LANES)),
                  _resident((1, ROUTER_LANES)), _resident((SUBLANES, ROUTER_LANES))],
        out_specs=[tok, tok, lanes, lanes, full((SUBLANES, ROUTER_LANES)),
                   pl.BlockSpec((tail_rows, d), lambda s, t: (row(s, t), 0)),
                   pl.BlockSpec((SUBLANES, tm), lambda s, t: (0, row(s, t))),
                   pl.BlockSpec((SUBLANES, ROUTER_LANES), lambda s, t: (row(s, t), 0))],
        out_shape=[jax.ShapeDtypeStruct((n_tok, d), F32),
                   jax.ShapeDtypeStruct((n_tok, d), F32),
                   jax.ShapeDtypeStruct((n_tok, ROUTER_LANES), F32),
                   jax.ShapeDtypeStruct((n_tok, ROUTER_LANES), F32),
                   jax.ShapeDtypeStruct((SUBLANES, ROUTER_LANES), F32),
                   jax.ShapeDtypeStruct((ns * nt * tail_rows, d), F32),
                   jax.ShapeDtypeStruct((SUBLANES, n_tok), F32),
                   jax.ShapeDtypeStruct((ns * nt * SUBLANES, ROUTER_LANES), I32)],
        scratch_shapes=[pltpu.VMEM((SUBLANES, ROUTER_LANES), F32), pltpu.VMEM((SUBLANES, d), F32)],
        compiler_params=_params(("arbitrary", "arbitrary")),
        name="merge",
    )(x, on, *head_args, sc0, g1, sc2, sh2, w_a, w_b, w_o, conv_w, norm2_w, w_r, b_r, cnt0)


def _pow2_pieces(length, max_rows):
    out = []
    rows = max_rows
    while rows >= SUBLANES:
        shift = rows.bit_length()
        offset = lax.shift_left(lax.shift_right_logical(length, shift), shift)
        out.append(((length & rows) != 0, offset, rows))
        rows //= 2
    return out


def _segment_copies(tbl_ref, tile_id, local_ref, global_ref, sem, to_global, max_rows):
    out = []
    base = tile_id * (3 * N_EXPERTS)
    for e in range(N_EXPERTS):
        local0 = tbl_ref[base + e]
        length = tbl_ref[base + N_EXPERTS + e]
        global0 = tbl_ref[base + 2 * N_EXPERTS + e]
        for pred, offset, rows in _pow2_pieces(length, max_rows):
            loc = local_ref.at[pl.ds(pl.multiple_of(local0 + offset, SUBLANES), rows)]
            glo = global_ref.at[pl.ds(pl.multiple_of(global0 + offset, SUBLANES), rows)]
            cp = pltpu.make_async_copy(loc, glo, sem) if to_global else pltpu.make_async_copy(glo, loc, sem)
            out.append((pred, cp))
    return out


def _wait_rows(total, local_ref, global_ref, sem, to_global):
    top = 1 << (local_ref.shape[0].bit_length() - 1)
    for pred, _, rows in _pow2_pieces(total, top):
        loc, glo = local_ref.at[pl.ds(0, rows)], global_ref.at[pl.ds(0, rows)]
        cp = pltpu.make_async_copy(loc, glo, sem) if to_global else pltpu.make_async_copy(glo, loc, sem)
        pl.when(pred)(cp.wait)


def _start_all(copies):
    for pred, cp in copies:
        pl.when(pred)(cp.start)


def _wait_all(copies):
    for pred, cp in copies:
        pl.when(pred)(cp.wait)


def _dispatch_body(tbl_ref, tails_ref, post_ref, h2_ref, *refs, tile0, totals_at, zero_tails):
    buf_ref, sorted_scr, zero_scr, sems = refs[-4:]
    i = pl.program_id(0)
    slot = lax.rem(i, 2)
    tm = h2_ref.shape[0]
    r_rows = sorted_scr.shape[1]

    if zero_tails:
        @pl.when(i == 0)
        def _():
            zero_scr[...] = jnp.zeros(zero_scr.shape, U32)
            z_rows = zero_scr.shape[0]

            def fill(b, carry):
                cp = pltpu.make_async_copy(zero_scr, buf_ref.at[pl.ds(pl.multiple_of(b * SUBLANES, SUBLANES), z_rows)],
                                           sems.at[2])
                cp.start()
                cp.wait()
                return carry

            tails = []
            for e in range(N_EXPERTS):
                start, length = tails_ref[e], tails_ref[N_EXPERTS + e]
                whole = length // z_rows
                lax.fori_loop(0, whole, lambda b, c, s=start: fill(s // SUBLANES + b * (z_rows // SUBLANES), c), 0)
                rest0 = start + whole * z_rows
                for pred, offset, rows in _pow2_pieces(length - whole * z_rows, z_rows // 2):
                    at = pl.multiple_of(rest0 + offset, SUBLANES)
                    tails.append((pred, pltpu.make_async_copy(zero_scr.at[pl.ds(0, rows)],
                                                              buf_ref.at[pl.ds(at, rows)], sems.at[2])))
            _start_all(tails)
            _wait_all(tails)
            lax.fori_loop(tails_ref[2 * N_EXPERTS] // z_rows, buf_ref.shape[0] // z_rows,
                          lambda b, c: fill(b * (z_rows // SUBLANES), c), 0)

    pos = post_ref[...].astype(I32)
    j = lax.broadcasted_iota(I32, (r_rows, tm), 0)
    onehot = jnp.zeros((r_rows, tm), F32)
    for k in range(TOP_K):
        onehot = jnp.where(j == pos[k:k + 1, :], 1.0, onehot)
    sorted_scr[slot] = _pack_halves(_bdot(onehot, h2_ref[...]))

    _start_all(_segment_copies(tbl_ref, tile0 + i, sorted_scr.at[slot], buf_ref, sems.at[slot], True, tm))

    @pl.when(i > 0)
    def _():
        _wait_rows(tbl_ref[totals_at + tile0 + i - 1], sorted_scr.at[1 - slot], buf_ref, sems.at[1 - slot], True)

    @pl.when(i == pl.num_programs(0) - 1)
    def _():
        _wait_rows(tbl_ref[totals_at + tile0 + i], sorted_scr.at[slot], buf_ref, sems.at[slot], True)


def _dispatch(tbl, tails, post, h2, buf, buf_rows, tm, tile0, totals_at):
    n_tok, d = h2.shape
    r_rows = TOP_K * tm + N_EXPERTS * SUBLANES
    first = buf is None
    grid_spec = pltpu.PrefetchScalarGridSpec(
        num_scalar_prefetch=2,
        grid=(n_tok // tm,),
        in_specs=[pl.BlockSpec((SUBLANES, tm), lambda i, *_: (0, i)),
                  pl.BlockSpec((tm, d), lambda i, *_: (i, 0))]
                 + ([] if first else [pl.BlockSpec(memory_space=pl.ANY)]),
        out_specs=pl.BlockSpec(memory_space=pl.ANY),
        scratch_shapes=[pltpu.VMEM((2, r_rows, d // 2), U32),
                        pltpu.VMEM((EXPERT_BLOCK // 2, d // 2), U32),
                        pltpu.SemaphoreType.DMA((3,))],
    )
    return pl.pallas_call(
        functools.partial(_dispatch_body, tile0=tile0, totals_at=totals_at, zero_tails=first),
        grid_spec=grid_spec,
        out_shape=jax.ShapeDtypeStruct((buf_rows, d // 2), U32),
        input_output_aliases={} if first else {4: 0},
        compiler_params=pltpu.CompilerParams(dimension_semantics=("arbitrary",), has_side_effects=True,
                                             vmem_limit_bytes=VMEM_LIMIT),
        name="dispatch",
    )(tbl, tails, post, h2, *([] if first else [buf]))


def _expert_body(be_ref, nv_ref, x_ref, wg_ref, bg_ref, wu_ref, bu_ref, wd_ref, bd_ref, o_ref,
                 wg_s, wu_s, wd_s):
    i = pl.program_id(0)

    @pl.when(i < nv_ref[0])
    def _():
        @pl.when((i == 0) | (be_ref[i] != be_ref[jnp.maximum(i - 1, 0)]))
        def _():
            wg_s[...] = wg_ref[0].astype(BF16)
            wu_s[...] = wu_ref[0].astype(BF16)
            wd_s[...] = wd_ref[0].astype(BF16)

        x = _unpack_halves(x_ref[...])
        gate = jnp.dot(x, wg_s[...], preferred_element_type=F32) + bg_ref[0]
        up = jnp.dot(x, wu_s[...], preferred_element_type=F32) + bu_ref[0]
        gate = jnp.minimum(gate, SWIGLU_LIMIT)
        up = jnp.clip(up, -SWIGLU_LIMIT, SWIGLU_LIMIT)
        glu = gate * jax.nn.sigmoid(SWIGLU_ALPHA * gate)
        out = _bdot((up + 1.0) * glu, wd_s[...]) + bd_ref[0]
        o_ref[...] = _pack_halves(out.astype(BF16).astype(F32))

    @pl.when(i >= nv_ref[0])
    def _():
        o_ref[...] = jnp.zeros(o_ref.shape, U32)


def _experts(block_e, n_valid, xs, w_gate, b_gate, w_up, b_up, w_down, b_down):
    m_pad, d_half = xs.shape
    d = 2 * d_half
    n_blocks = m_pad // EXPERT_BLOCK
    f = w_gate.shape[2]
    blk = lambda i, be, nv: (jnp.maximum(jnp.minimum(i, nv[0] - 1), 0), 0)
    wspec = lambda a, b: pl.BlockSpec((1, a, b), lambda i, be, nv: (be[i], 0, 0))
    grid_spec = pltpu.PrefetchScalarGridSpec(
        num_scalar_prefetch=2,
        grid=(n_blocks,),
        in_specs=[pl.BlockSpec((EXPERT_BLOCK, d_half), blk),
                  wspec(d, f), wspec(1, f), wspec(d, f), wspec(1, f), wspec(f, d), wspec(1, d)],
        out_specs=pl.BlockSpec((EXPERT_BLOCK, d_half), lambda i, be, nv: (i, 0)),
        scratch_shapes=[pltpu.VMEM((d, f), BF16), pltpu.VMEM((d, f), BF16), pltpu.VMEM((f, d), BF16)],
    )
    return pl.pallas_call(
        _expert_body,
        grid_spec=grid_spec,
        out_shape=jax.ShapeDtypeStruct((m_pad, d_half), U32),
        compiler_params=_params(("arbitrary",)),
        name="experts",
    )(block_e, n_valid, xs, w_gate, b_gate[:, None, :], w_up, b_up[:, None, :], w_down, b_down[:, None, :])


def _combine_body(tbl_ref, ys_ref, pos_ref, rw_ref, x1_ref, g2_ref, fw_ref, o_ref, blk_scr, sems,
                  *, tile0, totals_at):
    step = pl.program_id(0) * pl.num_programs(1) + pl.program_id(1)
    n_steps = pl.num_programs(0) * pl.num_programs(1)
    slot = lax.rem(step, 2)
    tm = pos_ref.shape[0]
    r_rows = blk_scr.shape[1]

    def fetch(tile, into):
        return _segment_copies(tbl_ref, tile0 + tile, blk_scr.at[into], ys_ref, sems.at[into], False, tm)

    @pl.when(step == 0)
    def _():
        blk_scr[...] = jnp.zeros(blk_scr.shape, U32)
        _start_all(fetch(0, 0))

    @pl.when(step + 1 < n_steps)
    def _():
        _start_all(fetch(step + 1, 1 - slot))

    _wait_rows(tbl_ref[totals_at + tile0 + step], blk_scr.at[slot], ys_ref, sems.at[slot], False)

    pos = pos_ref[...].astype(I32)
    rw = rw_ref[...]
    j = lax.broadcasted_iota(I32, (tm, r_rows), 1)
    pw = jnp.zeros((tm, r_rows), F32)
    for k in range(TOP_K):
        pw = jnp.where(j == pos[:, k:k + 1], rw[:, k:k + 1], pw)
    ffn = jnp.dot(pw.astype(BF16), _unpack_halves(blk_scr[slot]), preferred_element_type=F32)
    y = x1_ref[...] + g2_ref[...] * ffn.reshape(x1_ref.shape)
    o_ref[...] = y * lax.rsqrt(jnp.mean(y * y, axis=-1, keepdims=True) + NORM_EPS) * fw_ref[...]


def _combine(tbl, ys, pos, rw, x1, g2, final_w, sb, tb, tile0, totals_at):
    n_seq, seq_len, d = x1.shape
    tm = sb * tb
    nt = seq_len // tb
    r_rows = TOP_K * tm + N_EXPERTS * SUBLANES
    row = lambda s, t: s * nt + t
    grid_spec = pltpu.PrefetchScalarGridSpec(
        num_scalar_prefetch=1,
        grid=(n_seq // sb, nt),
        in_specs=[pl.BlockSpec(memory_space=pl.ANY),
                  pl.BlockSpec((tm, ROUTER_LANES), lambda s, t, *_: (row(s, t), 0)),
                  pl.BlockSpec((tm, ROUTER_LANES), lambda s, t, *_: (row(s, t), 0)),
                  pl.BlockSpec((sb, tb, d), lambda s, t, *_: (s, t, 0)),
                  pl.BlockSpec((sb, 1, d), lambda s, t, *_: (s, 0, 0)),
                  pl.BlockSpec((1, d), lambda s, t, *_: (0, 0))],
        out_specs=pl.BlockSpec((sb, tb, d), lambda s, t, *_: (s, t, 0)),
        scratch_shapes=[pltpu.VMEM((2, r_rows, d // 2), U32), pltpu.SemaphoreType.DMA((2,))],
    )
    return pl.pallas_call(
        functools.partial(_combine_body, tile0=tile0, totals_at=totals_at),
        grid_spec=grid_spec,
        out_shape=jax.ShapeDtypeStruct((n_seq, seq_len, d), F32),
        compiler_params=_params(("arbitrary", "arbitrary")),
        name="combine",
    )(tbl, ys, pos, rw, x1, g2, final_w)


def _pad_state(state, rows):
    return jnp.pad(state, ((0, 0), (rows - state.shape[1], 0), (0, 0)))


def kernel(x_prompt, x_sample, c_prompt, c_sample, state_gdn_conv, state_gdn_rec, state_sc_conv, w_ada, b_ada,
           norm1_w, w_in, gdn_conv_w, gdn_a_log, gdn_dt_bias, gdn_norm_w, w_branch_a, sc_conv_w, w_branch_b,
           w_out, norm2_w, w_router, b_router, w_gate, b_gate, w_up, b_up, w_down, b_down, final_norm_w):
    assert w_ada.shape[0] == 1, "single-layer trunk"
    d = D_MODEL
    bp, tp, _ = x_prompt.shape
    bs, ts, _ = x_sample.shape
    n_p, n_s = bp * tp, bs * ts
    n_tok = n_p + n_s

    w_in0 = w_in[0]
    g_lo, g_hi = QKV_W + d, QKV_W + d + N_GATE_COLS
    w_qkvz = w_in0[:, :g_lo].astype(BF16)
    w5 = w_in0[:, g_hi:].astype(BF16)
    w_all = jnp.concatenate([w_qkvz, w5], axis=1)
    w_g = w_in0[:, g_lo:g_hi].astype(BF16)
    w_gt = w_g.T
    zeros_h = jnp.zeros((GDN_HEADS,), F32)
    p_row = jnp.stack([jnp.concatenate([zeros_h, gdn_a_log[0]]), jnp.concatenate([zeros_h, gdn_dt_bias[0]])])
    p_col = p_row.T
    w_a = w_branch_a[0].astype(BF16)
    w_b = w_branch_b[0].astype(BF16)
    w_o = w_out[0].astype(BF16)
    w_r = jnp.pad(w_router[0], ((0, 0), (0, ROUTER_LANES - N_EXPERTS)))
    w_r_hi = w_r.astype(BF16)
    w_r = jnp.stack([w_r_hi, (w_r - w_r_hi.astype(F32)).astype(BF16)])
    b_r = jnp.pad(b_router[0], (0, ROUTER_LANES - N_EXPERTS), constant_values=-jnp.inf).reshape(1, ROUTER_LANES)

    n_c = bp + bs
    c_rows = -(-n_c // 16) * 16
    c_all = jnp.pad(jnp.concatenate([c_prompt, c_sample], axis=0), ((0, c_rows - n_c), (0, 0)))
    ada = _ada(c_all, w_ada[0], b_ada[0])

    def ada_parts(lo, hi):
        return [ada[lo:hi, j * d:(j + 1) * d].reshape(hi - lo, 1, d) for j in range(6)]

    groups = [
        dict(x=x_prompt, ada=ada_parts(0, bp), n_seq=bp, seq_len=tp,
             conv0=jnp.zeros((bp, SUBLANES, QKV_W), F32),
             rec0=jnp.zeros((bp, GDN_HEADS, GDN_DK, GDN_DV), F32),
             sc0=jnp.zeros((bp, SUBLANES, d), F32)),
        dict(x=x_sample, ada=ada_parts(bp, n_c), n_seq=bs, seq_len=ts,
             conv0=_pad_state(state_gdn_conv[0], SUBLANES),
             rec0=state_gdn_rec[0],
             sc0=_pad_state(state_sc_conv[0], SUBLANES)),
    ]

    counts = jnp.zeros((SUBLANES, ROUTER_LANES), F32)
    for g in groups:
        sh1, sc1, g1, sh2, sc2, g2 = g["ada"]
        n_seq, seq_len = g["n_seq"], g["seq_len"]
        wide = seq_len > SUBLANES
        on, rec, conv_tail, *p5 = _gdn(g["x"], sc1, sh1, norm1_w, w_all if wide else w_qkvz, w_g, w_gt, p_row, p_col,
                                       g["conv0"], g["rec0"], gdn_conv_w[0], gdn_norm_w)
        x1, h2, pos, rw, counts, tail, post, tbl = _merge(g["x"], on, p5[0] if wide else None, sc1, sh1, norm1_w, w5,
                                                          g["sc0"], g1, sc2, sh2, w_a, w_b, w_o, sc_conv_w[0],
                                                          norm2_w, w_r, b_r, counts)
        sb, tb = _route_tile(n_seq, seq_len)
        g.update(x1=x1, h2=h2, pos=pos, rw=rw, post=post, tbl=tbl, rec=rec, g2=g2, sb=sb, tb=tb, tm=sb * tb,
                 rows_after=counts[0, :N_EXPERTS].astype(I32),
                 new_conv=conv_tail[:, SUBLANES - (GDN_CONV - 1):, :],
                 new_sc=tail.reshape(n_seq, -1, d)[:, -(SC_CONV - 1):, :])

    rows_e = counts[0, :N_EXPERTS].astype(I32)
    padded = (rows_e + EXPERT_BLOCK - 1) // EXPERT_BLOCK * EXPERT_BLOCK
    pad_end = jnp.cumsum(padded)
    expert0 = (pad_end - padded).astype(I32)
    tbl_all = jnp.concatenate([g["tbl"] for g in groups], axis=0).reshape(-1, SUBLANES, ROUTER_LANES)
    n_tiles = tbl_all.shape[0]
    tbl_all = tbl_all[:, :3, :N_EXPERTS].at[:, 2, :].add(expert0[None, :])
    totals_at = n_tiles * 3 * N_EXPERTS
    tbl_all = jnp.concatenate([tbl_all.reshape(-1), jnp.sum(tbl_all[:, 1, :], axis=1)]).astype(I32)
    rows_first = groups[0]["rows_after"]
    tails = jnp.concatenate([expert0 + rows_first, padded - rows_first, pad_end[-1:]]).astype(I32)
    max_rows = n_tok * TOP_K + n_tiles * N_EXPERTS * (SUBLANES - 1)
    n_blocks = -(-max_rows // EXPERT_BLOCK) + N_EXPERTS
    block_start = jnp.arange(n_blocks, dtype=I32) * EXPERT_BLOCK
    block_e = jnp.minimum(jnp.sum((pad_end[None, :] <= block_start[:, None]).astype(I32), axis=1), N_EXPERTS - 1)
    n_valid = (pad_end[-1:] // EXPERT_BLOCK).astype(I32)

    buf = None
    tile0 = 0
    for g in groups:
        g["tile0"] = tile0
        buf = _dispatch(tbl_all, tails, g["post"], g["h2"], buf, n_blocks * EXPERT_BLOCK, g["tm"], tile0, totals_at)
        tile0 += g["h2"].shape[0] // g["tm"]
    ys = _experts(block_e, n_valid, buf, w_gate[0], b_gate[0], w_up[0], b_up[0], w_down[0], b_down[0])
    outs = []
    for g in groups:
        n_seq, seq_len = g["n_seq"], g["seq_len"]
        x1 = g["x1"].reshape(n_seq, seq_len, d)
        outs.append(_combine(tbl_all, ys, g["pos"], g["rw"], x1, g["g2"], final_norm_w.reshape(1, d),
                             g["sb"], g["tb"], g["tile0"], totals_at))

    gp, gs = groups
    return (outs[0], outs[1], gp["new_conv"][None], gp["rec"][None], gp["new_sc"][None],
            gs["new_conv"][None], gs["rec"][None], gs["new_sc"][None])
```

```python
import functools

import jax
import jax.numpy as jnp
from jax import lax
from jax.experimental import pallas as pl
from jax.experimental.pallas import tpu as pltpu

F32 = jnp.float32
BF16 = jnp.bfloat16
I32 = jnp.int32
U32 = jnp.uint32
HIGHEST = lax.Precision.HIGHEST

D_MODEL = 1024
GDN_HEADS = 8
GDN_DK = 128
GDN_DV = 128
GDN_QK = GDN_HEADS * GDN_DK
QKV_W = 3 * GDN_QK
GDN_CONV = 4
GDN_CHUNK = 128
GDN_STEP_CHUNKS = 2
GDN_STEP_SEQS = 8
SC_CONV = 3
N_EXPERTS = 32
TOP_K = 4
SWIGLU_LIMIT = 7.0
SWIGLU_ALPHA = 1.702
NORM_EPS = 1e-6
N_GATE_COLS = 2 * GDN_HEADS
ROUTER_LANES = 128
EXPERT_BLOCK = 512
ROUTE_TILE = 512
ADA_TILE = 1024
SUBLANES = 8
VMEM_LIMIT = 56 * 1024 * 1024

_NT = (((1,), (1,)), ((), ()))
_TN = (((0,), (0,)), ((), ()))


def _bdot(a, b):
    return jnp.dot(a.astype(BF16), b.astype(BF16), preferred_element_type=F32)


def _bdot_nt(a, b):
    return lax.dot_general(a.astype(BF16), b.astype(BF16), _NT, preferred_element_type=F32)


def _bdot_tn(a, b):
    return lax.dot_general(a.astype(BF16), b.astype(BF16), _TN, preferred_element_type=F32)


def _pack_halves(x):
    w = x.shape[1] // 2
    lo = lax.bitcast_convert_type(x[:, :w], U32)
    hi = lax.bitcast_convert_type(x[:, w:], U32)
    return lax.shift_right_logical(lo, jnp.uint32(16)) | (hi & jnp.uint32(0xFFFF0000))


def _unpack_halves(p):
    lo = lax.bitcast_convert_type(lax.shift_left(p, jnp.uint32(16)), F32)
    hi = lax.bitcast_convert_type(p & jnp.uint32(0xFFFF0000), F32)
    return jnp.concatenate([lo, hi], axis=1).astype(BF16)


def _silu(x):
    return x * jax.nn.sigmoid(x)


def _softplus(x):
    return jnp.maximum(x, 0.0) + jnp.log1p(jnp.exp(-jnp.abs(x)))


def _seq_tile(n_seq, seq_len, target):
    if seq_len >= target:
        assert seq_len % target == 0
        return 1, target
    sb = min(n_seq, target // seq_len)
    assert n_seq % sb == 0 and seq_len % SUBLANES == 0
    return sb, seq_len


def _route_tile(n_seq, seq_len):
    return _seq_tile(n_seq, seq_len, ROUTE_TILE)


def _params(sem):
    return pltpu.CompilerParams(dimension_semantics=sem, vmem_limit_bytes=VMEM_LIMIT)


def _ada_body(c_ref, w_ref, b_ref, o_ref):
    o_ref[...] = _bdot(_silu(c_ref[...]), w_ref[...]) + b_ref[...]


def _ada(c, w_ada, b_ada):
    rows, d = c.shape
    n = w_ada.shape[1]
    tn = ADA_TILE
    return pl.pallas_call(
        _ada_body,
        grid=(n // tn,),
        in_specs=[pl.BlockSpec((rows, d), lambda j: (0, 0)),
                  pl.BlockSpec((d, tn), lambda j: (0, j)),
                  pl.BlockSpec((1, tn), lambda j: (0, j))],
        out_specs=pl.BlockSpec((rows, tn), lambda j: (0, j)),
        out_shape=jax.ShapeDtypeStruct((rows, n), F32),
        compiler_params=_params(("arbitrary",)),
        name="ada",
    )(c, w_ada, b_ada.reshape(1, n))


def _gates(v, a_log, dt_bias, axis):
    is_beta = lax.broadcasted_iota(I32, v.shape, axis) < GDN_HEADS
    beta = jax.nn.sigmoid(v)
    g = -jnp.exp(a_log) * _softplus(v + dt_bias)
    return jnp.where(is_beta, beta, g)


def _ada_norm(x_ref, sc_ref, sh_ref, nw_ref):
    x = x_ref[...]
    y = x * lax.rsqrt(jnp.mean(x * x, axis=-1, keepdims=True) + NORM_EPS) * nw_ref[...]
    h = y * (1.0 + sc_ref[...]) + sh_ref[...]
    return h.reshape(x.shape[0] * x.shape[1], x.shape[2]).astype(BF16)


def _resident(shape):
    return pl.BlockSpec(shape, lambda *_: (0,) * len(shape), pipeline_mode=pl.Buffered(1))


def _shift_rows(x, hist, s, per_seq):
    rows, width = x.shape
    x3 = x.reshape(rows // SUBLANES, SUBLANES, width)
    xr = pltpu.roll(x3, s, 1)
    if per_seq:
        src = pltpu.roll(hist.reshape(x3.shape), s, 1)
    else:
        hr = pltpu.roll(hist.reshape(1, SUBLANES, width), s, 1)
        src = hr if rows == SUBLANES else jnp.concatenate([hr, xr[:-1]], axis=0)
    sub = lax.broadcasted_iota(I32, x3.shape, 1)
    return jnp.where(sub < s, src, xr).reshape(rows, width)


def _causal_conv(x, hist, w, per_seq):
    taps = w.shape[0]
    acc = x * w[taps - 1:taps, :]
    for s in range(1, taps):
        acc = acc + _shift_rows(x, hist, s, per_seq) * w[taps - 1 - s:taps - s, :]
    return acc


def _gdn_body(x_ref, sc_ref, sh_ref, n1_ref, wq_ref, wg_ref, wgt_ref, prow_ref, pcol_ref, conv0_ref, s0_ref,
              cw_ref, nw_ref, o_ref, s_ref, conv_ref, *rest, chunk, n_chunks):
    stacked = s_ref.shape[0] > 1
    hist_scr = rest[-1]

    @pl.when(pl.program_id(1) == 0)
    def _():
        s_ref[...] = s0_ref[...]
        if not stacked:
            hist_scr[...] = conv0_ref[0]

    h_in = _ada_norm(x_ref, sc_ref, sh_ref, n1_ref)
    n_own = QKV_W + D_MODEL
    proj = jnp.dot(h_in, wq_ref[:, :n_own], preferred_element_type=F32)
    x = proj[:, :QKV_W]
    z = proj[:, QKV_W:]
    if len(rest) == 2:
        rest[0][...] = jnp.dot(h_in, wq_ref[:, n_own:], preferred_element_type=F32)
    if stacked:
        hist = conv0_ref[...].reshape(x.shape)
        conv_ref[...] = x.reshape(conv_ref.shape)
    else:
        hist = hist_scr[...]
        hist_scr[...] = x[x.shape[0] - SUBLANES:, :]
        conv_ref[0] = x[x.shape[0] - SUBLANES:, :]
    qkvc = _silu(_causal_conv(x, hist, cw_ref[...], stacked))

    gcol = _gates(jnp.dot(h_in, wg_ref[...], preferred_element_type=F32), prow_ref[0:1, :], prow_ref[1:2, :], 1)
    ri = lax.broadcasted_iota(I32, (chunk, chunk), 0)
    ci = lax.broadcasted_iota(I32, (chunk, chunk), 1)
    causal = ri >= ci
    strict = ri > ci
    tri = causal.astype(F32)
    eye = (ri == ci).astype(F32)
    n_lvl = chunk.bit_length() - 1
    blk = [lax.shift_right_logical(ri, l) == lax.shift_right_logical(ci, l) for l in range(1, n_lvl + 1)]
    pair = [blk[l] & jnp.logical_not(blk[l - 1]) for l in range(1, n_lvl)]

    units = [(c, h) for c in range(n_chunks) for h in range(GDN_HEADS)]
    every = range(len(units))
    q, k, v, beta, dcol, dlast, decay, edec = [], [], [], [], [], [], [], []
    for c in range(n_chunks):
        rows = slice(c * chunk, (c + 1) * chunk)
        dec_col = jnp.dot(tri, gcol[rows, GDN_HEADS:], precision=HIGHEST, preferred_element_type=F32)
        grow = _gates(lax.dot_general(wgt_ref[...], h_in[rows, :], _NT, preferred_element_type=F32),
                      pcol_ref[:, 0:1], pcol_ref[:, 1:2], 0)
        dec_row = lax.dot_general(grow[GDN_HEADS:, :], tri, _NT, precision=HIGHEST, preferred_element_type=F32)
        for h in range(GDN_HEADS):
            qh = qkvc[rows, h * GDN_DK:(h + 1) * GDN_DK]
            kh = qkvc[rows, GDN_QK + h * GDN_DK:GDN_QK + (h + 1) * GDN_DK]
            q.append(qh * (lax.rsqrt(jnp.sum(qh * qh, axis=-1, keepdims=True) + 1e-6) * (GDN_DK ** -0.5)))
            k.append(kh * lax.rsqrt(jnp.sum(kh * kh, axis=-1, keepdims=True) + 1e-6))
            v.append(qkvc[rows, 2 * GDN_QK + h * GDN_DV:2 * GDN_QK + (h + 1) * GDN_DV])
            beta.append(gcol[rows, h:h + 1])
            dcol.append(dec_col[:, h:h + 1])
            dlast.append(dec_col[chunk - 1:chunk, h:h + 1])
            decay.append(jnp.where(causal, jnp.exp(dcol[-1] - dec_row[h:h + 1, :]), 0.0))
            edec.append(jnp.exp(dcol[-1]))
    kk = [_bdot_nt(k[i], k[i]) for i in every]
    qk = [_bdot_nt(q[i], k[i]) * decay[i] for i in every]
    a = [jnp.where(strict, beta[i] * kk[i] * decay[i], 0.0) for i in every]
    inv = [eye - jnp.where(blk[0], a[i], 0.0) for i in every]
    for lower_left in pair:
        right = [_bdot(jnp.where(lower_left, a[i], 0.0), inv[i]) for i in every]
        inv = [inv[i] - _bdot(inv[i], right[i]) for i in every]
    sol = [_bdot(inv[i], jnp.concatenate([beta[i] * v[i], (beta[i] * edec[i]) * k[i]], axis=1)) for i in every]
    seq_of = (lambda c: c) if stacked else (lambda c: 0)
    waves = [list(every)] if stacked else [[c * GDN_HEADS + h for h in range(GDN_HEADS)] for c in range(n_chunks)]
    state = {(seq_of(c), h): s_ref[seq_of(c), h] for c, h in units}
    for wave in waves:
        key = {i: (seq_of(units[i][0]), units[i][1]) for i in wave}
        ws = {i: _bdot(jnp.concatenate([sol[i][:, GDN_DV:], q[i] * edec[i]], axis=0), state[key[i]]) for i in wave}
        u = {i: sol[i][:, :GDN_DV] - ws[i][:chunk] for i in wave}
        o = {i: ws[i][chunk:] + _bdot(qk[i], u[i]) for i in wave}
        upd = {i: _bdot_tn(k[i] * jnp.exp(dlast[i] - dcol[i]), u[i]) for i in wave}
        for i in wave:
            c, h = units[i]
            state[key[i]] = state[key[i]] * jnp.exp(dlast[i]) + upd[i]
            rows = slice(c * chunk, (c + 1) * chunk)
            on = o[i] * lax.rsqrt(jnp.mean(o[i] * o[i], axis=-1, keepdims=True) + NORM_EPS) * nw_ref[...]
            o_ref[rows, h * GDN_DV:(h + 1) * GDN_DV] = on * _silu(z[rows, h * GDN_DV:(h + 1) * GDN_DV])
    for (b, h), value in state.items():
        s_ref[b, h] = value


def _gdn(x, sc, sh, norm1_w, w_proj, w_g, w_gt, p_row, p_col, conv0, state0, conv_w, norm_w):
    n_seq, seq_len, d = x.shape
    chunk = min(GDN_CHUNK, seq_len)
    assert seq_len % chunk == 0 and chunk % SUBLANES == 0
    n_tok = n_seq * seq_len
    stacked = seq_len == SUBLANES and n_seq % GDN_STEP_SEQS == 0
    if stacked:
        n_step_chunks, sb, nt = GDN_STEP_SEQS, GDN_STEP_SEQS, 1
    else:
        n_step_chunks = GDN_STEP_CHUNKS if (seq_len // chunk) % GDN_STEP_CHUNKS == 0 else 1
        sb, nt = 1, seq_len // (n_step_chunks * chunk)
    step_rows = n_step_chunks * chunk
    tb = step_rows // sb
    row = lambda s, t: s * nt + t
    ada = pl.BlockSpec((sb, 1, d), lambda s, t: (s, 0, 0))
    extra = w_proj.shape[1] - (QKV_W + d)
    extra_spec = [pl.BlockSpec((step_rows, extra), lambda s, t: (row(s, t), 0))] if extra else []
    extra_shape = [jax.ShapeDtypeStruct((n_tok, extra), F32)] if extra else []
    return pl.pallas_call(
        functools.partial(_gdn_body, chunk=chunk, n_chunks=n_step_chunks),
        grid=(n_seq // sb, nt),
        in_specs=[pl.BlockSpec((sb, tb, d), lambda s, t: (s, t, 0)),
                  ada, ada, _resident((1, d)),
                  _resident(w_proj.shape), _resident((d, N_GATE_COLS)), _resident((N_GATE_COLS, d)),
                  _resident((2, N_GATE_COLS)), _resident((N_GATE_COLS, 2)),
                  pl.BlockSpec((sb, SUBLANES, QKV_W), lambda s, t: (s, 0, 0)),
                  pl.BlockSpec((sb, GDN_HEADS, GDN_DK, GDN_DV), lambda s, t: (s, 0, 0, 0)),
                  _resident((GDN_CONV, QKV_W)), _resident((1, GDN_DV))],
        out_specs=[pl.BlockSpec((step_rows, d), lambda s, t: (row(s, t), 0)),
                   pl.BlockSpec((sb, GDN_HEADS, GDN_DK, GDN_DV), lambda s, t: (s, 0, 0, 0)),
                   pl.BlockSpec((sb, SUBLANES, QKV_W), lambda s, t: (s, 0, 0))] + extra_spec,
        out_shape=[jax.ShapeDtypeStruct((n_tok, d), F32),
                   jax.ShapeDtypeStruct((n_seq, GDN_HEADS, GDN_DK, GDN_DV), F32),
                   jax.ShapeDtypeStruct((n_seq, SUBLANES, QKV_W), F32)] + extra_shape,
        scratch_shapes=[pltpu.VMEM((SUBLANES, QKV_W), F32)],
        compiler_params=_params(("arbitrary", "arbitrary")),
        name="gdn",
    )(x, sc, sh, norm1_w, w_proj, w_g, w_gt, p_row, p_col, conv0, state0, conv_w, norm_w)


def _merge_body(x_ref, on_ref, *refs, per_seq, n_tiles, projected):
    n_head = 5 if projected else 4
    (sc0_ref, g1_ref, sc2_ref, sh2_ref, wa_ref, wb_ref, wo_ref, cw_ref, n2_ref, wr_ref, br_ref, cnt0_ref,
     x1_ref, h2_ref, pos_ref, rw_ref, cnt_ref, tail_ref, post_ref, tbl_ref, cnt_scr, hist_scr) = refs[n_head:]
    step = pl.program_id(0) * pl.num_programs(1) + pl.program_id(1)

    @pl.when(step == 0)
    def _():
        cnt_scr[...] = cnt0_ref[...]

    tm, d = on_ref.shape
    if projected:
        sc_b, sc_c, sc_h, gate_a, gate_b = (r[...] for r in refs[:n_head])
    else:
        sc1_ref, sh1_ref, n1_ref, w5_ref = refs[:n_head]
        p5 = jnp.dot(_ada_norm(x_ref, sc1_ref, sh1_ref, n1_ref), w5_ref[...], preferred_element_type=F32)
        sc_b, sc_c, sc_h, gate_a, gate_b = (p5[:, j * d:(j + 1) * d] for j in range(5))
    pre = sc_c * sc_h
    if per_seq:
        hist = sc0_ref[...].reshape(tm, d)
        tail_ref[...] = pre
    else:
        @pl.when(pl.program_id(1) == 0)
        def _():
            hist_scr[...] = sc0_ref[0]

        hist = hist_scr[...]
        hist_scr[...] = pre[tm - SUBLANES:, :]
        tail_ref[...] = pre[tm - SUBLANES:, :]
    y_b = _bdot(sc_b * _causal_conv(pre, hist, cw_ref[...], per_seq), wb_ref[...])
    y_a = _bdot(on_ref[...], wa_ref[...])
    merged = jax.nn.sigmoid(gate_a) * y_a + jax.nn.sigmoid(gate_b) * y_b
    mo = _bdot(merged, wo_ref[...]).reshape(x_ref.shape)
    x1 = x_ref[...] + g1_ref[...] * mo
    y = x1 * lax.rsqrt(jnp.mean(x1 * x1, axis=-1, keepdims=True) + NORM_EPS) * n2_ref[...]
    h2 = (y * (1.0 + sc2_ref[...]) + sh2_ref[...]).reshape(tm, d)
    x1_ref[...] = x1.reshape(tm, d)
    h2_ref[...] = h2

    h2_hi = h2.astype(BF16)
    h2_lo = (h2 - h2_hi.astype(F32)).astype(BF16)
    w_hi = wr_ref[0]
    logits = (jnp.dot(h2_hi, w_hi, preferred_element_type=F32)
              + (jnp.dot(h2_hi, wr_ref[1], preferred_element_type=F32)
                 + jnp.dot(h2_lo, w_hi, preferred_element_type=F32))) + br_ref[...]
    lane = lax.broadcasted_iota(I32, logits.shape, 1)
    lane_f = lane.astype(F32)
    work = logits
    vals, hots = [], []
    member = jnp.zeros(logits.shape, F32)
    for _ in range(TOP_K):
        m = jnp.max(work, axis=-1, keepdims=True)
        sel = jnp.min(jnp.where(work == m, lane_f, float(N_EXPERTS - 1)), axis=-1, keepdims=True)
        hot = lane_f == sel
        vals.append(m)
        hots.append(hot)
        member = member + hot.astype(F32)
        work = jnp.where(hot, -jnp.inf, work)
    exps = [jnp.exp(v - vals[0]) for v in vals]
    denom = exps[0] + exps[1] + exps[2] + exps[3]
    ti = lax.broadcasted_iota(I32, (tm, tm), 0)
    tj = lax.broadcasted_iota(I32, (tm, tm), 1)
    before = (tj < ti).astype(BF16)
    rank_loc = jnp.dot(before, member.astype(BF16), preferred_element_type=F32)
    cnt = jnp.sum(member, axis=0, keepdims=True).astype(I32)
    seg = lax.shift_left(lax.shift_right_logical(cnt + (SUBLANES - 1), 3), 3)
    seg8 = jnp.broadcast_to(seg.astype(F32), (SUBLANES, ROUTER_LANES))
    ei = lax.broadcasted_iota(I32, (ROUTER_LANES, ROUTER_LANES), 0)
    ej = lax.broadcasted_iota(I32, (ROUTER_LANES, ROUTER_LANES), 1)
    base8 = jnp.dot(seg8, (ei < ej).astype(F32), precision=HIGHEST, preferred_element_type=F32)
    row_all = base8[0:1, :] + rank_loc
    pos = jnp.zeros((tm, ROUTER_LANES), F32)
    rw = jnp.zeros((tm, ROUTER_LANES), F32)
    for kk in range(TOP_K):
        pos_k = jnp.sum(jnp.where(hots[kk], row_all, 0.0), axis=-1, keepdims=True)
        pos = jnp.where(lane == kk, pos_k, pos)
        rw = jnp.where(lane == kk, exps[kk] / denom, rw)
    pos_ref[...] = pos
    rw_ref[...] = rw
    pick = (lax.broadcasted_iota(I32, (SUBLANES, ROUTER_LANES), 0)
            == lax.broadcasted_iota(I32, (SUBLANES, ROUTER_LANES), 1)).astype(F32)
    post_ref[...] = lax.dot_general(pick, pos, _NT, precision=HIGHEST, preferred_element_type=F32)
    sub = lax.broadcasted_iota(I32, (SUBLANES, ROUTER_LANES), 0)
    tbl_ref[...] = jnp.where(sub == 0, base8.astype(I32),
                             jnp.where(sub == 1, seg8.astype(I32), cnt_scr[...].astype(I32)))
    cnt_scr[...] = cnt_scr[...] + seg8

    @pl.when(step == n_tiles - 1)
    def _():
        cnt_ref[...] = cnt_scr[...]


def _merge(x, on, p5, sc1, sh1, norm1_w, w5, sc0, g1, sc2, sh2, w_a, w_b, w_o, conv_w, norm2_w, w_r, b_r, cnt0):
    n_seq, seq_len, d = x.shape
    sb, tb = _route_tile(n_seq, seq_len)
    per_seq = sb > 1 or tb == SUBLANES
    if per_seq:
        assert tb == SUBLANES
    tm = sb * tb
    nt = seq_len // tb
    ns = n_seq // sb
    n_tok = n_seq * seq_len
    row = lambda s, t: s * nt + t
    full = lambda shape: pl.BlockSpec(shape, lambda s, t: (0,) * len(shape))
    ada = pl.BlockSpec((sb, 1, d), lambda s, t: (s, 0, 0))
    tok = pl.BlockSpec((tm, d), lambda s, t: (row(s, t), 0))
    lanes = pl.BlockSpec((tm, ROUTER_LANES), lambda s, t: (row(s, t), 0))
    tail_rows = tm if per_seq else SUBLANES
    if p5 is None:
        head_specs = [ada, ada, _resident((1, d)), _resident((d, 5 * d))]
        head_args = [sc1, sh1, norm1_w, w5]
    else:
        head_specs = [pl.BlockSpec((tm, d), lambda s, t, j=j: (row(s, t), j)) for j in range(5)]
        head_args = [p5] * 5
    return pl.pallas_call(
        functools.partial(_merge_body, per_seq=per_seq, n_tiles=ns * nt, projected=p5 is not None),
        grid=(ns, nt),
        in_specs=[pl.BlockSpec((sb, tb, d), lambda s, t: (s, t, 0)),
                  tok] + head_specs + [
                  pl.BlockSpec((sb, SUBLANES, d), lambda s, t: (s, 0, 0)),
                  ada, ada, ada,
                  _resident((d, d)), _resident((d, d)), _resident((d, d)),
                  _resident((SC_CONV, d)), _resident((1, d)), _resident((2, d, ROUTER_LANES)),
                  _resident((1, ROUTER_LANES)), _resident((SUBLANES, ROUTER_LANES))],
        out_specs=[tok, tok, lanes, lanes, full((SUBLANES, ROUTER_LANES)),
                   pl.BlockSpec((tail_rows, d), lambda s, t: (row(s, t), 0)),
                   pl.BlockSpec((SUBLANES, tm), lambda s, t: (0, row(s, t))),
                   pl.BlockSpec((SUBLANES, ROUTER_LANES), lambda s, t: (row(s, t), 0))],
        out_shape=[jax.ShapeDtypeStruct((n_tok, d), F32),
                   jax.ShapeDtypeStruct((n_tok, d), F32),
                   jax.ShapeDtypeStruct((n_tok, ROUTER_LANES), F32),
                   jax.ShapeDtypeStruct((n_tok, ROUTER_LANES), F32),
                   jax.ShapeDtypeStruct((SUBLANES, ROUTER_LANES), F32),
                   jax.ShapeDtypeStruct((ns * nt * tail_rows, d), F32),
                   jax.ShapeDtypeStruct((SUBLANES, n_tok), F32),
                   jax.ShapeDtypeStruct((ns * nt * SUBLANES, ROUTER_LANES), I32)],
        scratch_shapes=[pltpu.VMEM((SUBLANES, ROUTER_LANES), F32), pltpu.VMEM((SUBLANES, d), F32)],
        compiler_params=_params(("arbitrary", "arbitrary")),
        name="merge",
    )(x, on, *head_args, sc0, g1, sc2, sh2, w_a, w_b, w_o, conv_w, norm2_w, w_r, b_r, cnt0)


def _pow2_pieces(length, max_rows):
    out = []
    rows = max_rows
    while rows >= SUBLANES:
        shift = rows.bit_length()
        offset = lax.shift_left(lax.shift_right_logical(length, shift), shift)
        out.append(((length & rows) != 0, offset, rows))
        rows //= 2
    return out


def _segment_copies(tbl_ref, tile_id, local_ref, global_ref, sem, to_global, max_rows):
    out = []
    base = tile_id * (3 * N_EXPERTS)
    for e in range(N_EXPERTS):
        local0 = tbl_ref[base + e]
        length = tbl_ref[base + N_EXPERTS + e]
        global0 = tbl_ref[base + 2 * N_EXPERTS + e]
        for pred, offset, rows in _pow2_pieces(length, max_rows):
            loc = local_ref.at[pl.ds(pl.multiple_of(local0 + offset, SUBLANES), rows)]
            glo = global_ref.at[pl.ds(pl.multiple_of(global0 + offset, SUBLANES), rows)]
            cp = pltpu.make_async_copy(loc, glo, sem) if to_global else pltpu.make_async_copy(glo, loc, sem)
            out.append((pred, cp))
    return out


def _wait_rows(total, local_ref, global_ref, sem, to_global):
    top = 1 << (local_ref.shape[0].bit_length() - 1)
    for pred, _, rows in _pow2_pieces(total, top):
        loc, glo = local_ref.at[pl.ds(0, rows)], global_ref.at[pl.ds(0, rows)]
        cp = pltpu.make_async_copy(loc, glo, sem) if to_global else pltpu.make_async_copy(glo, loc, sem)
        pl.when(pred)(cp.wait)


def _start_all(copies):
    for pred, cp in copies:
        pl.when(pred)(cp.start)


def _wait_all(copies):
    for pred, cp in copies:
        pl.when(pred)(cp.wait)


def _dispatch_body(tbl_ref, tails_ref, post_ref, h2_ref, *refs, tile0, totals_at, zero_tails):
    buf_ref, sorted_scr, zero_scr, sems = refs[-4:]
    i = pl.program_id(0)
    slot = lax.rem(i, 2)
    tm = h2_ref.shape[0]
    r_rows = sorted_scr.shape[1]

    if zero_tails:
        @pl.when(i == 0)
        def _():
            zero_scr[...] = jnp.zeros(zero_scr.shape, U32)
            z_rows = zero_scr.shape[0]

            def fill(b, carry):
                cp = pltpu.make_async_copy(zero_scr, buf_ref.at[pl.ds(pl.multiple_of(b * SUBLANES, SUBLANES), z_rows)],
                                           sems.at[2])
                cp.start()
                cp.wait()
                return carry

            tails = []
            for e in range(N_EXPERTS):
                start, length = tails_ref[e], tails_ref[N_EXPERTS + e]
                whole = length // z_rows
                lax.fori_loop(0, whole, lambda b, c, s=start: fill(s // SUBLANES + b * (z_rows // SUBLANES), c), 0)
                rest0 = start + whole * z_rows
                for pred, offset, rows in _pow2_pieces(length - whole * z_rows, z_rows // 2):
                    at = pl.multiple_of(rest0 + offset, SUBLANES)
                    tails.append((pred, pltpu.make_async_copy(zero_scr.at[pl.ds(0, rows)],
                                                              buf_ref.at[pl.ds(at, rows)], sems.at[2])))
            _start_all(tails)
            _wait_all(tails)
            lax.fori_loop(tails_ref[2 * N_EXPERTS] // z_rows, buf_ref.shape[0] // z_rows,
                          lambda b, c: fill(b * (z_rows // SUBLANES), c), 0)

    pos = post_ref[...].astype(I32)
    j = lax.broadcasted_iota(I32, (r_rows, tm), 0)
    onehot = jnp.zeros((r_rows, tm), F32)
    for k in range(TOP_K):
        onehot = jnp.where(j == pos[k:k + 1, :], 1.0, onehot)
    sorted_scr[slot] = _pack_halves(_bdot(onehot, h2_ref[...]))

    _start_all(_segment_copies(tbl_ref, tile0 + i, sorted_scr.at[slot], buf_ref, sems.at[slot], True, tm))

    @pl.when(i > 0)
    def _():
        _wait_rows(tbl_ref[totals_at + tile0 + i - 1], sorted_scr.at[1 - slot], buf_ref, sems.at[1 - slot], True)

    @pl.when(i == pl.num_programs(0) - 1)
    def _():
        _wait_rows(tbl_ref[totals_at + tile0 + i], sorted_scr.at[slot], buf_ref, sems.at[slot], True)


def _dispatch(tbl, tails, post, h2, buf, buf_rows, tm, tile0, totals_at):
    n_tok, d = h2.shape
    r_rows = TOP_K * tm + N_EXPERTS * SUBLANES
    first = buf is None
    grid_spec = pltpu.PrefetchScalarGridSpec(
        num_scalar_prefetch=2,
        grid=(n_tok // tm,),
        in_specs=[pl.BlockSpec((SUBLANES, tm), lambda i, *_: (0, i)),
                  pl.BlockSpec((tm, d), lambda i, *_: (i, 0))]
                 + ([] if first else [pl.BlockSpec(memory_space=pl.ANY)]),
        out_specs=pl.BlockSpec(memory_space=pl.ANY),
        scratch_shapes=[pltpu.VMEM((2, r_rows, d // 2), U32),
                        pltpu.VMEM((EXPERT_BLOCK // 2, d // 2), U32),
                        pltpu.SemaphoreType.DMA((3,))],
    )
    return pl.pallas_call(
        functools.partial(_dispatch_body, tile0=tile0, totals_at=totals_at, zero_tails=first),
        grid_spec=grid_spec,
        out_shape=jax.ShapeDtypeStruct((buf_rows, d // 2), U32),
        input_output_aliases={} if first else {4: 0},
        compiler_params=pltpu.CompilerParams(dimension_semantics=("arbitrary",), has_side_effects=True,
                                             vmem_limit_bytes=VMEM_LIMIT),
        name="dispatch",
    )(tbl, tails, post, h2, *([] if first else [buf]))


def _expert_body(be_ref, live_ref, nv_ref, x_ref, wg_ref, bg_ref, wu_ref, bu_ref, wd_ref, bd_ref, o_ref,
                 wg_s, wu_s, wd_s):
    i = pl.program_id(0)
    half = EXPERT_BLOCK // 2

    def ffn(rows):
        x = _unpack_halves(x_ref[:rows, :])
        gate = jnp.dot(x, wg_s[...], preferred_element_type=F32) + bg_ref[0]
        up = jnp.dot(x, wu_s[...], preferred_element_type=F32) + bu_ref[0]
        gate = jnp.minimum(gate, SWIGLU_LIMIT)
        up = jnp.clip(up, -SWIGLU_LIMIT, SWIGLU_LIMIT)
        glu = gate * jax.nn.sigmoid(SWIGLU_ALPHA * gate)
        out = _bdot((up + 1.0) * glu, wd_s[...]) + bd_ref[0]
        o_ref[:rows, :] = _pack_halves(out.astype(BF16).astype(F32))

    @pl.when(i < nv_ref[0])
    def _():
        @pl.when((i == 0) | (be_ref[i] != be_ref[jnp.maximum(i - 1, 0)]))
        def _():
            wg_s[...] = wg_ref[0].astype(BF16)
            wu_s[...] = wu_ref[0].astype(BF16)
            wd_s[...] = wd_ref[0].astype(BF16)

        @pl.when(live_ref[i] > half)
        def _():
            ffn(EXPERT_BLOCK)

        @pl.when(live_ref[i] <= half)
        def _():
            ffn(half)
            o_ref[half:, :] = jnp.zeros((EXPERT_BLOCK - half, o_ref.shape[1]), U32)

    @pl.when(i >= nv_ref[0])
    def _():
        o_ref[...] = jnp.zeros(o_ref.shape, U32)


def _experts(block_e, block_live, n_valid, xs, w_gate, b_gate, w_up, b_up, w_down, b_down):
    m_pad, d_half = xs.shape
    d = 2 * d_half
    n_blocks = m_pad // EXPERT_BLOCK
    f = w_gate.shape[2]
    blk = lambda i, be, lv, nv: (jnp.maximum(jnp.minimum(i, nv[0] - 1), 0), 0)
    wspec = lambda a, b: pl.BlockSpec((1, a, b), lambda i, be, lv, nv: (be[i], 0, 0))
    grid_spec = pltpu.PrefetchScalarGridSpec(
        num_scalar_prefetch=3,
        grid=(n_blocks,),
        in_specs=[pl.BlockSpec((EXPERT_BLOCK, d_half), blk),
                  wspec(d, f), wspec(1, f), wspec(d, f), wspec(1, f), wspec(f, d), wspec(1, d)],
        out_specs=pl.BlockSpec((EXPERT_BLOCK, d_half), lambda i, be, lv, nv: (i, 0)),
        scratch_shapes=[pltpu.VMEM((d, f), BF16), pltpu.VMEM((d, f), BF16), pltpu.VMEM((f, d), BF16)],
    )
    return pl.pallas_call(
        _expert_body,
        grid_spec=grid_spec,
        out_shape=jax.ShapeDtypeStruct((m_pad, d_half), U32),
        compiler_params=_params(("arbitrary",)),
        name="experts",
    )(block_e, block_live, n_valid, xs, w_gate, b_gate[:, None, :], w_up, b_up[:, None, :], w_down,
      b_down[:, None, :])


def _combine_body(tbl_ref, ys_ref, pos_ref, rw_ref, x1_ref, g2_ref, fw_ref, o_ref, blk_scr, sems,
                  *, tile0, totals_at):
    step = pl.program_id(0) * pl.num_programs(1) + pl.program_id(1)
    n_steps = pl.num_programs(0) * pl.num_programs(1)
    slot = lax.rem(step, 2)
    tm = pos_ref.shape[0]
    r_rows = blk_scr.shape[1]

    def fetch(tile, into):
        return _segment_copies(tbl_ref, tile0 + tile, blk_scr.at[into], ys_ref, sems.at[into], False, tm)

    @pl.when(step == 0)
    def _():
        blk_scr[...] = jnp.zeros(blk_scr.shape, U32)
        _start_all(fetch(0, 0))

    @pl.when(step + 1 < n_steps)
    def _():
        _start_all(fetch(step + 1, 1 - slot))

    _wait_rows(tbl_ref[totals_at + tile0 + step], blk_scr.at[slot], ys_ref, sems.at[slot], False)

    pos = pos_ref[...].astype(I32)
    rw = rw_ref[...]
    j = lax.broadcasted_iota(I32, (tm, r_rows), 1)
    pw = jnp.zeros((tm, r_rows), F32)
    for k in range(TOP_K):
        pw = jnp.where(j == pos[:, k:k + 1], rw[:, k:k + 1], pw)
    ffn = jnp.dot(pw.astype(BF16), _unpack_halves(blk_scr[slot]), preferred_element_type=F32)
    y = x1_ref[...] + g2_ref[...] * ffn.reshape(x1_ref.shape)
    o_ref[...] = y * lax.rsqrt(jnp.mean(y * y, axis=-1, keepdims=True) + NORM_EPS) * fw_ref[...]


def _combine(tbl, ys, pos, rw, x1, g2, final_w, sb, tb, tile0, totals_at):
    n_seq, seq_len, d = x1.shape
    tm = sb * tb
    nt = seq_len // tb
    r_rows = TOP_K * tm + N_EXPERTS * SUBLANES
    row = lambda s, t: s * nt + t
    grid_spec = pltpu.PrefetchScalarGridSpec(
        num_scalar_prefetch=1,
        grid=(n_seq // sb, nt),
        in_specs=[pl.BlockSpec(memory_space=pl.ANY),
                  pl.BlockSpec((tm, ROUTER_LANES), lambda s, t, *_: (row(s, t), 0)),
                  pl.BlockSpec((tm, ROUTER_LANES), lambda s, t, *_: (row(s, t), 0)),
                  pl.BlockSpec((sb, tb, d), lambda s, t, *_: (s, t, 0)),
                  pl.BlockSpec((sb, 1, d), lambda s, t, *_: (s, 0, 0)),
                  pl.BlockSpec((1, d), lambda s, t, *_: (0, 0))],
        out_specs=pl.BlockSpec((sb, tb, d), lambda s, t, *_: (s, t, 0)),
        scratch_shapes=[pltpu.VMEM((2, r_rows, d // 2), U32), pltpu.SemaphoreType.DMA((2,))],
    )
    return pl.pallas_call(
        functools.partial(_combine_body, tile0=tile0, totals_at=totals_at),
        grid_spec=grid_spec,
        out_shape=jax.ShapeDtypeStruct((n_seq, seq_len, d), F32),
        compiler_params=_params(("arbitrary", "arbitrary")),
        name="combine",
    )(tbl, ys, pos, rw, x1, g2, final_w)


def _pad_state(state, rows):
    return jnp.pad(state, ((0, 0), (rows - state.shape[1], 0), (0, 0)))


def kernel(x_prompt, x_sample, c_prompt, c_sample, state_gdn_conv, state_gdn_rec, state_sc_conv, w_ada, b_ada,
           norm1_w, w_in, gdn_conv_w, gdn_a_log, gdn_dt_bias, gdn_norm_w, w_branch_a, sc_conv_w, w_branch_b,
           w_out, norm2_w, w_router, b_router, w_gate, b_gate, w_up, b_up, w_down, b_down, final_norm_w):
    assert w_ada.shape[0] == 1, "single-layer trunk"
    d = D_MODEL
    bp, tp, _ = x_prompt.shape
    bs, ts, _ = x_sample.shape
    n_p, n_s = bp * tp, bs * ts
    n_tok = n_p + n_s

    w_in0 = w_in[0]
    g_lo, g_hi = QKV_W + d, QKV_W + d + N_GATE_COLS
    w_qkvz = w_in0[:, :g_lo].astype(BF16)
    w5 = w_in0[:, g_hi:].astype(BF16)
    w_all = jnp.concatenate([w_qkvz, w5], axis=1)
    w_g = w_in0[:, g_lo:g_hi].astype(BF16)
    w_gt = w_g.T
    zeros_h = jnp.zeros((GDN_HEADS,), F32)
    p_row = jnp.stack([jnp.concatenate([zeros_h, gdn_a_log[0]]), jnp.concatenate([zeros_h, gdn_dt_bias[0]])])
    p_col = p_row.T
    w_a = w_branch_a[0].astype(BF16)
    w_b = w_branch_b[0].astype(BF16)
    w_o = w_out[0].astype(BF16)
    w_r = jnp.pad(w_router[0], ((0, 0), (0, ROUTER_LANES - N_EXPERTS)))
    w_r_hi = w_r.astype(BF16)
    w_r = jnp.stack([w_r_hi, (w_r - w_r_hi.astype(F32)).astype(BF16)])
    b_r = jnp.pad(b_router[0], (0, ROUTER_LANES - N_EXPERTS), constant_values=-jnp.inf).reshape(1, ROUTER_LANES)

    n_c = bp + bs
    c_rows = -(-n_c // 16) * 16
    c_all = jnp.pad(jnp.concatenate([c_prompt, c_sample], axis=0), ((0, c_rows - n_c), (0, 0)))
    ada = _ada(c_all, w_ada[0], b_ada[0])

    def ada_parts(lo, hi):
        return [ada[lo:hi, j * d:(j + 1) * d].reshape(hi - lo, 1, d) for j in range(6)]

    groups = [
        dict(x=x_prompt, ada=ada_parts(0, bp), n_seq=bp, seq_len=tp,
             conv0=jnp.zeros((bp, SUBLANES, QKV_W), F32),
             rec0=jnp.zeros((bp, GDN_HEADS, GDN_DK, GDN_DV), F32),
             sc0=jnp.zeros((bp, SUBLANES, d), F32)),
        dict(x=x_sample, ada=ada_parts(bp, n_c), n_seq=bs, seq_len=ts,
             conv0=_pad_state(state_gdn_conv[0], SUBLANES),
             rec0=state_gdn_rec[0],
             sc0=_pad_state(state_sc_conv[0], SUBLANES)),
    ]

    counts = jnp.zeros((SUBLANES, ROUTER_LANES), F32)
    for g in groups:
        sh1, sc1, g1, sh2, sc2, g2 = g["ada"]
        n_seq, seq_len = g["n_seq"], g["seq_len"]
        wide = seq_len > SUBLANES
        on, rec, conv_tail, *p5 = _gdn(g["x"], sc1, sh1, norm1_w, w_all if wide else w_qkvz, w_g, w_gt, p_row, p_col,
                                       g["conv0"], g["rec0"], gdn_conv_w[0], gdn_norm_w)
        x1, h2, pos, rw, counts, tail, post, tbl = _merge(g["x"], on, p5[0] if wide else None, sc1, sh1, norm1_w, w5,
                                                          g["sc0"], g1, sc2, sh2, w_a, w_b, w_o, sc_conv_w[0],
                                                          norm2_w, w_r, b_r, counts)
        sb, tb = _route_tile(n_seq, seq_len)
        g.update(x1=x1, h2=h2, pos=pos, rw=rw, post=post, tbl=tbl, rec=rec, g2=g2, sb=sb, tb=tb, tm=sb * tb,
                 rows_after=counts[0, :N_EXPERTS].astype(I32),
                 new_conv=conv_tail[:, SUBLANES - (GDN_CONV - 1):, :],
                 new_sc=tail.reshape(n_seq, -1, d)[:, -(SC_CONV - 1):, :])

    rows_e = counts[0, :N_EXPERTS].astype(I32)
    padded = (rows_e + EXPERT_BLOCK - 1) // EXPERT_BLOCK * EXPERT_BLOCK
    pad_end = jnp.cumsum(padded)
    expert0 = (pad_end - padded).astype(I32)
    tbl_all = jnp.concatenate([g["tbl"] for g in groups], axis=0).reshape(-1, SUBLANES, ROUTER_LANES)
    n_tiles = tbl_all.shape[0]
    tbl_all = tbl_all[:, :3, :N_EXPERTS].at[:, 2, :].add(expert0[None, :])
    totals_at = n_tiles * 3 * N_EXPERTS
    tbl_all = jnp.concatenate([tbl_all.reshape(-1), jnp.sum(tbl_all[:, 1, :], axis=1)]).astype(I32)
    rows_first = groups[0]["rows_after"]
    tails = jnp.concatenate([expert0 + rows_first, padded - rows_first, pad_end[-1:]]).astype(I32)
    max_rows = n_tok * TOP_K + n_tiles * N_EXPERTS * (SUBLANES - 1)
    n_blocks = -(-max_rows // EXPERT_BLOCK) + N_EXPERTS
    block_start = jnp.arange(n_blocks, dtype=I32) * EXPERT_BLOCK
    block_e = jnp.minimum(jnp.sum((pad_end[None, :] <= block_start[:, None]).astype(I32), axis=1), N_EXPERTS - 1)
    n_valid = (pad_end[-1:] // EXPERT_BLOCK).astype(I32)
    block_live = jnp.clip((expert0 + rows_e)[block_e] - block_start, 0, EXPERT_BLOCK).astype(I32)

    buf = None
    tile0 = 0
    for g in groups:
        g["tile0"] = tile0
        buf = _dispatch(tbl_all, tails, g["post"], g["h2"], buf, n_blocks * EXPERT_BLOCK, g["tm"], tile0, totals_at)
        tile0 += g["h2"].shape[0] // g["tm"]
    ys = _experts(block_e, block_live, n_valid, buf, w_gate[0], b_gate[0], w_up[0], b_up[0], w_down[0], b_down[0])
    outs = []
    for g in groups:
        n_seq, seq_len = g["n_seq"], g["seq_len"]
        x1 = g["x1"].reshape(n_seq, seq_len, d)
        outs.append(_combine(tbl_all, ys, g["pos"], g["rw"], x1, g["g2"], final_norm_w.reshape(1, d),
                             g["sb"], g["tb"], g["tile0"], totals_at))

    gp, gs = groups
    return (outs[0], outs[1], gp["new_conv"][None], gp["rec"][None], gp["new_sc"][None],
            gs["new_conv"][None], gs["rec"][None], gs["new_sc"][None])
```

```python
import functools

import jax
import jax.numpy as jnp
from jax import lax
from jax.experimental import pallas as pl
from jax.experimental.pallas import tpu as pltpu

F32 = jnp.float32
BF16 = jnp.bfloat16
I32 = jnp.int32
U32 = jnp.uint32
HIGHEST = lax.Precision.HIGHEST

D_MODEL = 1024
GDN_HEADS = 8
GDN_DK = 128
GDN_DV = 128
GDN_QK = GDN_HEADS * GDN_DK
QKV_W = 3 * GDN_QK
GDN_CONV = 4
GDN_CHUNK = 128
GDN_STEP_CHUNKS = 2
GDN_STEP_SEQS = 8
SC_CONV = 3
N_EXPERTS = 32
TOP_K = 4
SWIGLU_LIMIT = 7.0
SWIGLU_ALPHA = 1.702
NORM_EPS = 1e-6
N_GATE_COLS = 2 * GDN_HEADS
ROUTER_LANES = 128
EXPERT_BLOCK = 512
ROUTE_TILE = 512
ADA_TILE = 1024
SUBLANES = 8
VMEM_LIMIT = 56 * 1024 * 1024

_NT = (((1,), (1,)), ((), ()))
_TN = (((0,), (0,)), ((), ()))


def _bdot(a, b):
    return jnp.dot(a.astype(BF16), b.astype(BF16), preferred_element_type=F32)


def _bdot_nt(a, b):
    return lax.dot_general(a.astype(BF16), b.astype(BF16), _NT, preferred_element_type=F32)


def _bdot_tn(a, b):
    return lax.dot_general(a.astype(BF16), b.astype(BF16), _TN, preferred_element_type=F32)


def _pack_halves(x):
    w = x.shape[1] // 2
    lo = lax.bitcast_convert_type(x[:, :w], U32)
    hi = lax.bitcast_convert_type(x[:, w:], U32)
    return lax.shift_right_logical(lo, jnp.uint32(16)) | (hi & jnp.uint32(0xFFFF0000))


def _unpack_halves(p):
    lo = lax.bitcast_convert_type(lax.shift_left(p, jnp.uint32(16)), F32)
    hi = lax.bitcast_convert_type(p & jnp.uint32(0xFFFF0000), F32)
    return jnp.concatenate([lo, hi], axis=1).astype(BF16)


def _silu(x):
    return x * jax.nn.sigmoid(x)


def _softplus(x):
    return jnp.maximum(x, 0.0) + jnp.log1p(jnp.exp(-jnp.abs(x)))


def _seq_tile(n_seq, seq_len, target):
    if seq_len >= target:
        assert seq_len % target == 0
        return 1, target
    sb = min(n_seq, target // seq_len)
    assert n_seq % sb == 0 and seq_len % SUBLANES == 0
    return sb, seq_len


def _route_tile(n_seq, seq_len):
    return _seq_tile(n_seq, seq_len, ROUTE_TILE)


def _params(sem):
    return pltpu.CompilerParams(dimension_semantics=sem, vmem_limit_bytes=VMEM_LIMIT)


def _ada_body(c_ref, w_ref, b_ref, o_ref):
    o_ref[...] = _bdot(_silu(c_ref[...]), w_ref[...]) + b_ref[...]


def _ada(c, w_ada, b_ada):
    rows, d = c.shape
    n = w_ada.shape[1]
    tn = ADA_TILE
    return pl.pallas_call(
        _ada_body,
        grid=(n // tn,),
        in_specs=[pl.BlockSpec((rows, d), lambda j: (0, 0)),
                  pl.BlockSpec((d, tn), lambda j: (0, j)),
                  pl.BlockSpec((1, tn), lambda j: (0, j))],
        out_specs=pl.BlockSpec((rows, tn), lambda j: (0, j)),
        out_shape=jax.ShapeDtypeStruct((rows, n), F32),
        compiler_params=_params(("arbitrary",)),
        name="ada",
    )(c, w_ada, b_ada.reshape(1, n))


def _gates(v, a_log, dt_bias, axis):
    is_beta = lax.broadcasted_iota(I32, v.shape, axis) < GDN_HEADS
    beta = jax.nn.sigmoid(v)
    g = -jnp.exp(a_log) * _softplus(v + dt_bias)
    return jnp.where(is_beta, beta, g)


def _ada_norm(x_ref, sc_ref, sh_ref, nw_ref):
    x = x_ref[...]
    y = x * lax.rsqrt(jnp.mean(x * x, axis=-1, keepdims=True) + NORM_EPS) * nw_ref[...]
    h = y * (1.0 + sc_ref[...]) + sh_ref[...]
    return h.reshape(x.shape[0] * x.shape[1], x.shape[2]).astype(BF16)


def _resident(shape):
    return pl.BlockSpec(shape, lambda *_: (0,) * len(shape), pipeline_mode=pl.Buffered(1))


def _shift_rows(x, hist, s, per_seq):
    rows, width = x.shape
    x3 = x.reshape(rows // SUBLANES, SUBLANES, width)
    xr = pltpu.roll(x3, s, 1)
    if per_seq:
        src = pltpu.roll(hist.reshape(x3.shape), s, 1)
    else:
        hr = pltpu.roll(hist.reshape(1, SUBLANES, width), s, 1)
        src = hr if rows == SUBLANES else jnp.concatenate([hr, xr[:-1]], axis=0)
    sub = lax.broadcasted_iota(I32, x3.shape, 1)
    return jnp.where(sub < s, src, xr).reshape(rows, width)


def _causal_conv(x, hist, w, per_seq):
    taps = w.shape[0]
    acc = x * w[taps - 1:taps, :]
    for s in range(1, taps):
        acc = acc + _shift_rows(x, hist, s, per_seq) * w[taps - 1 - s:taps - s, :]
    return acc


def _gdn_body(x_ref, sc_ref, sh_ref, n1_ref, wq_ref, wg_ref, wgt_ref, prow_ref, pcol_ref, conv0_ref, s0_ref,
              cw_ref, nw_ref, o_ref, s_ref, conv_ref, *rest, chunk, n_chunks):
    stacked = s_ref.shape[0] > 1
    hist_scr = rest[-1]

    @pl.when(pl.program_id(1) == 0)
    def _():
        s_ref[...] = s0_ref[...]
        if not stacked:
            hist_scr[...] = conv0_ref[0]

    h_in = _ada_norm(x_ref, sc_ref, sh_ref, n1_ref)
    n_own = QKV_W + D_MODEL
    proj = jnp.dot(h_in, wq_ref[:, :n_own], preferred_element_type=F32)
    x = proj[:, :QKV_W]
    z = proj[:, QKV_W:]
    if len(rest) == 2:
        rest[0][...] = jnp.dot(h_in, wq_ref[:, n_own:], preferred_element_type=F32)
    if stacked:
        hist = conv0_ref[...].reshape(x.shape)
        conv_ref[...] = x.reshape(conv_ref.shape)
    else:
        hist = hist_scr[...]
        hist_scr[...] = x[x.shape[0] - SUBLANES:, :]
        conv_ref[0] = x[x.shape[0] - SUBLANES:, :]
    qkvc = _silu(_causal_conv(x, hist, cw_ref[...], stacked))

    gcol = _gates(jnp.dot(h_in, wg_ref[...], preferred_element_type=F32), prow_ref[0:1, :], prow_ref[1:2, :], 1)
    ri = lax.broadcasted_iota(I32, (chunk, chunk), 0)
    ci = lax.broadcasted_iota(I32, (chunk, chunk), 1)
    causal = ri >= ci
    strict = ri > ci
    tri = causal.astype(F32)
    eye = (ri == ci).astype(F32)
    n_lvl = chunk.bit_length() - 1
    blk = [lax.shift_right_logical(ri, l) == lax.shift_right_logical(ci, l) for l in range(1, n_lvl + 1)]
    pair = [blk[l] & jnp.logical_not(blk[l - 1]) for l in range(1, n_lvl)]

    units = [(c, h) for c in range(n_chunks) for h in range(GDN_HEADS)]
    every = range(len(units))
    q, k, v, beta, dcol, dlast, decay, edec = [], [], [], [], [], [], [], []
    for c in range(n_chunks):
        rows = slice(c * chunk, (c + 1) * chunk)
        dec_col = jnp.dot(tri, gcol[rows, GDN_HEADS:], precision=HIGHEST, preferred_element_type=F32)
        grow = _gates(lax.dot_general(wgt_ref[...], h_in[rows, :], _NT, preferred_element_type=F32),
                      pcol_ref[:, 0:1], pcol_ref[:, 1:2], 0)
        dec_row = lax.dot_general(grow[GDN_HEADS:, :], tri, _NT, precision=HIGHEST, preferred_element_type=F32)
        for h in range(GDN_HEADS):
            qh = qkvc[rows, h * GDN_DK:(h + 1) * GDN_DK]
            kh = qkvc[rows, GDN_QK + h * GDN_DK:GDN_QK + (h + 1) * GDN_DK]
            q.append(qh * (lax.rsqrt(jnp.sum(qh * qh, axis=-1, keepdims=True) + 1e-6) * (GDN_DK ** -0.5)))
            k.append(kh * lax.rsqrt(jnp.sum(kh * kh, axis=-1, keepdims=True) + 1e-6))
            v.append(qkvc[rows, 2 * GDN_QK + h * GDN_DV:2 * GDN_QK + (h + 1) * GDN_DV])
            beta.append(gcol[rows, h:h + 1])
            dcol.append(dec_col[:, h:h + 1])
            dlast.append(dec_col[chunk - 1:chunk, h:h + 1])
            decay.append(jnp.where(causal, jnp.exp(dcol[-1] - dec_row[h:h + 1, :]), 0.0))
            edec.append(jnp.exp(dcol[-1]))
    kk = [_bdot_nt(k[i], k[i]) for i in every]
    qk = [_bdot_nt(q[i], k[i]) * decay[i] for i in every]
    a = [jnp.where(strict, beta[i] * kk[i] * decay[i], 0.0) for i in every]
    inv = [eye - jnp.where(blk[0], a[i], 0.0) for i in every]
    for lower_left in pair:
        right = [_bdot(jnp.where(lower_left, a[i], 0.0), inv[i]) for i in every]
        inv = [inv[i] - _bdot(inv[i], right[i]) for i in every]
    sol = [_bdot(inv[i], jnp.concatenate([beta[i] * v[i], (beta[i] * edec[i]) * k[i]], axis=1)) for i in every]
    seq_of = (lambda c: c) if stacked else (lambda c: 0)
    waves = [list(every)] if stacked else [[c * GDN_HEADS + h for h in range(GDN_HEADS)] for c in range(n_chunks)]
    state = {(seq_of(c), h): s_ref[seq_of(c), h] for c, h in units}
    for wave in waves:
        key = {i: (seq_of(units[i][0]), units[i][1]) for i in wave}
        ws = {i: _bdot(jnp.concatenate([sol[i][:, GDN_DV:], q[i] * edec[i]], axis=0), state[key[i]]) for i in wave}
        u = {i: sol[i][:, :GDN_DV] - ws[i][:chunk] for i in wave}
        o = {i: ws[i][chunk:] + _bdot(qk[i], u[i]) for i in wave}
        upd = {i: _bdot_tn(k[i] * jnp.exp(dlast[i] - dcol[i]), u[i]) for i in wave}
        for i in wave:
            c, h = units[i]
            state[key[i]] = state[key[i]] * jnp.exp(dlast[i]) + upd[i]
            rows = slice(c * chunk, (c + 1) * chunk)
            on = o[i] * lax.rsqrt(jnp.mean(o[i] * o[i], axis=-1, keepdims=True) + NORM_EPS) * nw_ref[...]
            o_ref[rows, h * GDN_DV:(h + 1) * GDN_DV] = on * _silu(z[rows, h * GDN_DV:(h + 1) * GDN_DV])
    for (b, h), value in state.items():
        s_ref[b, h] = value


def _gdn(x, sc, sh, norm1_w, w_proj, w_g, w_gt, p_row, p_col, conv0, state0, conv_w, norm_w):
    n_seq, seq_len, d = x.shape
    chunk = min(GDN_CHUNK, seq_len)
    assert seq_len % chunk == 0 and chunk % SUBLANES == 0
    n_tok = n_seq * seq_len
    stacked = seq_len == SUBLANES and n_seq % GDN_STEP_SEQS == 0
    if stacked:
        n_step_chunks, sb, nt = GDN_STEP_SEQS, GDN_STEP_SEQS, 1
    else:
        n_step_chunks = GDN_STEP_CHUNKS if (seq_len // chunk) % GDN_STEP_CHUNKS == 0 else 1
        sb, nt = 1, seq_len // (n_step_chunks * chunk)
    step_rows = n_step_chunks * chunk
    tb = step_rows // sb
    row = lambda s, t: s * nt + t
    ada = pl.BlockSpec((sb, 1, d), lambda s, t: (s, 0, 0))
    extra = w_proj.shape[1] - (QKV_W + d)
    extra_spec = [pl.BlockSpec((step_rows, extra), lambda s, t: (row(s, t), 0))] if extra else []
    extra_shape = [jax.ShapeDtypeStruct((n_tok, extra), F32)] if extra else []
    return pl.pallas_call(
        functools.partial(_gdn_body, chunk=chunk, n_chunks=n_step_chunks),
        grid=(n_seq // sb, nt),
        in_specs=[pl.BlockSpec((sb, tb, d), lambda s, t: (s, t, 0)),
                  ada, ada, _resident((1, d)),
                  _resident(w_proj.shape), _resident((d, N_GATE_COLS)), _resident((N_GATE_COLS, d)),
                  _resident((2, N_GATE_COLS)), _resident((N_GATE_COLS, 2)),
                  pl.BlockSpec((sb, SUBLANES, QKV_W), lambda s, t: (s, 0, 0)),
                  pl.BlockSpec((sb, GDN_HEADS, GDN_DK, GDN_DV), lambda s, t: (s, 0, 0, 0)),
                  _resident((GDN_CONV, QKV_W)), _resident((1, GDN_DV))],
        out_specs=[pl.BlockSpec((step_rows, d), lambda s, t: (row(s, t), 0)),
                   pl.BlockSpec((sb, GDN_HEADS, GDN_DK, GDN_DV), lambda s, t: (s, 0, 0, 0)),
                   pl.BlockSpec((sb, SUBLANES, QKV_W), lambda s, t: (s, 0, 0))] + extra_spec,
        out_shape=[jax.ShapeDtypeStruct((n_tok, d), F32),
                   jax.ShapeDtypeStruct((n_seq, GDN_HEADS, GDN_DK, GDN_DV), F32),
                   jax.ShapeDtypeStruct((n_seq, SUBLANES, QKV_W), F32)] + extra_shape,
        scratch_shapes=[pltpu.VMEM((SUBLANES, QKV_W), F32)],
        compiler_params=_params(("arbitrary", "arbitrary")),
        name="gdn",
    )(x, sc, sh, norm1_w, w_proj, w_g, w_gt, p_row, p_col, conv0, state0, conv_w, norm_w)


def _merge_body(x_ref, on_ref, *refs, per_seq, n_tiles, projected):
    n_head = 5 if projected else 4
    (sc0_ref, g1_ref, sc2_ref, sh2_ref, wa_ref, wb_ref, wo_ref, cw_ref, n2_ref, wr_ref, br_ref, cnt0_ref,
     x1_ref, h2_ref, pos_ref, rw_ref, cnt_ref, tail_ref, post_ref, tbl_ref, cnt_scr, hist_scr) = refs[n_head:]
    step = pl.program_id(0) * pl.num_programs(1) + pl.program_id(1)

    @pl.when(step == 0)
    def _():
        cnt_scr[...] = cnt0_ref[...]

    tm, d = on_ref.shape
    if projected:
        sc_b, sc_c, sc_h, gate_a, gate_b = (r[...] for r in refs[:n_head])
    else:
        sc1_ref, sh1_ref, n1_ref, w5_ref = refs[:n_head]
        p5 = jnp.dot(_ada_norm(x_ref, sc1_ref, sh1_ref, n1_ref), w5_ref[...], preferred_element_type=F32)
        sc_b, sc_c, sc_h, gate_a, gate_b = (p5[:, j * d:(j + 1) * d] for j in range(5))
    pre = sc_c * sc_h
    if per_seq:
        hist = sc0_ref[...].reshape(tm, d)
        tail_ref[...] = pre
    else:
        @pl.when(pl.program_id(1) == 0)
        def _():
            hist_scr[...] = sc0_ref[0]

        hist = hist_scr[...]
        hist_scr[...] = pre[tm - SUBLANES:, :]
        tail_ref[...] = pre[tm - SUBLANES:, :]
    y_b = _bdot(sc_b * _causal_conv(pre, hist, cw_ref[...], per_seq), wb_ref[...])
    y_a = _bdot(on_ref[...], wa_ref[...])
    merged = jax.nn.sigmoid(gate_a) * y_a + jax.nn.sigmoid(gate_b) * y_b
    mo = _bdot(merged, wo_ref[...]).reshape(x_ref.shape)
    x1 = x_ref[...] + g1_ref[...] * mo
    y = x1 * lax.rsqrt(jnp.mean(x1 * x1, axis=-1, keepdims=True) + NORM_EPS) * n2_ref[...]
    h2 = (y * (1.0 + sc2_ref[...]) + sh2_ref[...]).reshape(tm, d)
    x1_ref[...] = x1.reshape(tm, d)
    h2_ref[...] = h2

    h2_hi = h2.astype(BF16)
    h2_lo = (h2 - h2_hi.astype(F32)).astype(BF16)
    w_hi = wr_ref[0]
    logits = (jnp.dot(h2_hi, w_hi, preferred_element_type=F32)
              + (jnp.dot(h2_hi, wr_ref[1], preferred_element_type=F32)
                 + jnp.dot(h2_lo, w_hi, preferred_element_type=F32))) + br_ref[...]
    lane = lax.broadcasted_iota(I32, logits.shape, 1)
    lane_f = lane.astype(F32)
    work = logits
    vals, hots = [], []
    member = jnp.zeros(logits.shape, F32)
    for _ in range(TOP_K):
        m = jnp.max(work, axis=-1, keepdims=True)
        sel = jnp.min(jnp.where(work == m, lane_f, float(N_EXPERTS - 1)), axis=-1, keepdims=True)
        hot = lane_f == sel
        vals.append(m)
        hots.append(hot)
        member = member + hot.astype(F32)
        work = jnp.where(hot, -jnp.inf, work)
    exps = [jnp.exp(v - vals[0]) for v in vals]
    denom = exps[0] + exps[1] + exps[2] + exps[3]
    ti = lax.broadcasted_iota(I32, (tm, tm), 0)
    tj = lax.broadcasted_iota(I32, (tm, tm), 1)
    before = (tj < ti).astype(BF16)
    rank_loc = jnp.dot(before, member.astype(BF16), preferred_element_type=F32)
    cnt = jnp.sum(member, axis=0, keepdims=True).astype(I32)
    seg = lax.shift_left(lax.shift_right_logical(cnt + (SUBLANES - 1), 3), 3)
    seg8 = jnp.broadcast_to(seg.astype(F32), (SUBLANES, ROUTER_LANES))
    ei = lax.broadcasted_iota(I32, (ROUTER_LANES, ROUTER_LANES), 0)
    ej = lax.broadcasted_iota(I32, (ROUTER_LANES, ROUTER_LANES), 1)
    base8 = jnp.dot(seg8, (ei < ej).astype(F32), precision=HIGHEST, preferred_element_type=F32)
    row_all = base8[0:1, :] + rank_loc
    pos = jnp.zeros((tm, ROUTER_LANES), F32)
    rw = jnp.zeros((tm, ROUTER_LANES), F32)
    for kk in range(TOP_K):
        pos_k = jnp.sum(jnp.where(hots[kk], row_all, 0.0), axis=-1, keepdims=True)
        pos = jnp.where(lane == kk, pos_k, pos)
        rw = jnp.where(lane == kk, exps[kk] / denom, rw)
    pos_ref[...] = pos
    rw_ref[...] = rw
    pick = (lax.broadcasted_iota(I32, (SUBLANES, ROUTER_LANES), 0)
            == lax.broadcasted_iota(I32, (SUBLANES, ROUTER_LANES), 1)).astype(F32)
    post_ref[...] = lax.dot_general(pick, pos, _NT, precision=HIGHEST, preferred_element_type=F32)
    sub = lax.broadcasted_iota(I32, (SUBLANES, ROUTER_LANES), 0)
    tbl_ref[...] = jnp.where(sub == 0, base8.astype(I32),
                             jnp.where(sub == 1, seg8.astype(I32), cnt_scr[...].astype(I32)))
    cnt_scr[...] = cnt_scr[...] + seg8

    @pl.when(step == n_tiles - 1)
    def _():
        cnt_ref[...] = cnt_scr[...]


def _merge(x, on, p5, sc1, sh1, norm1_w, w5, sc0, g1, sc2, sh2, w_a, w_b, w_o, conv_w, norm2_w, w_r, b_r, cnt0):
    n_seq, seq_len, d = x.shape
    sb, tb = _route_tile(n_seq, seq_len)
    per_seq = sb > 1 or tb == SUBLANES
    if per_seq:
        assert tb == SUBLANES
    tm = sb * tb
    nt = seq_len // tb
    ns = n_seq // sb
    n_tok = n_seq * seq_len
    row = lambda s, t: s * nt + t
    full = lambda shape: pl.BlockSpec(shape, lambda s, t: (0,) * len(shape))
    ada = pl.BlockSpec((sb, 1, d), lambda s, t: (s, 0, 0))
    tok = pl.BlockSpec((tm, d), lambda s, t: (row(s, t), 0))
    lanes = pl.BlockSpec((tm, ROUTER_LANES), lambda s, t: (row(s, t), 0))
    tail_rows = tm if per_seq else SUBLANES
    if p5 is None:
        head_specs = [ada, ada, _resident((1, d)), _resident((d, 5 * d))]
        head_args = [sc1, sh1, norm1_w, w5]
    else:
        head_specs = [pl.BlockSpec((tm, d), lambda s, t, j=j: (row(s, t), j)) for j in range(5)]
        head_args = [p5] * 5
    return pl.pallas_call(
        functools.partial(_merge_body, per_seq=per_seq, n_tiles=ns * nt, projected=p5 is not None),
        grid=(ns, nt),
        in_specs=[pl.BlockSpec((sb, tb, d), lambda s, t: (s, t, 0)),
                  tok] + head_specs + [
                  pl.BlockSpec((sb, SUBLANES, d), lambda s, t: (s, 0, 0)),
                  ada, ada, ada,
                  _resident((d, d)), _resident((d, d)), _resident((d, d)),
                  _resident((SC_CONV, d)), _resident((1, d)), _resident((2, d, ROUTER_LANES)),
                  _resident((1, ROUTER_LANES)), _resident((SUBLANES, ROUTER_LANES))],
        out_specs=[tok, tok, lanes, lanes, full((SUBLANES, ROUTER_LANES)),
                   pl.BlockSpec((tail_rows, d), lambda s, t: (row(s, t), 0)),
                   pl.BlockSpec((SUBLANES, tm), lambda s, t: (0, row(s, t))),
                   pl.BlockSpec((SUBLANES, ROUTER_LANES), lambda s, t: (row(s, t), 0))],
        out_shape=[jax.ShapeDtypeStruct((n_tok, d), F32),
                   jax.ShapeDtypeStruct((n_tok, d), F32),
                   jax.ShapeDtypeStruct((n_tok, ROUTER_LANES), F32),
                   jax.ShapeDtypeStruct((n_tok, ROUTER_LANES), F32),
                   jax.ShapeDtypeStruct((SUBLANES, ROUTER_LANES), F32),
                   jax.ShapeDtypeStruct((ns * nt * tail_rows, d), F32),
                   jax.ShapeDtypeStruct((SUBLANES, n_tok), F32),
                   jax.ShapeDtypeStruct((ns * nt * SUBLANES, ROUTER_LANES), I32)],
        scratch_shapes=[pltpu.VMEM((SUBLANES, ROUTER_LANES), F32), pltpu.VMEM((SUBLANES, d), F32)],
        compiler_params=_params(("arbitrary", "arbitrary")),
        name="merge",
    )(x, on, *head_args, sc0, g1, sc2, sh2, w_a, w_b, w_o, conv_w, norm2_w, w_r, b_r, cnt0)


def _pow2_pieces(length, max_rows):
    out = []
    rows = max_rows
    while rows >= SUBLANES:
        shift = rows.bit_length()
        offset = lax.shift_left(lax.shift_right_logical(length, shift), shift)
        out.append(((length & rows) != 0, offset, rows))
        rows //= 2
    return out


def _segment_copies(tbl_ref, tile_id, local_ref, global_ref, sem, to_global, max_rows):
    out = []
    base = tile_id * (3 * N_EXPERTS)
    for e in range(N_EXPERTS):
        local0 = tbl_ref[base + e]
        length = tbl_ref[base + N_EXPERTS + e]
        global0 = tbl_ref[base + 2 * N_EXPERTS + e]
        for pred, offset, rows in _pow2_pieces(length, max_rows):
            loc = local_ref.at[pl.ds(pl.multiple_of(local0 + offset, SUBLANES), rows)]
            glo = global_ref.at[pl.ds(pl.multiple_of(global0 + offset, SUBLANES), rows)]
            cp = pltpu.make_async_copy(loc, glo, sem) if to_global else pltpu.make_async_copy(glo, loc, sem)
            out.append((pred, cp))
    return out


def _wait_rows(total, local_ref, global_ref, sem, to_global):
    top = 1 << (local_ref.shape[0].bit_length() - 1)
    for pred, _, rows in _pow2_pieces(total, top):
        loc, glo = local_ref.at[pl.ds(0, rows)], global_ref.at[pl.ds(0, rows)]
        cp = pltpu.make_async_copy(loc, glo, sem) if to_global else pltpu.make_async_copy(glo, loc, sem)
        pl.when(pred)(cp.wait)


def _start_all(copies):
    for pred, cp in copies:
        pl.when(pred)(cp.start)


def _wait_all(copies):
    for pred, cp in copies:
        pl.when(pred)(cp.wait)


def _dispatch_body(tbl_ref, tails_ref, post_ref, h2_ref, *refs, tile0, totals_at, zero_tails):
    buf_ref, sorted_scr, zero_scr, sems = refs[-4:]
    i = pl.program_id(0)
    slot = lax.rem(i, 2)
    tm = h2_ref.shape[0]
    r_rows = sorted_scr.shape[1]

    if zero_tails:
        @pl.when(i == 0)
        def _():
            zero_scr[...] = jnp.zeros(zero_scr.shape, U32)
            z_rows = zero_scr.shape[0]

            def fill(b, carry):
                cp = pltpu.make_async_copy(zero_scr, buf_ref.at[pl.ds(pl.multiple_of(b * SUBLANES, SUBLANES), z_rows)],
                                           sems.at[2])
                cp.start()
                cp.wait()
                return carry

            tails = []
            for e in range(N_EXPERTS):
                start, length = tails_ref[e], tails_ref[N_EXPERTS + e]
                whole = length // z_rows
                lax.fori_loop(0, whole, lambda b, c, s=start: fill(s // SUBLANES + b * (z_rows // SUBLANES), c), 0)
                rest0 = start + whole * z_rows
                for pred, offset, rows in _pow2_pieces(length - whole * z_rows, z_rows // 2):
                    at = pl.multiple_of(rest0 + offset, SUBLANES)
                    tails.append((pred, pltpu.make_async_copy(zero_scr.at[pl.ds(0, rows)],
                                                              buf_ref.at[pl.ds(at, rows)], sems.at[2])))
            _start_all(tails)
            _wait_all(tails)
            lax.fori_loop(tails_ref[2 * N_EXPERTS] // z_rows, buf_ref.shape[0] // z_rows,
                          lambda b, c: fill(b * (z_rows // SUBLANES), c), 0)

    pos = post_ref[...].astype(I32)
    j = lax.broadcasted_iota(I32, (r_rows, tm), 0)
    onehot = jnp.zeros((r_rows, tm), F32)
    for k in range(TOP_K):
        onehot = jnp.where(j == pos[k:k + 1, :], 1.0, onehot)
    sorted_scr[slot] = _pack_halves(_bdot(onehot, h2_ref[...]))

    _start_all(_segment_copies(tbl_ref, tile0 + i, sorted_scr.at[slot], buf_ref, sems.at[slot], True, tm))

    @pl.when(i > 0)
    def _():
        _wait_rows(tbl_ref[totals_at + tile0 + i - 1], sorted_scr.at[1 - slot], buf_ref, sems.at[1 - slot], True)

    @pl.when(i == pl.num_programs(0) - 1)
    def _():
        _wait_rows(tbl_ref[totals_at + tile0 + i], sorted_scr.at[slot], buf_ref, sems.at[slot], True)


def _dispatch(tbl, tails, post, h2, buf, buf_rows, tm, tile0, totals_at):
    n_tok, d = h2.shape
    r_rows = TOP_K * tm + N_EXPERTS * SUBLANES
    first = buf is None
    grid_spec = pltpu.PrefetchScalarGridSpec(
        num_scalar_prefetch=2,
        grid=(n_tok // tm,),
        in_specs=[pl.BlockSpec((SUBLANES, tm), lambda i, *_: (0, i)),
                  pl.BlockSpec((tm, d), lambda i, *_: (i, 0))]
                 + ([] if first else [pl.BlockSpec(memory_space=pl.ANY)]),
        out_specs=pl.BlockSpec(memory_space=pl.ANY),
        scratch_shapes=[pltpu.VMEM((2, r_rows, d // 2), U32),
                        pltpu.VMEM((EXPERT_BLOCK // 2, d // 2), U32),
                        pltpu.SemaphoreType.DMA((3,))],
    )
    return pl.pallas_call(
        functools.partial(_dispatch_body, tile0=tile0, totals_at=totals_at, zero_tails=first),
        grid_spec=grid_spec,
        out_shape=jax.ShapeDtypeStruct((buf_rows, d // 2), U32),
        input_output_aliases={} if first else {4: 0},
        compiler_params=pltpu.CompilerParams(dimension_semantics=("arbitrary",), has_side_effects=True,
                                             vmem_limit_bytes=VMEM_LIMIT),
        name="dispatch",
    )(tbl, tails, post, h2, *([] if first else [buf]))


def _expert_body(be_ref, nv_ref, x_ref, wg_ref, bg_ref, wu_ref, bu_ref, wd_ref, bd_ref, o_ref,
                 wg_s, wu_s, wd_s):
    i = pl.program_id(0)

    @pl.when(i < nv_ref[0])
    def _():
        @pl.when((i == 0) | (be_ref[i] != be_ref[jnp.maximum(i - 1, 0)]))
        def _():
            wg_s[...] = wg_ref[0].astype(BF16)
            wu_s[...] = wu_ref[0].astype(BF16)
            wd_s[...] = wd_ref[0].astype(BF16)

        x = _unpack_halves(x_ref[...])
        gate = jnp.dot(x, wg_s[...], preferred_element_type=F32) + bg_ref[0]
        up = jnp.dot(x, wu_s[...], preferred_element_type=F32) + bu_ref[0]
        gate = jnp.minimum(gate, SWIGLU_LIMIT)
        up = jnp.clip(up, -SWIGLU_LIMIT, SWIGLU_LIMIT)
        glu = gate * jax.nn.sigmoid(SWIGLU_ALPHA * gate)
        out = _bdot((up + 1.0) * glu, wd_s[...]) + bd_ref[0]
        o_ref[...] = _pack_halves(out.astype(BF16).astype(F32))

    @pl.when(i >= nv_ref[0])
    def _():
        o_ref[...] = jnp.zeros(o_ref.shape, U32)


def _experts(block_e, n_valid, xs, w_gate, b_gate, w_up, b_up, w_down, b_down):
    m_pad, d_half = xs.shape
    d = 2 * d_half
    n_blocks = m_pad // EXPERT_BLOCK
    f = w_gate.shape[2]
    blk = lambda i, be, nv: (jnp.maximum(jnp.minimum(i, nv[0] - 1), 0), 0)
    wspec = lambda a, b: pl.BlockSpec((1, a, b), lambda i, be, nv: (be[i], 0, 0))
    grid_spec = pltpu.PrefetchScalarGridSpec(
        num_scalar_prefetch=2,
        grid=(n_blocks,),
        in_specs=[pl.BlockSpec((EXPERT_BLOCK, d_half), blk),
                  wspec(d, f), wspec(1, f), wspec(d, f), wspec(1, f), wspec(f, d), wspec(1, d)],
        out_specs=pl.BlockSpec((EXPERT_BLOCK, d_half), lambda i, be, nv: (i, 0)),
        scratch_shapes=[pltpu.VMEM((d, f), BF16), pltpu.VMEM((d, f), BF16), pltpu.VMEM((f, d), BF16)],
    )
    return pl.pallas_call(
        _expert_body,
        grid_spec=grid_spec,
        out_shape=jax.ShapeDtypeStruct((m_pad, d_half), U32),
        compiler_params=_params(("arbitrary",)),
        name="experts",
    )(block_e, n_valid, xs, w_gate, b_gate[:, None, :], w_up, b_up[:, None, :], w_down, b_down[:, None, :])


def _combine_body(tbl_ref, ys_ref, pos_ref, rw_ref, x1_ref, g2_ref, fw_ref, o_ref, blk_scr, sems,
                  *, tile0, totals_at):
    step = pl.program_id(0) * pl.num_programs(1) + pl.program_id(1)
    n_steps = pl.num_programs(0) * pl.num_programs(1)
    slot = lax.rem(step, 2)
    tm = pos_ref.shape[0]
    r_rows = blk_scr.shape[1]

    def fetch(tile, into):
        return _segment_copies(tbl_ref, tile0 + tile, blk_scr.at[into], ys_ref, sems.at[into], False, tm)

    @pl.when(step == 0)
    def _():
        blk_scr[...] = jnp.zeros(blk_scr.shape, U32)
        _start_all(fetch(0, 0))

    @pl.when(step + 1 < n_steps)
    def _():
        _start_all(fetch(step + 1, 1 - slot))

    _wait_rows(tbl_ref[totals_at + tile0 + step], blk_scr.at[slot], ys_ref, sems.at[slot], False)

    pos = pos_ref[...].astype(I32)
    rw = rw_ref[...]
    j = lax.broadcasted_iota(I32, (tm, r_rows), 1)
    pw = jnp.zeros((tm, r_rows), F32)
    for k in range(TOP_K):
        pw = jnp.where(j == pos[:, k:k + 1], rw[:, k:k + 1], pw)
    ffn = jnp.dot(pw.astype(BF16), _unpack_halves(blk_scr[slot]), preferred_element_type=F32)
    y = x1_ref[...] + g2_ref[...] * ffn.reshape(x1_ref.shape)
    o_ref[...] = y * lax.rsqrt(jnp.mean(y * y, axis=-1, keepdims=True) + NORM_EPS) * fw_ref[...]


def _combine(tbl, ys, pos, rw, x1, g2, final_w, sb, tb, tile0, totals_at):
    n_seq, seq_len, d = x1.shape
    tm = sb * tb
    nt = seq_len // tb
    r_rows = TOP_K * tm + N_EXPERTS * SUBLANES
    row = lambda s, t: s * nt + t
    grid_spec = pltpu.PrefetchScalarGridSpec(
        num_scalar_prefetch=1,
        grid=(n_seq // sb, nt),
        in_specs=[pl.BlockSpec(memory_space=pl.ANY),
                  pl.BlockSpec((tm, ROUTER_LANES), lambda s, t, *_: (row(s, t), 0)),
                  pl.BlockSpec((tm, ROUTER_LANES), lambda s, t, *_: (row(s, t), 0)),
                  pl.BlockSpec((sb, tb, d), lambda s, t, *_: (s, t, 0)),
                  pl.BlockSpec((sb, 1, d), lambda s, t, *_: (s, 0, 0)),
                  pl.BlockSpec((1, d), lambda s, t, *_: (0, 0))],
        out_specs=pl.BlockSpec((sb, tb, d), lambda s, t, *_: (s, t, 0)),
        scratch_shapes=[pltpu.VMEM((2, r_rows, d // 2), U32), pltpu.SemaphoreType.DMA((2,))],
    )
    return pl.pallas_call(
        functools.partial(_combine_body, tile0=tile0, totals_at=totals_at),
        grid_spec=grid_spec,
        out_shape=jax.ShapeDtypeStruct((n_seq, seq_len, d), F32),
        compiler_params=_params(("arbitrary", "arbitrary")),
        name="combine",
    )(tbl, ys, pos, rw, x1, g2, final_w)


def _pad_state(state, rows):
    return jnp.pad(state, ((0, 0), (rows - state.shape[1], 0), (0, 0)))


def kernel(x_prompt, x_sample, c_prompt, c_sample, state_gdn_conv, state_gdn_rec, state_sc_conv, w_ada, b_ada,
           norm1_w, w_in, gdn_conv_w, gdn_a_log, gdn_dt_bias, gdn_norm_w, w_branch_a, sc_conv_w, w_branch_b,
           w_out, norm2_w, w_router, b_router, w_gate, b_gate, w_up, b_up, w_down, b_down, final_norm_w):
    assert w_ada.shape[0] == 1, "single-layer trunk"
    d = D_MODEL
    bp, tp, _ = x_prompt.shape
    bs, ts, _ = x_sample.shape
    n_p, n_s = bp * tp, bs * ts
    n_tok = n_p + n_s

    w_in0 = w_in[0]
    g_lo, g_hi = QKV_W + d, QKV_W + d + N_GATE_COLS
    w_qkvz = w_in0[:, :g_lo].astype(BF16)
    w5 = w_in0[:, g_hi:].astype(BF16)
    w_all = jnp.concatenate([w_qkvz, w5], axis=1)
    w_g = w_in0[:, g_lo:g_hi].astype(BF16)
    w_gt = w_g.T
    zeros_h = jnp.zeros((GDN_HEADS,), F32)
    p_row = jnp.stack([jnp.concatenate([zeros_h, gdn_a_log[0]]), jnp.concatenate([zeros_h, gdn_dt_bias[0]])])
    p_col = p_row.T
    w_a = w_branch_a[0].astype(BF16)
    w_b = w_branch_b[0].astype(BF16)
    w_o = w_out[0].astype(BF16)
    w_r = jnp.pad(w_router[0], ((0, 0), (0, ROUTER_LANES - N_EXPERTS)))
    w_r_hi = w_r.astype(BF16)
    w_r = jnp.stack([w_r_hi, (w_r - w_r_hi.astype(F32)).astype(BF16)])
    b_r = jnp.pad(b_router[0], (0, ROUTER_LANES - N_EXPERTS), constant_values=-jnp.inf).reshape(1, ROUTER_LANES)

    n_c = bp + bs
    c_rows = -(-n_c // 16) * 16
    c_all = jnp.pad(jnp.concatenate([c_prompt, c_sample], axis=0), ((0, c_rows - n_c), (0, 0)))
    ada = _ada(c_all, w_ada[0], b_ada[0])

    def ada_parts(lo, hi):
        return [ada[lo:hi, j * d:(j + 1) * d].reshape(hi - lo, 1, d) for j in range(6)]

    groups = [
        dict(x=x_prompt, ada=ada_parts(0, bp), n_seq=bp, seq_len=tp,
             conv0=jnp.zeros((bp, SUBLANES, QKV_W), F32),
             rec0=jnp.zeros((bp, GDN_HEADS, GDN_DK, GDN_DV), F32),
             sc0=jnp.zeros((bp, SUBLANES, d), F32)),
        dict(x=x_sample, ada=ada_parts(bp, n_c), n_seq=bs, seq_len=ts,
             conv0=_pad_state(state_gdn_conv[0], SUBLANES),
             rec0=state_gdn_rec[0],
             sc0=_pad_state(state_sc_conv[0], SUBLANES)),
    ]

    counts = jnp.zeros((SUBLANES, ROUTER_LANES), F32)
    for g in groups:
        sh1, sc1, g1, sh2, sc2, g2 = g["ada"]
        n_seq, seq_len = g["n_seq"], g["seq_len"]
        wide = seq_len > SUBLANES
        on, rec, conv_tail, *p5 = _gdn(g["x"], sc1, sh1, norm1_w, w_all if wide else w_qkvz, w_g, w_gt, p_row, p_col,
                                       g["conv0"], g["rec0"], gdn_conv_w[0], gdn_norm_w)
        x1, h2, pos, rw, counts, tail, post, tbl = _merge(g["x"], on, p5[0] if wide else None, sc1, sh1, norm1_w, w5,
                                                          g["sc0"], g1, sc2, sh2, w_a, w_b, w_o, sc_conv_w[0],
                                                          norm2_w, w_r, b_r, counts)
        sb, tb = _route_tile(n_seq, seq_len)
        g.update(x1=x1, h2=h2, pos=pos, rw=rw, post=post, tbl=tbl, rec=rec, g2=g2, sb=sb, tb=tb, tm=sb * tb,
                 rows_after=counts[0, :N_EXPERTS].astype(I32),
                 new_conv=conv_tail[:, SUBLANES - (GDN_CONV - 1):, :],
                 new_sc=tail.reshape(n_seq, -1, d)[:, -(SC_CONV - 1):, :])

    rows_e = counts[0, :N_EXPERTS].astype(I32)
    padded = (rows_e + EXPERT_BLOCK - 1) // EXPERT_BLOCK * EXPERT_BLOCK
    pad_end = jnp.cumsum(padded)
    expert0 = (pad_end - padded).astype(I32)
    tbl_all = jnp.concatenate([g["tbl"] for g in groups], axis=0).reshape(-1, SUBLANES, ROUTER_LANES)
    n_tiles = tbl_all.shape[0]
    tbl_all = tbl_all[:, :3, :N_EXPERTS].at[:, 2, :].add(expert0[None, :])
    totals_at = n_tiles * 3 * N_EXPERTS
    tbl_all = jnp.concatenate([tbl_all.reshape(-1), jnp.sum(tbl_all[:, 1, :], axis=1)]).astype(I32)
    rows_first = groups[0]["rows_after"]
    tails = jnp.concatenate([expert0 + rows_first, padded - rows_first, pad_end[-1:]]).astype(I32)
    max_rows = n_tok * TOP_K + n_tiles * N_EXPERTS * (SUBLANES - 1)
    n_blocks = -(-max_rows // EXPERT_BLOCK) + N_EXPERTS
    block_start = jnp.arange(n_blocks, dtype=I32) * EXPERT_BLOCK
    block_e = jnp.minimum(jnp.sum((pad_end[None, :] <= block_start[:, None]).astype(I32), axis=1), N_EXPERTS - 1)
    n_valid = (pad_end[-1:] // EXPERT_BLOCK).astype(I32)

    buf = None
    tile0 = 0
    for g in groups:
        g["tile0"] = tile0
        buf = _dispatch(tbl_all, tails, g["post"], g["h2"], buf, n_blocks * EXPERT_BLOCK, g["tm"], tile0, totals_at)
        tile0 += g["h2"].shape[0] // g["tm"]
    ys = _experts(block_e, n_valid, buf, w_gate[0], b_gate[0], w_up[0], b_up[0], w_down[0], b_down[0])
    outs = []
    for g in groups:
        n_seq, seq_len = g["n_seq"], g["seq_len"]
        x1 = g["x1"].reshape(n_seq, seq_len, d)
        outs.append(_combine(tbl_all, ys, g["pos"], g["rw"], x1, g["g2"], final_norm_w.reshape(1, d),
                             g["sb"], g["tb"], g["tile0"], totals_at))

    gp, gs = groups
    return (outs[0], outs[1], gp["new_conv"][None], gp["rec"][None], gp["new_sc"][None],
            gs["new_conv"][None], gs["rec"][None], gs["new_sc"][None])
```

```python
import functools

import jax
import jax.numpy as jnp
from jax import lax
from jax.experimental import pallas as pl
from jax.experimental.pallas import tpu as pltpu

F32 = jnp.float32
BF16 = jnp.bfloat16
I32 = jnp.int32
U32 = jnp.uint32
HIGHEST = lax.Precision.HIGHEST

D_MODEL = 1024
GDN_HEADS = 8
GDN_DK = 128
GDN_DV = 128
GDN_QK = GDN_HEADS * GDN_DK
QKV_W = 3 * GDN_QK
GDN_CONV = 4
GDN_CHUNK = 128
GDN_STEP_CHUNKS = 2
GDN_STEP_SEQS = 8
SC_CONV = 3
N_EXPERTS = 32
TOP_K = 4
SWIGLU_LIMIT = 7.0
SWIGLU_ALPHA = 1.702
NORM_EPS = 1e-6
N_GATE_COLS = 2 * GDN_HEADS
ROUTER_LANES = 128
EXPERT_BLOCK = 512
ROUTE_TILE = 512
ADA_TILE = 1024
SUBLANES = 8
VMEM_LIMIT = 56 * 1024 * 1024

_NT = (((1,), (1,)), ((), ()))
_TN = (((0,), (0,)), ((), ()))


def _bdot(a, b):
    return jnp.dot(a.astype(BF16), b.astype(BF16), preferred_element_type=F32)


def _bdot_nt(a, b):
    return lax.dot_general(a.astype(BF16), b.astype(BF16), _NT, preferred_element_type=F32)


def _bdot_tn(a, b):
    return lax.dot_general(a.astype(BF16), b.astype(BF16), _TN, preferred_element_type=F32)


def _pack_halves(x):
    w = x.shape[1] // 2
    lo = lax.bitcast_convert_type(x[:, :w], U32)
    hi = lax.bitcast_convert_type(x[:, w:], U32)
    return lax.shift_right_logical(lo, jnp.uint32(16)) | (hi & jnp.uint32(0xFFFF0000))


def _unpack_halves(p):
    lo = lax.bitcast_convert_type(lax.shift_left(p, jnp.uint32(16)), F32)
    hi = lax.bitcast_convert_type(p & jnp.uint32(0xFFFF0000), F32)
    return jnp.concatenate([lo, hi], axis=1).astype(BF16)


def _silu(x):
    return x * jax.nn.sigmoid(x)


def _softplus(x):
    return jnp.maximum(x, 0.0) + jnp.log1p(jnp.exp(-jnp.abs(x)))


def _seq_tile(n_seq, seq_len, target):
    if seq_len >= target:
        assert seq_len % target == 0
        return 1, target
    sb = min(n_seq, target // seq_len)
    assert n_seq % sb == 0 and seq_len % SUBLANES == 0
    return sb, seq_len


def _route_tile(n_seq, seq_len):
    return _seq_tile(n_seq, seq_len, ROUTE_TILE)


def _params(sem):
    return pltpu.CompilerParams(dimension_semantics=sem, vmem_limit_bytes=VMEM_LIMIT)


def _ada_body(c_ref, w_ref, b_ref, o_ref):
    o_ref[...] = _bdot(_silu(c_ref[...]), w_ref[...]) + b_ref[...]


def _ada(c, w_ada, b_ada):
    rows, d = c.shape
    n = w_ada.shape[1]
    tn = ADA_TILE
    return pl.pallas_call(
        _ada_body,
        grid=(n // tn,),
        in_specs=[pl.BlockSpec((rows, d), lambda j: (0, 0)),
                  pl.BlockSpec((d, tn), lambda j: (0, j)),
                  pl.BlockSpec((1, tn), lambda j: (0, j))],
        out_specs=pl.BlockSpec((rows, tn), lambda j: (0, j)),
        out_shape=jax.ShapeDtypeStruct((rows, n), F32),
        compiler_params=_params(("arbitrary",)),
        name="ada",
    )(c, w_ada, b_ada.reshape(1, n))


def _gates(v, a_log, dt_bias, axis):
    is_beta = lax.broadcasted_iota(I32, v.shape, axis) < GDN_HEADS
    beta = jax.nn.sigmoid(v)
    g = -jnp.exp(a_log) * _softplus(v + dt_bias)
    return jnp.where(is_beta, beta, g)


def _ada_norm(x_ref, sc_ref, sh_ref, nw_ref):
    x = x_ref[...]
    y = x * lax.rsqrt(jnp.mean(x * x, axis=-1, keepdims=True) + NORM_EPS) * nw_ref[...]
    h = y * (1.0 + sc_ref[...]) + sh_ref[...]
    return h.reshape(x.shape[0] * x.shape[1], x.shape[2]).astype(BF16)


def _resident(shape):
    return pl.BlockSpec(shape, lambda *_: (0,) * len(shape), pipeline_mode=pl.Buffered(1))


def _shift_rows(x, hist, s, per_seq):
    rows, width = x.shape
    x3 = x.reshape(rows // SUBLANES, SUBLANES, width)
    xr = pltpu.roll(x3, s, 1)
    if per_seq:
        src = pltpu.roll(hist.reshape(x3.shape), s, 1)
    else:
        hr = pltpu.roll(hist.reshape(1, SUBLANES, width), s, 1)
        src = hr if rows == SUBLANES else jnp.concatenate([hr, xr[:-1]], axis=0)
    sub = lax.broadcasted_iota(I32, x3.shape, 1)
    return jnp.where(sub < s, src, xr).reshape(rows, width)


def _causal_conv(x, hist, w, per_seq):
    taps = w.shape[0]
    acc = x * w[taps - 1:taps, :]
    for s in range(1, taps):
        acc = acc + _shift_rows(x, hist, s, per_seq) * w[taps - 1 - s:taps - s, :]
    return acc


def _gdn_body(x_ref, sc_ref, sh_ref, n1_ref, wq_ref, wg_ref, wgt_ref, prow_ref, pcol_ref, conv0_ref, s0_ref,
              cw_ref, nw_ref, o_ref, s_ref, conv_ref, *rest, chunk, n_chunks):
    stacked = s_ref.shape[0] > 1
    hist_scr = rest[-1]

    @pl.when(pl.program_id(1) == 0)
    def _():
        s_ref[...] = s0_ref[...]
        if not stacked:
            hist_scr[...] = conv0_ref[0]

    h_in = _ada_norm(x_ref, sc_ref, sh_ref, n1_ref)
    n_own = QKV_W + D_MODEL
    proj = jnp.dot(h_in, wq_ref[:, :n_own], preferred_element_type=F32)
    x = proj[:, :QKV_W]
    z = proj[:, QKV_W:]
    if len(rest) == 2:
        rest[0][...] = jnp.dot(h_in, wq_ref[:, n_own:], preferred_element_type=F32)
    if stacked:
        hist = conv0_ref[...].reshape(x.shape)
        conv_ref[...] = x.reshape(conv_ref.shape)
    else:
        hist = hist_scr[...]
        hist_scr[...] = x[x.shape[0] - SUBLANES:, :]
        conv_ref[0] = x[x.shape[0] - SUBLANES:, :]
    qkvc = _silu(_causal_conv(x, hist, cw_ref[...], stacked))

    gcol = _gates(jnp.dot(h_in, wg_ref[...], preferred_element_type=F32), prow_ref[0:1, :], prow_ref[1:2, :], 1)
    ri = lax.broadcasted_iota(I32, (chunk, chunk), 0)
    ci = lax.broadcasted_iota(I32, (chunk, chunk), 1)
    causal = ri >= ci
    strict = ri > ci
    tri = causal.astype(F32)
    eye = (ri == ci).astype(F32)
    n_lvl = chunk.bit_length() - 1
    blk = [lax.shift_right_logical(ri, l) == lax.shift_right_logical(ci, l) for l in range(1, n_lvl + 1)]
    pair = [blk[l] & jnp.logical_not(blk[l - 1]) for l in range(1, n_lvl)]

    units = [(c, h) for c in range(n_chunks) for h in range(GDN_HEADS)]
    every = range(len(units))
    q, k, v, beta, dcol, dlast, decay, edec = [], [], [], [], [], [], [], []
    for c in range(n_chunks):
        rows = slice(c * chunk, (c + 1) * chunk)
        dec_col = jnp.dot(tri, gcol[rows, GDN_HEADS:], precision=HIGHEST, preferred_element_type=F32)
        grow = _gates(lax.dot_general(wgt_ref[...], h_in[rows, :], _NT, preferred_element_type=F32),
                      pcol_ref[:, 0:1], pcol_ref[:, 1:2], 0)
        dec_row = lax.dot_general(grow[GDN_HEADS:, :], tri, _NT, precision=HIGHEST, preferred_element_type=F32)
        for h in range(GDN_HEADS):
            qh = qkvc[rows, h * GDN_DK:(h + 1) * GDN_DK]
            kh = qkvc[rows, GDN_QK + h * GDN_DK:GDN_QK + (h + 1) * GDN_DK]
            q.append(qh * (lax.rsqrt(jnp.sum(qh * qh, axis=-1, keepdims=True) + 1e-6) * (GDN_DK ** -0.5)))
            k.append(kh * lax.rsqrt(jnp.sum(kh * kh, axis=-1, keepdims=True) + 1e-6))
            v.append(qkvc[rows, 2 * GDN_QK + h * GDN_DV:2 * GDN_QK + (h + 1) * GDN_DV])
            beta.append(gcol[rows, h:h + 1])
            dcol.append(dec_col[:, h:h + 1])
            dlast.append(dec_col[chunk - 1:chunk, h:h + 1])
            decay.append(jnp.where(causal, jnp.exp(dcol[-1] - dec_row[h:h + 1, :]), 0.0))
            edec.append(jnp.exp(dcol[-1]))
    kk = [_bdot_nt(k[i], k[i]) for i in every]
    qk = [_bdot_nt(q[i], k[i]) * decay[i] for i in every]
    a = [jnp.where(strict, beta[i] * kk[i] * decay[i], 0.0) for i in every]
    inv = [eye - jnp.where(blk[0], a[i], 0.0) for i in every]
    for lower_left in pair:
        right = [_bdot(jnp.where(lower_left, a[i], 0.0), inv[i]) for i in every]
        inv = [inv[i] - _bdot(inv[i], right[i]) for i in every]
    sol = [_bdot(inv[i], jnp.concatenate([beta[i] * v[i], (beta[i] * edec[i]) * k[i]], axis=1)) for i in every]
    seq_of = (lambda c: c) if stacked else (lambda c: 0)
    waves = [list(every)] if stacked else [[c * GDN_HEADS + h for h in range(GDN_HEADS)] for c in range(n_chunks)]
    state = {(seq_of(c), h): s_ref[seq_of(c), h] for c, h in units}
    for wave in waves:
        key = {i: (seq_of(units[i][0]), units[i][1]) for i in wave}
        ws = {i: _bdot(jnp.concatenate([sol[i][:, GDN_DV:], q[i] * edec[i]], axis=0), state[key[i]]) for i in wave}
        u = {i: sol[i][:, :GDN_DV] - ws[i][:chunk] for i in wave}
        o = {i: ws[i][chunk:] + _bdot(qk[i], u[i]) for i in wave}
        upd = {i: _bdot_tn(k[i] * jnp.exp(dlast[i] - dcol[i]), u[i]) for i in wave}
        for i in wave:
            c, h = units[i]
            state[key[i]] = state[key[i]] * jnp.exp(dlast[i]) + upd[i]
            rows = slice(c * chunk, (c + 1) * chunk)
            on = o[i] * lax.rsqrt(jnp.mean(o[i] * o[i], axis=-1, keepdims=True) + NORM_EPS) * nw_ref[...]
            o_ref[rows, h * GDN_DV:(h + 1) * GDN_DV] = on * _silu(z[rows, h * GDN_DV:(h + 1) * GDN_DV])
    for (b, h), value in state.items():
        s_ref[b, h] = value


def _gdn(x, sc, sh, norm1_w, w_proj, w_g, w_gt, p_row, p_col, conv0, state0, conv_w, norm_w):
    n_seq, seq_len, d = x.shape
    chunk = min(GDN_CHUNK, seq_len)
    assert seq_len % chunk == 0 and chunk % SUBLANES == 0
    n_tok = n_seq * seq_len
    stacked = seq_len == SUBLANES and n_seq % GDN_STEP_SEQS == 0
    if stacked:
        n_step_chunks, sb, nt = GDN_STEP_SEQS, GDN_STEP_SEQS, 1
    else:
        n_step_chunks = GDN_STEP_CHUNKS if (seq_len // chunk) % GDN_STEP_CHUNKS == 0 else 1
        sb, nt = 1, seq_len // (n_step_chunks * chunk)
    step_rows = n_step_chunks * chunk
    tb = step_rows // sb
    row = lambda s, t: s * nt + t
    ada = pl.BlockSpec((sb, 1, d), lambda s, t: (s, 0, 0))
    extra = w_proj.shape[1] - (QKV_W + d)
    extra_spec = [pl.BlockSpec((step_rows, extra), lambda s, t: (row(s, t), 0))] if extra else []
    extra_shape = [jax.ShapeDtypeStruct((n_tok, extra), F32)] if extra else []
    return pl.pallas_call(
        functools.partial(_gdn_body, chunk=chunk, n_chunks=n_step_chunks),
        grid=(n_seq // sb, nt),
        in_specs=[pl.BlockSpec((sb, tb, d), lambda s, t: (s, t, 0)),
                  ada, ada, _resident((1, d)),
                  _resident(w_proj.shape), _resident((d, N_GATE_COLS)), _resident((N_GATE_COLS, d)),
                  _resident((2, N_GATE_COLS)), _resident((N_GATE_COLS, 2)),
                  pl.BlockSpec((sb, SUBLANES, QKV_W), lambda s, t: (s, 0, 0)),
                  pl.BlockSpec((sb, GDN_HEADS, GDN_DK, GDN_DV), lambda s, t: (s, 0, 0, 0)),
                  _resident((GDN_CONV, QKV_W)), _resident((1, GDN_DV))],
        out_specs=[pl.BlockSpec((step_rows, d), lambda s, t: (row(s, t), 0)),
                   pl.BlockSpec((sb, GDN_HEADS, GDN_DK, GDN_DV), lambda s, t: (s, 0, 0, 0)),
                   pl.BlockSpec((sb, SUBLANES, QKV_W), lambda s, t: (s, 0, 0))] + extra_spec,
        out_shape=[jax.ShapeDtypeStruct((n_tok, d), F32),
                   jax.ShapeDtypeStruct((n_seq, GDN_HEADS, GDN_DK, GDN_DV), F32),
                   jax.ShapeDtypeStruct((n_seq, SUBLANES, QKV_W), F32)] + extra_shape,
        scratch_shapes=[pltpu.VMEM((SUBLANES, QKV_W), F32)],
        compiler_params=_params(("arbitrary", "arbitrary")),
        name="gdn",
    )(x, sc, sh, norm1_w, w_proj, w_g, w_gt, p_row, p_col, conv0, state0, conv_w, norm_w)


def _merge_body(x_ref, on_ref, *refs, per_seq, n_tiles, projected):
    n_head = 5 if projected else 4
    (sc0_ref, g1_ref, sc2_ref, sh2_ref, wa_ref, wb_ref, wo_ref, cw_ref, n2_ref, wr_ref, br_ref, cnt0_ref,
     x1_ref, h2_ref, pos_ref, rw_ref, cnt_ref, tail_ref, post_ref, tbl_ref, cnt_scr, hist_scr) = refs[n_head:]
    step = pl.program_id(0) * pl.num_programs(1) + pl.program_id(1)

    @pl.when(step == 0)
    def _():
        cnt_scr[...] = cnt0_ref[...]

    tm, d = on_ref.shape
    if projected:
        sc_b, sc_c, sc_h, gate_a, gate_b = (r[...] for r in refs[:n_head])
    else:
        sc1_ref, sh1_ref, n1_ref, w5_ref = refs[:n_head]
        p5 = jnp.dot(_ada_norm(x_ref, sc1_ref, sh1_ref, n1_ref), w5_ref[...], preferred_element_type=F32)
        sc_b, sc_c, sc_h, gate_a, gate_b = (p5[:, j * d:(j + 1) * d] for j in range(5))
    pre = sc_c * sc_h
    if per_seq:
        hist = sc0_ref[...].reshape(tm, d)
        tail_ref[...] = pre
    else:
        @pl.when(pl.program_id(1) == 0)
        def _():
            hist_scr[...] = sc0_ref[0]

        hist = hist_scr[...]
        hist_scr[...] = pre[tm - SUBLANES:, :]
        tail_ref[...] = pre[tm - SUBLANES:, :]
    y_b = _bdot(sc_b * _causal_conv(pre, hist, cw_ref[...], per_seq), wb_ref[...])
    y_a = _bdot(on_ref[...], wa_ref[...])
    merged = jax.nn.sigmoid(gate_a) * y_a + jax.nn.sigmoid(gate_b) * y_b
    mo = _bdot(merged, wo_ref[...]).reshape(x_ref.shape)
    x1 = x_ref[...] + g1_ref[...] * mo
    y = x1 * lax.rsqrt(jnp.mean(x1 * x1, axis=-1, keepdims=True) + NORM_EPS) * n2_ref[...]
    h2 = (y * (1.0 + sc2_ref[...]) + sh2_ref[...]).reshape(tm, d)
    x1_ref[...] = x1.reshape(tm, d)
    h2_ref[...] = h2

    h2_hi = h2.astype(BF16)
    h2_lo = (h2 - h2_hi.astype(F32)).astype(BF16)
    w_hi = wr_ref[0]
    logits = (jnp.dot(h2_hi, w_hi, preferred_element_type=F32)
              + (jnp.dot(h2_hi, wr_ref[1], preferred_element_type=F32)
                 + jnp.dot(h2_lo, w_hi, preferred_element_type=F32))) + br_ref[...]
    lane = lax.broadcasted_iota(I32, logits.shape, 1)
    lane_f = lane.astype(F32)
    work = logits
    vals, hots = [], []
    member = jnp.zeros(logits.shape, F32)
    for _ in range(TOP_K):
        m = jnp.max(work, axis=-1, keepdims=True)
        sel = jnp.min(jnp.where(work == m, lane_f, float(N_EXPERTS - 1)), axis=-1, keepdims=True)
        hot = lane_f == sel
        vals.append(m)
        hots.append(hot)
        member = member + hot.astype(F32)
        work = jnp.where(hot, -jnp.inf, work)
    exps = [jnp.exp(v - vals[0]) for v in vals]
    denom = exps[0] + exps[1] + exps[2] + exps[3]
    ti = lax.broadcasted_iota(I32, (tm, tm), 0)
    tj = lax.broadcasted_iota(I32, (tm, tm), 1)
    before = (tj < ti).astype(BF16)
    rank_loc = jnp.dot(before, member.astype(BF16), preferred_element_type=F32)
    cnt = jnp.sum(member, axis=0, keepdims=True).astype(I32)
    seg = lax.shift_left(lax.shift_right_logical(cnt + (SUBLANES - 1), 3), 3)
    seg8 = jnp.broadcast_to(seg.astype(F32), (SUBLANES, ROUTER_LANES))
    ei = lax.broadcasted_iota(I32, (ROUTER_LANES, ROUTER_LANES), 0)
    ej = lax.broadcasted_iota(I32, (ROUTER_LANES, ROUTER_LANES), 1)
    base8 = jnp.dot(seg8, (ei < ej).astype(F32), precision=HIGHEST, preferred_element_type=F32)
    row_all = base8[0:1, :] + rank_loc
    pos = jnp.zeros((tm, ROUTER_LANES), F32)
    rw = jnp.zeros((tm, ROUTER_LANES), F32)
    for kk in range(TOP_K):
        pos_k = jnp.sum(jnp.where(hots[kk], row_all, 0.0), axis=-1, keepdims=True)
        pos = jnp.where(lane == kk, pos_k, pos)
        rw = jnp.where(lane == kk, exps[kk] / denom, rw)
    pos_ref[...] = pos
    rw_ref[...] = rw
    pick = (lax.broadcasted_iota(I32, (SUBLANES, ROUTER_LANES), 0)
            == lax.broadcasted_iota(I32, (SUBLANES, ROUTER_LANES), 1)).astype(F32)
    post_ref[...] = lax.dot_general(pick, pos, _NT, precision=HIGHEST, preferred_element_type=F32)
    sub = lax.broadcasted_iota(I32, (SUBLANES, ROUTER_LANES), 0)
    tbl_ref[...] = jnp.where(sub == 0, base8.astype(I32),
                             jnp.where(sub == 1, seg8.astype(I32), cnt_scr[...].astype(I32)))
    cnt_scr[...] = cnt_scr[...] + seg8

    @pl.when(step == n_tiles - 1)
    def _():
        cnt_ref[...] = cnt_scr[...]


def _merge(x, on, p5, sc1, sh1, norm1_w, w5, sc0, g1, sc2, sh2, w_a, w_b, w_o, conv_w, norm2_w, w_r, b_r, cnt0):
    n_seq, seq_len, d = x.shape
    sb, tb = _route_tile(n_seq, seq_len)
    per_seq = sb > 1 or tb == SUBLANES
    if per_seq:
        assert tb == SUBLANES
    tm = sb * tb
    nt = seq_len // tb
    ns = n_seq // sb
    n_tok = n_seq * seq_len
    row = lambda s, t: s * nt + t
    full = lambda shape: pl.BlockSpec(shape, lambda s, t: (0,) * len(shape))
    ada = pl.BlockSpec((sb, 1, d), lambda s, t: (s, 0, 0))
    tok = pl.BlockSpec((tm, d), lambda s, t: (row(s, t), 0))
    lanes = pl.BlockSpec((tm, ROUTER_LANES), lambda s, t: (row(s, t), 0))
    tail_rows = tm if per_seq else SUBLANES
    if p5 is None:
        head_specs = [ada, ada, _resident((1, d)), _resident((d, 5 * d))]
        head_args = [sc1, sh1, norm1_w, w5]
    else:
        head_specs = [pl.BlockSpec((tm, d), lambda s, t, j=j: (row(s, t), j)) for j in range(5)]
        head_args = [p5] * 5
    return pl.pallas_call(
        functools.partial(_merge_body, per_seq=per_seq, n_tiles=ns * nt, projected=p5 is not None),
        grid=(ns, nt),
        in_specs=[pl.BlockSpec((sb, tb, d), lambda s, t: (s, t, 0)),
                  tok] + head_specs + [
                  pl.BlockSpec((sb, SUBLANES, d), lambda s, t: (s, 0, 0)),
                  ada, ada, ada,
                  _resident((d, d)), _resident((d, d)), _resident((d, d)),
                  _resident((SC_CONV, d)), _resident((1, d)), _resident((2, d, ROUTER_LANES)),
                  _resident((1, ROUTER_LANES)), _resident((SUBLANES, ROUTER_LANES))],
        out_specs=[tok, tok, lanes, lanes, full((SUBLANES, ROUTER_LANES)),
                   pl.BlockSpec((tail_rows, d), lambda s, t: (row(s, t), 0)),
                   pl.BlockSpec((SUBLANES, tm), lambda s, t: (0, row(s, t))),
                   pl.BlockSpec((SUBLANES, ROUTER_LANES), lambda s, t: (row(s, t), 0))],
        out_shape=[jax.ShapeDtypeStruct((n_tok, d), F32),
                   jax.ShapeDtypeStruct((n_tok, d), F32),
                   jax.ShapeDtypeStruct((n_tok, ROUTER_LANES), F32),
                   jax.ShapeDtypeStruct((n_tok, ROUTER_LANES), F32),
                   jax.ShapeDtypeStruct((SUBLANES, ROUTER_LANES), F32),
                   jax.ShapeDtypeStruct((ns * nt * tail_rows, d), F32),
                   jax.ShapeDtypeStruct((SUBLANES, n_tok), F32),
                   jax.ShapeDtypeStruct((ns * nt * SUBLANES, ROUTER_LANES), I32)],
        scratch_shapes=[pltpu.VMEM((SUBLANES, ROUTER_LANES), F32), pltpu.VMEM((SUBLANES, d), F32)],
        compiler_params=_params(("arbitrary", "arbitrary")),
        name="merge",
    )(x, on, *head_args, sc0, g1, sc2, sh2, w_a, w_b, w_o, conv_w, norm2_w, w_r, b_r, cnt0)


def _pow2_pieces(length, max_rows):
    out = []
    rows = max_rows
    while rows >= SUBLANES:
        shift = rows.bit_length()
        offset = lax.shift_left(lax.shift_right_logical(length, shift), shift)
        out.append(((length & rows) != 0, offset, rows))
        rows //= 2
    return out


def _segment_copies(tbl_ref, tile_id, local_ref, global_ref, sem, to_global, max_rows):
    out = []
    base = tile_id * (3 * N_EXPERTS)
    for e in range(N_EXPERTS):
        local0 = tbl_ref[base + e]
        length = tbl_ref[base + N_EXPERTS + e]
        global0 = tbl_ref[base + 2 * N_EXPERTS + e]
        for pred, offset, rows in _pow2_pieces(length, max_rows):
            loc = local_ref.at[pl.ds(pl.multiple_of(local0 + offset, SUBLANES), rows)]
            glo = global_ref.at[pl.ds(pl.multiple_of(global0 + offset, SUBLANES), rows)]
            cp = pltpu.make_async_copy(loc, glo, sem) if to_global else pltpu.make_async_copy(glo, loc, sem)
            out.append((pred, cp))
    return out


def _wait_rows(total, local_ref, global_ref, sem, to_global):
    top = 1 << (local_ref.shape[0].bit_length() - 1)
    for pred, _, rows in _pow2_pieces(total, top):
        loc, glo = local_ref.at[pl.ds(0, rows)], global_ref.at[pl.ds(0, rows)]
        cp = pltpu.make_async_copy(loc, glo, sem) if to_global else pltpu.make_async_copy(glo, loc, sem)
        pl.when(pred)(cp.wait)


def _start_all(copies):
    for n, (pred, cp) in enumerate(copies):
        pl.when(pred)(functools.partial(cp.start, priority=n % 2))


def _wait_all(copies):
    for pred, cp in copies:
        pl.when(pred)(cp.wait)


def _dispatch_body(tbl_ref, tails_ref, post_ref, h2_ref, *refs, tile0, totals_at, zero_tails):
    buf_ref, sorted_scr, zero_scr, sems = refs[-4:]
    i = pl.program_id(0)
    slot = lax.rem(i, 2)
    tm = h2_ref.shape[0]
    r_rows = sorted_scr.shape[1]

    if zero_tails:
        @pl.when(i == 0)
        def _():
            zero_scr[...] = jnp.zeros(zero_scr.shape, U32)
            z_rows = zero_scr.shape[0]

            def fill(b, carry):
                cp = pltpu.make_async_copy(zero_scr, buf_ref.at[pl.ds(pl.multiple_of(b * SUBLANES, SUBLANES), z_rows)],
                                           sems.at[2])
                cp.start()
                cp.wait()
                return carry

            tails = []
            for e in range(N_EXPERTS):
                start, length = tails_ref[e], tails_ref[N_EXPERTS + e]
                whole = length // z_rows
                lax.fori_loop(0, whole, lambda b, c, s=start: fill(s // SUBLANES + b * (z_rows // SUBLANES), c), 0)
                rest0 = start + whole * z_rows
                for pred, offset, rows in _pow2_pieces(length - whole * z_rows, z_rows // 2):
                    at = pl.multiple_of(rest0 + offset, SUBLANES)
                    tails.append((pred, pltpu.make_async_copy(zero_scr.at[pl.ds(0, rows)],
                                                              buf_ref.at[pl.ds(at, rows)], sems.at[2])))
            _start_all(tails)
            _wait_all(tails)
            lax.fori_loop(tails_ref[2 * N_EXPERTS] // z_rows, buf_ref.shape[0] // z_rows,
                          lambda b, c: fill(b * (z_rows // SUBLANES), c), 0)

    pos = post_ref[...].astype(I32)
    j = lax.broadcasted_iota(I32, (r_rows, tm), 0)
    onehot = jnp.zeros((r_rows, tm), F32)
    for k in range(TOP_K):
        onehot = jnp.where(j == pos[k:k + 1, :], 1.0, onehot)
    sorted_scr[slot] = _pack_halves(_bdot(onehot, h2_ref[...]))

    _start_all(_segment_copies(tbl_ref, tile0 + i, sorted_scr.at[slot], buf_ref, sems.at[slot], True, tm))

    @pl.when(i > 0)
    def _():
        _wait_rows(tbl_ref[totals_at + tile0 + i - 1], sorted_scr.at[1 - slot], buf_ref, sems.at[1 - slot], True)

    @pl.when(i == pl.num_programs(0) - 1)
    def _():
        _wait_rows(tbl_ref[totals_at + tile0 + i], sorted_scr.at[slot], buf_ref, sems.at[slot], True)


def _dispatch(tbl, tails, post, h2, buf, buf_rows, tm, tile0, totals_at):
    n_tok, d = h2.shape
    r_rows = TOP_K * tm + N_EXPERTS * SUBLANES
    first = buf is None
    grid_spec = pltpu.PrefetchScalarGridSpec(
        num_scalar_prefetch=2,
        grid=(n_tok // tm,),
        in_specs=[pl.BlockSpec((SUBLANES, tm), lambda i, *_: (0, i)),
                  pl.BlockSpec((tm, d), lambda i, *_: (i, 0))]
                 + ([] if first else [pl.BlockSpec(memory_space=pl.ANY)]),
        out_specs=pl.BlockSpec(memory_space=pl.ANY),
        scratch_shapes=[pltpu.VMEM((2, r_rows, d // 2), U32),
                        pltpu.VMEM((EXPERT_BLOCK // 2, d // 2), U32),
                        pltpu.SemaphoreType.DMA((3,))],
    )
    return pl.pallas_call(
        functools.partial(_dispatch_body, tile0=tile0, totals_at=totals_at, zero_tails=first),
        grid_spec=grid_spec,
        out_shape=jax.ShapeDtypeStruct((buf_rows, d // 2), U32),
        input_output_aliases={} if first else {4: 0},
        compiler_params=pltpu.CompilerParams(dimension_semantics=("arbitrary",), has_side_effects=True,
                                             vmem_limit_bytes=VMEM_LIMIT),
        name="dispatch",
    )(tbl, tails, post, h2, *([] if first else [buf]))


def _expert_body(be_ref, nv_ref, x_ref, wg_ref, bg_ref, wu_ref, bu_ref, wd_ref, bd_ref, o_ref,
                 wg_s, wu_s, wd_s):
    i = pl.program_id(0)

    @pl.when(i < nv_ref[0])
    def _():
        @pl.when((i == 0) | (be_ref[i] != be_ref[jnp.maximum(i - 1, 0)]))
        def _():
            wg_s[...] = wg_ref[0].astype(BF16)
            wu_s[...] = wu_ref[0].astype(BF16)
            wd_s[...] = wd_ref[0].astype(BF16)

        x = _unpack_halves(x_ref[...])
        gate = jnp.dot(x, wg_s[...], preferred_element_type=F32) + bg_ref[0]
        up = jnp.dot(x, wu_s[...], preferred_element_type=F32) + bu_ref[0]
        gate = jnp.minimum(gate, SWIGLU_LIMIT)
        up = jnp.clip(up, -SWIGLU_LIMIT, SWIGLU_LIMIT)
        glu = gate * jax.nn.sigmoid(SWIGLU_ALPHA * gate)
        out = _bdot((up + 1.0) * glu, wd_s[...]) + bd_ref[0]
        o_ref[...] = _pack_halves(out.astype(BF16).astype(F32))

    @pl.when(i >= nv_ref[0])
    def _():
        o_ref[...] = jnp.zeros(o_ref.shape, U32)


def _experts(block_e, n_valid, xs, w_gate, b_gate, w_up, b_up, w_down, b_down):
    m_pad, d_half = xs.shape
    d = 2 * d_half
    n_blocks = m_pad // EXPERT_BLOCK
    f = w_gate.shape[2]
    blk = lambda i, be, nv: (jnp.maximum(jnp.minimum(i, nv[0] - 1), 0), 0)
    wspec = lambda a, b: pl.BlockSpec((1, a, b), lambda i, be, nv: (be[i], 0, 0))
    grid_spec = pltpu.PrefetchScalarGridSpec(
        num_scalar_prefetch=2,
        grid=(n_blocks,),
        in_specs=[pl.BlockSpec((EXPERT_BLOCK, d_half), blk),
                  wspec(d, f), wspec(1, f), wspec(d, f), wspec(1, f), wspec(f, d), wspec(1, d)],
        out_specs=pl.BlockSpec((EXPERT_BLOCK, d_half), lambda i, be, nv: (i, 0)),
        scratch_shapes=[pltpu.VMEM((d, f), BF16), pltpu.VMEM((d, f), BF16), pltpu.VMEM((f, d), BF16)],
    )
    return pl.pallas_call(
        _expert_body,
        grid_spec=grid_spec,
        out_shape=jax.ShapeDtypeStruct((m_pad, d_half), U32),
        compiler_params=_params(("arbitrary",)),
        name="experts",
    )(block_e, n_valid, xs, w_gate, b_gate[:, None, :], w_up, b_up[:, None, :], w_down, b_down[:, None, :])


def _combine_body(tbl_ref, ys_ref, pos_ref, rw_ref, x1_ref, g2_ref, fw_ref, o_ref, blk_scr, sems,
                  *, tile0, totals_at):
    step = pl.program_id(0) * pl.num_programs(1) + pl.program_id(1)
    n_steps = pl.num_programs(0) * pl.num_programs(1)
    slot = lax.rem(step, 2)
    tm = pos_ref.shape[0]
    r_rows = blk_scr.shape[1]

    def fetch(tile, into):
        return _segment_copies(tbl_ref, tile0 + tile, blk_scr.at[into], ys_ref, sems.at[into], False, tm)

    @pl.when(step == 0)
    def _():
        blk_scr[...] = jnp.zeros(blk_scr.shape, U32)
        _start_all(fetch(0, 0))

    @pl.when(step + 1 < n_steps)
    def _():
        _start_all(fetch(step + 1, 1 - slot))

    _wait_rows(tbl_ref[totals_at + tile0 + step], blk_scr.at[slot], ys_ref, sems.at[slot], False)

    pos = pos_ref[...].astype(I32)
    rw = rw_ref[...]
    j = lax.broadcasted_iota(I32, (tm, r_rows), 1)
    pw = jnp.zeros((tm, r_rows), F32)
    for k in range(TOP_K):
        pw = jnp.where(j == pos[:, k:k + 1], rw[:, k:k + 1], pw)
    ffn = jnp.dot(pw.astype(BF16), _unpack_halves(blk_scr[slot]), preferred_element_type=F32)
    y = x1_ref[...] + g2_ref[...] * ffn.reshape(x1_ref.shape)
    o_ref[...] = y * lax.rsqrt(jnp.mean(y * y, axis=-1, keepdims=True) + NORM_EPS) * fw_ref[...]


def _combine(tbl, ys, pos, rw, x1, g2, final_w, sb, tb, tile0, totals_at):
    n_seq, seq_len, d = x1.shape
    tm = sb * tb
    nt = seq_len // tb
    r_rows = TOP_K * tm + N_EXPERTS * SUBLANES
    row = lambda s, t: s * nt + t
    grid_spec = pltpu.PrefetchScalarGridSpec(
        num_scalar_prefetch=1,
        grid=(n_seq // sb, nt),
        in_specs=[pl.BlockSpec(memory_space=pl.ANY),
                  pl.BlockSpec((tm, ROUTER_LANES), lambda s, t, *_: (row(s, t), 0)),
                  pl.BlockSpec((tm, ROUTER_LANES), lambda s, t, *_: (row(s, t), 0)),
                  pl.BlockSpec((sb, tb, d), lambda s, t, *_: (s, t, 0)),
                  pl.BlockSpec((sb, 1, d), lambda s, t, *_: (s, 0, 0)),
                  pl.BlockSpec((1, d), lambda s, t, *_: (0, 0))],
        out_specs=pl.BlockSpec((sb, tb, d), lambda s, t, *_: (s, t, 0)),
        scratch_shapes=[pltpu.VMEM((2, r_rows, d // 2), U32), pltpu.SemaphoreType.DMA((2,))],
    )
    return pl.pallas_call(
        functools.partial(_combine_body, tile0=tile0, totals_at=totals_at),
        grid_spec=grid_spec,
        out_shape=jax.ShapeDtypeStruct((n_seq, seq_len, d), F32),
        compiler_params=_params(("arbitrary", "arbitrary")),
        name="combine",
    )(tbl, ys, pos, rw, x1, g2, final_w)


def _pad_state(state, rows):
    return jnp.pad(state, ((0, 0), (rows - state.shape[1], 0), (0, 0)))


def kernel(x_prompt, x_sample, c_prompt, c_sample, state_gdn_conv, state_gdn_rec, state_sc_conv, w_ada, b_ada,
           norm1_w, w_in, gdn_conv_w, gdn_a_log, gdn_dt_bias, gdn_norm_w, w_branch_a, sc_conv_w, w_branch_b,
           w_out, norm2_w, w_router, b_router, w_gate, b_gate, w_up, b_up, w_down, b_down, final_norm_w):
    assert w_ada.shape[0] == 1, "single-layer trunk"
    d = D_MODEL
    bp, tp, _ = x_prompt.shape
    bs, ts, _ = x_sample.shape
    n_p, n_s = bp * tp, bs * ts
    n_tok = n_p + n_s

    w_in0 = w_in[0]
    g_lo, g_hi = QKV_W + d, QKV_W + d + N_GATE_COLS
    w_qkvz = w_in0[:, :g_lo].astype(BF16)
    w5 = w_in0[:, g_hi:].astype(BF16)
    w_all = jnp.concatenate([w_qkvz, w5], axis=1)
    w_g = w_in0[:, g_lo:g_hi].astype(BF16)
    w_gt = w_g.T
    zeros_h = jnp.zeros((GDN_HEADS,), F32)
    p_row = jnp.stack([jnp.concatenate([zeros_h, gdn_a_log[0]]), jnp.concatenate([zeros_h, gdn_dt_bias[0]])])
    p_col = p_row.T
    w_a = w_branch_a[0].astype(BF16)
    w_b = w_branch_b[0].astype(BF16)
    w_o = w_out[0].astype(BF16)
    w_r = jnp.pad(w_router[0], ((0, 0), (0, ROUTER_LANES - N_EXPERTS)))
    w_r_hi = w_r.astype(BF16)
    w_r = jnp.stack([w_r_hi, (w_r - w_r_hi.astype(F32)).astype(BF16)])
    b_r = jnp.pad(b_router[0], (0, ROUTER_LANES - N_EXPERTS), constant_values=-jnp.inf).reshape(1, ROUTER_LANES)

    n_c = bp + bs
    c_rows = -(-n_c // 16) * 16
    c_all = jnp.pad(jnp.concatenate([c_prompt, c_sample], axis=0), ((0, c_rows - n_c), (0, 0)))
    ada = _ada(c_all, w_ada[0], b_ada[0])

    def ada_parts(lo, hi):
        return [ada[lo:hi, j * d:(j + 1) * d].reshape(hi - lo, 1, d) for j in range(6)]

    groups = [
        dict(x=x_prompt, ada=ada_parts(0, bp), n_seq=bp, seq_len=tp,
             conv0=jnp.zeros((bp, SUBLANES, QKV_W), F32),
             rec0=jnp.zeros((bp, GDN_HEADS, GDN_DK, GDN_DV), F32),
             sc0=jnp.zeros((bp, SUBLANES, d), F32)),
        dict(x=x_sample, ada=ada_parts(bp, n_c), n_seq=bs, seq_len=ts,
             conv0=_pad_state(state_gdn_conv[0], SUBLANES),
             rec0=state_gdn_rec[0],
             sc0=_pad_state(state_sc_conv[0], SUBLANES)),
    ]

    counts = jnp.zeros((SUBLANES, ROUTER_LANES), F32)
    for g in groups:
        sh1, sc1, g1, sh2, sc2, g2 = g["ada"]
        n_seq, seq_len = g["n_seq"], g["seq_len"]
        wide = seq_len > SUBLANES
        on, rec, conv_tail, *p5 = _gdn(g["x"], sc1, sh1, norm1_w, w_all if wide else w_qkvz, w_g, w_gt, p_row, p_col,
                                       g["conv0"], g["rec0"], gdn_conv_w[0], gdn_norm_w)
        x1, h2, pos, rw, counts, tail, post, tbl = _merge(g["x"], on, p5[0] if wide else None, sc1, sh1, norm1_w, w5,
                                                          g["sc0"], g1, sc2, sh2, w_a, w_b, w_o, sc_conv_w[0],
                                                          norm2_w, w_r, b_r, counts)
        sb, tb = _route_tile(n_seq, seq_len)
        g.update(x1=x1, h2=h2, pos=pos, rw=rw, post=post, tbl=tbl, rec=rec, g2=g2, sb=sb, tb=tb, tm=sb * tb,
                 rows_after=counts[0, :N_EXPERTS].astype(I32),
                 new_conv=conv_tail[:, SUBLANES - (GDN_CONV - 1):, :],
                 new_sc=tail.reshape(n_seq, -1, d)[:, -(SC_CONV - 1):, :])

    rows_e = counts[0, :N_EXPERTS].astype(I32)
    padded = (rows_e + EXPERT_BLOCK - 1) // EXPERT_BLOCK * EXPERT_BLOCK
    pad_end = jnp.cumsum(padded)
    expert0 = (pad_end - padded).astype(I32)
    tbl_all = jnp.concatenate([g["tbl"] for g in groups], axis=0).reshape(-1, SUBLANES, ROUTER_LANES)
    n_tiles = tbl_all.shape[0]
    tbl_all = tbl_all[:, :3, :N_EXPERTS].at[:, 2, :].add(expert0[None, :])
    totals_at = n_tiles * 3 * N_EXPERTS
    tbl_all = jnp.concatenate([tbl_all.reshape(-1), jnp.sum(tbl_all[:, 1, :], axis=1)]).astype(I32)
    rows_first = groups[0]["rows_after"]
    tails = jnp.concatenate([expert0 + rows_first, padded - rows_first, pad_end[-1:]]).astype(I32)
    max_rows = n_tok * TOP_K + n_tiles * N_EXPERTS * (SUBLANES - 1)
    n_blocks = -(-max_rows // EXPERT_BLOCK) + N_EXPERTS
    block_start = jnp.arange(n_blocks, dtype=I32) * EXPERT_BLOCK
    block_e = jnp.minimum(jnp.sum((pad_end[None, :] <= block_start[:, None]).astype(I32), axis=1), N_EXPERTS - 1)
    n_valid = (pad_end[-1:] // EXPERT_BLOCK).astype(I32)

    buf = None
    tile0 = 0
    for g in groups:
        g["tile0"] = tile0
        buf = _dispatch(tbl_all, tails, g["post"], g["h2"], buf, n_blocks * EXPERT_BLOCK, g["tm"], tile0, totals_at)
        tile0 += g["h2"].shape[0] // g["tm"]
    ys = _experts(block_e, n_valid, buf, w_gate[0], b_gate[0], w_up[0], b_up[0], w_down[0], b_down[0])
    outs = []
    for g in groups:
        n_seq, seq_len = g["n_seq"], g["seq_len"]
        x1 = g["x1"].reshape(n_seq, seq_len, d)
        outs.append(_combine(tbl_all, ys, g["pos"], g["rw"], x1, g["g2"], final_norm_w.reshape(1, d),
                             g["sb"], g["tb"], g["tile0"], totals_at))

    gp, gs = groups
    return (outs[0], outs[1], gp["new_conv"][None], gp["rec"][None], gp["new_sc"][None],
            gs["new_conv"][None], gs["rec"][None], gs["new_sc"][None])
```
